```python
import math
import jax, jax.numpy as jnp
from jax import lax
import numpy as np

D_MODEL = 1024
BATCH = 8
SEQ = 2048
DEPTH = 2

HEAD_DIM = 64
ROPE_THETA = 10000.0
Q_BLOCK = 128
EPS = 1e-6
NEG = -1e30
FORCE = 1e4

NSA_HEADS = 4
NSA_CMP_LEN = 32
NSA_CMP_STRIDE = 16
NSA_SEL_LEN = 64
NSA_TOP_N = 16
NSA_WINDOW = 512

DIFF_HEADS = 4
DIFF_QK_DIM = 32
DIFF_V_DIM = 64

MLA_HEADS = 4
MLA_Q_RANK = 256
MLA_KV_RANK = 128
MLA_NOPE_DIM = 64
MLA_ROPE_DIM = 32
MLA_V_DIM = 64

SWA_HEADS = 4
SWA_KV_HEADS = 2
SWA_WINDOW = 128

MIX_WIDTH = NSA_HEADS * HEAD_DIM + DIFF_HEADS * DIFF_V_DIM + MLA_HEADS * MLA_V_DIM + SWA_HEADS * HEAD_DIM

NSA_COLS = NSA_HEADS * HEAD_DIM + 6 * HEAD_DIM + 3 * NSA_HEADS
DIFF_COLS = 2 * DIFF_HEADS * 2 * DIFF_QK_DIM + DIFF_HEADS * DIFF_V_DIM
MLA_COLS = MLA_Q_RANK + MLA_KV_RANK + MLA_ROPE_DIM
SWA_COLS = SWA_HEADS * HEAD_DIM + 2 * SWA_KV_HEADS * HEAD_DIM
IN_COLS = NSA_COLS + DIFF_COLS + MLA_COLS + SWA_COLS

PEER_HEADS = 8
PEER_N_KEYS = 128
PEER_TOPK = 16
PEER_QUERY_DIM = 256
PEER_EXPERTS = PEER_N_KEYS * PEER_N_KEYS
PEER_CHUNK = 128

kernel_name = "hybrid_nsa_diff_mla_swa_peer_adaln"


def rms_norm(x, g):
    xf = x.astype(jnp.float32)
    y = xf * lax.rsqrt(jnp.mean(xf * xf, axis=-1, keepdims=True) + EPS)
    return (y * g.astype(jnp.float32)).astype(x.dtype)


def split_cols(t, sizes):
    idx = [int(i) for i in np.cumsum(sizes)[:-1]]
    return jnp.split(t, idx, axis=-1)


def rope_tables(seq, dim):
    inv = 1.0 / (ROPE_THETA ** (jnp.arange(0, dim, 2, dtype=jnp.float32) / dim))
    ang = jnp.arange(seq, dtype=jnp.float32)[:, None] * inv[None, :]
    return jnp.cos(ang), jnp.sin(ang)


def apply_rope(x, cos, sin):
    x1, x2 = jnp.split(x, 2, axis=-1)
    c = cos[None, :, None, :].astype(x.dtype)
    s = sin[None, :, None, :].astype(x.dtype)
    return jnp.concatenate([x1 * c - x2 * s, x1 * s + x2 * c], axis=-1)


def to_blocks(t):
    b, s = t.shape[:2]
    t = t.reshape((b, s // Q_BLOCK, Q_BLOCK) + t.shape[2:])
    return jnp.moveaxis(t, 1, 0)


def from_blocks(t):
    t = jnp.moveaxis(t, 0, 1)
    return t.reshape((t.shape[0], t.shape[1] * t.shape[2]) + t.shape[3:])


def band_keys(t, window):
    s = t.shape[1]
    tp = jnp.pad(t, [(0, 0), (window, 0)] + [(0, 0)] * (t.ndim - 2))
    idx = jnp.arange(s // Q_BLOCK)[:, None] * Q_BLOCK + jnp.arange(window + Q_BLOCK)[None, :]
    return jnp.moveaxis(tp[:, idx], 1, 0)


def banded_attention(q, k, v, window, sinks=None):
    b, s, h, d = q.shape
    g = k.shape[2]
    r = h // g
    scale = d ** -0.5
    nblk = s // Q_BLOCK
    qb = to_blocks(q.reshape(b, s, g, r, d))
    kb = band_keys(k, window)
    vb = band_keys(v, window)

    def blk(args):
        qq, kk, vv, i = args
        qpos = i * Q_BLOCK + jnp.arange(Q_BLOCK)
        kpos = i * Q_BLOCK - window + jnp.arange(window + Q_BLOCK)
        dist = qpos[:, None] - kpos[None, :]
        mask = (dist >= 0) & (dist < window) & (kpos[None, :] >= 0)
        sc = jnp.einsum('bqgrd,bkgd->bgrqk', qq, kk).astype(jnp.float32) * scale
        sc = jnp.where(mask, sc, NEG)
        if sinks is None:
            p = jax.nn.softmax(sc, axis=-1)
        else:
            sk = jnp.broadcast_to(sinks.astype(jnp.float32).reshape(1, g, r, 1, 1), sc.shape[:-1] + (1,))
            p = jax.nn.softmax(jnp.concatenate([sc, sk], axis=-1), axis=-1)[..., :-1]
        return jnp.einsum('bgrqk,bkgd->bqgrd', p.astype(vv.dtype), vv)

    o = lax.map(blk, (qb, kb, vb, jnp.arange(nblk)))
    return from_blocks(o).reshape(b, s, h, d)


def nsa_mixer(cols, pos_k, pos_v, wk, wv, cos, sin):
    b, s, _ = cols.shape
    h, d = NSA_HEADS, HEAD_DIM
    q, kc, vc, ksl, vsl, kw, vw, gt = split_cols(cols, [h * d, d, d, d, d, d, d, 3 * h])
    q = apply_rope(q.reshape(b, s, h, d), cos, sin)
    kc = apply_rope(kc[:, :, None], cos, sin)[:, :, 0]
    ksl = apply_rope(ksl[:, :, None], cos, sin)[:, :, 0]
    kw = apply_rope(kw[:, :, None], cos, sin)
    gates = jax.nn.sigmoid(gt.reshape(b, s, h, 3))
    scale = d ** -0.5
    tpos = jnp.arange(s)

    n_c = (s - NSA_CMP_LEN) // NSA_CMP_STRIDE + 1
    cidx = jnp.arange(n_c)[:, None] * NSA_CMP_STRIDE + jnp.arange(NSA_CMP_LEN)[None, :]
    k_cmp = (kc[:, cidx] + pos_k).reshape(b, n_c, NSA_CMP_LEN * d) @ wk
    v_cmp = (vc[:, cidx] + pos_v).reshape(b, n_c, NSA_CMP_LEN * d) @ wv
    cmask = cidx[:, -1][None, :] <= tpos[:, None]
    sc = jnp.einsum('bqhd,bcd->bhqc', q, k_cmp).astype(jnp.float32) * scale
    p_cmp = jax.nn.softmax(jnp.where(cmask, sc, NEG), axis=-1) * cmask
    o_cmp = jnp.einsum('bhqc,bcd->bqhd', p_cmp.astype(v_cmp.dtype), v_cmp)

    n_sel = s // NSA_SEL_LEN
    top_n = min(NSA_TOP_N, n_sel)
    overlap = jax.nn.one_hot(cidx // NSA_SEL_LEN, n_sel, dtype=jnp.float32).mean(axis=1)
    imp = jnp.einsum('bhqc,cj->bqj', p_cmp, overlap)
    qblk = tpos // NSA_SEL_LEN
    jj = jnp.arange(n_sel)
    allowed = jj[None, :] <= qblk[:, None]
    forced = (jj[None, :] == 0) | (jj[None, :] == qblk[:, None]) | (jj[None, :] == qblk[:, None] - 1)
    imp = jnp.where(allowed, jnp.where(forced, FORCE, imp), NEG)
    _, sel_idx = lax.top_k(imp, top_n)
    sel_valid = sel_idx <= qblk[None, :, None]
    kb = ksl.reshape(b, n_sel, NSA_SEL_LEN, d)
    vb = vsl.reshape(b, n_sel, NSA_SEL_LEN, d)

    def sel_block(args):
        qq, ii, ok, bi = args
        qpos = bi * Q_BLOCK + jnp.arange(Q_BLOCK)
        kg = jax.vmap(lambda kk, ix: kk[ix])(kb, ii).reshape(b, Q_BLOCK, top_n * NSA_SEL_LEN, d)
        vg = jax.vmap(lambda vv, ix: vv[ix])(vb, ii).reshape(b, Q_BLOCK, top_n * NSA_SEL_LEN, d)
        kpos = (ii[..., None] * NSA_SEL_LEN + jnp.arange(NSA_SEL_LEN)).reshape(b, Q_BLOCK, -1)
        m = (kpos <= qpos[None, :, None]) & jnp.repeat(ok, NSA_SEL_LEN, axis=-1)
        ss = jnp.einsum('bqhd,bqkd->bqhk', qq, kg).astype(jnp.float32) * scale
        pp = jax.nn.softmax(jnp.where(m[:, :, None, :], ss, NEG), axis=-1)
        return jnp.einsum('bqhk,bqkd->bqhd', pp.astype(vg.dtype), vg)

    o_sel = from_blocks(lax.map(sel_block, (to_blocks(q), to_blocks(sel_idx), to_blocks(sel_valid),
                                           jnp.arange(s // Q_BLOCK))))

    o_win = banded_attention(q, kw, vw[:, :, None], NSA_WINDOW)

    o = gates[..., 0:1] * o_cmp + gates[..., 1:2] * o_sel + gates[..., 2:3] * o_win
    return o.reshape(b, s, h * d)


def diff_mixer(cols, layer, lq1, lk1, lq2, lk2, sub_g, cos, sin):
    b, s, _ = cols.shape
    h, dq, dv = DIFF_HEADS, DIFF_QK_DIM, DIFF_V_DIM
    q, k, v = split_cols(cols, [h * 2 * dq, h * 2 * dq, h * dv])
    q = apply_rope(q.reshape(b, s, h * 2, dq), cos, sin).reshape(b, s, h, 2, dq)
    k = apply_rope(k.reshape(b, s, h * 2, dq), cos, sin).reshape(b, s, h, 2, dq)
    v = v.reshape(b, s, h, dv)
    lam_init = 0.8 - 0.6 * math.exp(-0.3 * layer)
    f32 = jnp.float32
    lam = (jnp.exp(jnp.sum(lq1.astype(f32) * lk1.astype(f32)))
           - jnp.exp(jnp.sum(lq2.astype(f32) * lk2.astype(f32))) + lam_init)
    scale = dq ** -0.5
    kpos = jnp.arange(s)

    def blk(args):
        qq, i = args
        qpos = i * Q_BLOCK + jnp.arange(Q_BLOCK)
        mask = kpos[None, :] <= qpos[:, None]
        sc = jnp.einsum('bqhcd,bkhcd->bhcqk', qq, k).astype(f32) * scale
        p = jax.nn.softmax(jnp.where(mask, sc, NEG), axis=-1)
        a = p[:, :, 0] - lam * p[:, :, 1]
        return jnp.einsum('bhqk,bkhd->bqhd', a.astype(v.dtype), v)

    o = from_blocks(lax.map(blk, (to_blocks(q), jnp.arange(s // Q_BLOCK))))
    o = rms_norm(o, sub_g) * (1.0 - lam_init)
    return o.reshape(b, s, h * dv)


def mla_mixer(cols, q_norm_g, w_uq, kv_norm_g, w_ukv, cos, sin):
    b, s, _ = cols.shape
    h = MLA_HEADS
    c_q, c_kv, k_rope = split_cols(cols, [MLA_Q_RANK, MLA_KV_RANK, MLA_ROPE_DIM])
    q = (rms_norm(c_q, q_norm_g) @ w_uq).reshape(b, s, h, MLA_NOPE_DIM + MLA_ROPE_DIM)
    q_nope, q_rope = split_cols(q, [MLA_NOPE_DIM, MLA_ROPE_DIM])
    q_rope = apply_rope(q_rope, cos, sin)
    kv = (rms_norm(c_kv, kv_norm_g) @ w_ukv).reshape(b, s, h, MLA_NOPE_DIM + MLA_V_DIM)
    k_nope, v = split_cols(kv, [MLA_NOPE_DIM, MLA_V_DIM])
    k_rope = apply_rope(k_rope[:, :, None], cos, sin)[:, :, 0]
    scale = (MLA_NOPE_DIM + MLA_ROPE_DIM) ** -0.5
    kpos = jnp.arange(s)

    def blk(args):
        qn, qr, i = args
        qpos = i * Q_BLOCK + jnp.arange(Q_BLOCK)
        mask = kpos[None, :] <= qpos[:, None]
        sc = (jnp.einsum('bqhd,bkhd->bhqk', qn, k_nope)
              + jnp.einsum('bqhd,bkd->bhqk', qr, k_rope)).astype(jnp.float32) * scale
        p = jax.nn.softmax(jnp.where(mask, sc, NEG), axis=-1)
        return jnp.einsum('bhqk,bkhd->bqhd', p.astype(v.dtype), v)

    o = from_blocks(lax.map(blk, (to_blocks(q_nope), to_blocks(q_rope), jnp.arange(s // Q_BLOCK))))
    return o.reshape(b, s, h * MLA_V_DIM)


def swa_mixer(cols, sinks, cos, sin):
    b, s, _ = cols.shape
    d = HEAD_DIM
    q, k, v = split_cols(cols, [SWA_HEADS * d, SWA_KV_HEADS * d, SWA_KV_HEADS * d])
    q = apply_rope(q.reshape(b, s, SWA_HEADS, d), cos, sin)
    k = apply_rope(k.reshape(b, s, SWA_KV_HEADS, d), cos, sin)
    v = v.reshape(b, s, SWA_KV_HEADS, d)
    o = banded_attention(q, k, v, SWA_WINDOW, sinks)
    return o.reshape(b, s, SWA_HEADS * d)


def hybrid_mixer(h, layer, w_in, nsa_cmp_pos_k, nsa_cmp_pos_v, nsa_cmp_wk, nsa_cmp_wv,
                 diff_lam_q1, diff_lam_k1, diff_lam_q2, diff_lam_k2, diff_sub_g,
                 mla_q_norm_g, mla_w_uq, mla_kv_norm_g, mla_w_ukv, swa_sinks, w_out):
    s = h.shape[1]
    cols = h @ w_in
    nsa_c, diff_c, mla_c, swa_c = split_cols(cols, [NSA_COLS, DIFF_COLS, MLA_COLS, SWA_COLS])
    cos_h, sin_h = rope_tables(s, HEAD_DIM)
    cos_d, sin_d = rope_tables(s, DIFF_QK_DIM)
    cos_m, sin_m = rope_tables(s, MLA_ROPE_DIM)
    o_a = nsa_mixer(nsa_c, nsa_cmp_pos_k, nsa_cmp_pos_v, nsa_cmp_wk, nsa_cmp_wv, cos_h, sin_h)
    o_b = diff_mixer(diff_c, layer, diff_lam_q1, diff_lam_k1, diff_lam_q2, diff_lam_k2, diff_sub_g, cos_d, sin_d)
    o_c = mla_mixer(mla_c, mla_q_norm_g, mla_w_uq, mla_kv_norm_g, mla_w_ukv, cos_m, sin_m)
    o_d = swa_mixer(swa_c, swa_sinks, cos_h, sin_h)
    o = jnp.concatenate([o_a, o_b, o_c, o_d], axis=-1)
    return o @ w_out


def peer_ffn(h, w_q, sub_k1, sub_k2, u, v):
    b, s, d = h.shape
    t = b * s
    hk = PEER_HEADS * PEER_TOPK
    hf = h.reshape(t, d)
    q = (hf @ w_q).reshape(t, PEER_HEADS, 2, PEER_QUERY_DIM // 2)
    s1 = jnp.einsum('thd,kd->thk', q[:, :, 0], sub_k1).astype(jnp.float32)
    s2 = jnp.einsum('thd,kd->thk', q[:, :, 1], sub_k2).astype(jnp.float32)
    v1, i1 = lax.top_k(s1, PEER_TOPK)
    v2, i2 = lax.top_k(s2, PEER_TOPK)
    cand = (v1[..., :, None] + v2[..., None, :]).reshape(t, PEER_HEADS, PEER_TOPK * PEER_TOPK)
    cid = (i1[..., :, None] * PEER_N_KEYS + i2[..., None, :]).reshape(t, PEER_HEADS, PEER_TOPK * PEER_TOPK)
    top_s, top_j = lax.top_k(cand, PEER_TOPK)
    eid = jnp.take_along_axis(cid, top_j, axis=-1)
    gate = jax.nn.softmax(top_s, axis=-1)
    nch = t // PEER_CHUNK

    def chunk(args):
        hc, ec, gc = args
        act = jax.nn.gelu(jnp.einsum('cd,ced->ce', hc, u[ec]).astype(jnp.float32))
        w = (gc * act).astype(hc.dtype)
        return jnp.einsum('ce,ced->cd', w, v[ec])

    out = lax.map(chunk, (hf.reshape(nch, PEER_CHUNK, d), eid.reshape(nch, PEER_CHUNK, hk),
                          gate.reshape(nch, PEER_CHUNK, hk)))
    return out.reshape(b, s, d)


def setup_inputs(seed: int = 0) -> dict:
    key = jax.random.key(seed)
    ks = iter(jax.random.split(key, 40))

    def nrm(shape, scale):
        return jax.random.normal(next(ks), shape, jnp.float32) * scale

    def gain(shape):
        return 1.0 + 0.02 * jax.random.normal(next(ks), shape, jnp.float32)

    L, D = DEPTH, D_MODEL
    return {
        "x": nrm((BATCH, SEQ, D), 1.0),
        "c": nrm((BATCH, D), 1.0),
        "ada_w": nrm((L, D, 6 * D), 0.5 * D ** -0.5),
        "ada_b": nrm((L, 6 * D), 0.02),
        "norm_mix_g": gain((L, D)),
        "norm_ffn_g": gain((L, D)),
        "w_in": nrm((L, D, IN_COLS), D ** -0.5),
        "nsa_cmp_pos_k": nrm((L, NSA_CMP_LEN, HEAD_DIM), 0.1),
        "nsa_cmp_pos_v": nrm((L, NSA_CMP_LEN, HEAD_DIM), 0.1),
        "nsa_cmp_wk": nrm((L, NSA_CMP_LEN * HEAD_DIM, HEAD_DIM), (NSA_CMP_LEN * HEAD_DIM) ** -0.5),
        "nsa_cmp_wv": nrm((L, NSA_CMP_LEN * HEAD_DIM, HEAD_DIM), (NSA_CMP_LEN * HEAD_DIM) ** -0.5),
        "diff_lam_q1": nrm((L, DIFF_QK_DIM), 0.1),
        "diff_lam_k1": nrm((L, DIFF_QK_DIM), 0.1),
        "diff_lam_q2": nrm((L, DIFF_QK_DIM), 0.1),
        "diff_lam_k2": nrm((L, DIFF_QK_DIM), 0.1),
        "diff_sub_g": gain((L, DIFF_V_DIM)),
        "mla_q_norm_g": gain((L, MLA_Q_RANK)),
        "mla_w_uq": nrm((L, MLA_Q_RANK, MLA_HEADS * (MLA_NOPE_DIM + MLA_ROPE_DIM)), MLA_Q_RANK ** -0.5),
        "mla_kv_norm_g": gain((L, MLA_KV_RANK)),
        "mla_w_ukv": nrm((L, MLA_KV_RANK, MLA_HEADS * (MLA_NOPE_DIM + MLA_V_DIM)), MLA_KV_RANK ** -0.5),
        "swa_sinks": nrm((L, SWA_HEADS), 0.5),
        "w_out": nrm((L, MIX_WIDTH, D), MIX_WIDTH ** -0.5),
        "peer_w_q": nrm((L, D, PEER_HEADS * PEER_QUERY_DIM), D ** -0.5),
        "peer_sub_k1": nrm((L, PEER_N_KEYS, PEER_QUERY_DIM // 2), (PEER_QUERY_DIM // 2) ** -0.5),
        "peer_sub_k2": nrm((L, PEER_N_KEYS, PEER_QUERY_DIM // 2), (PEER_QUERY_DIM // 2) ** -0.5),
        "peer_u": nrm((L, PEER_EXPERTS, D), D ** -0.5),
        "peer_v": nrm((L, PEER_EXPERTS, D), PEER_HEADS ** -0.5),
        "final_g": gain((D,)),
    }


def reference(x, c, ada_w, ada_b, norm_mix_g, norm_ffn_g, w_in, nsa_cmp_pos_k, nsa_cmp_pos_v,
              nsa_cmp_wk, nsa_cmp_wv, diff_lam_q1, diff_lam_k1, diff_lam_q2, diff_lam_k2, diff_sub_g,
              mla_q_norm_g, mla_w_uq, mla_kv_norm_g, mla_w_ukv, swa_sinks, w_out,
              peer_w_q, peer_sub_k1, peer_sub_k2, peer_u, peer_v, final_g):
    for l in range(DEPTH):
        mod = jax.nn.silu(c) @ ada_w[l] + ada_b[l]
        sh1, sc1, g1, sh2, sc2, g2 = [m[:, None, :] for m in jnp.split(mod, 6, axis=-1)]
        h = rms_norm(x, norm_mix_g[l]) * (1.0 + sc1) + sh1
        x = x + g1 * hybrid_mixer(h, l, w_in[l], nsa_cmp_pos_k[l], nsa_cmp_pos_v[l], nsa_cmp_wk[l], nsa_cmp_wv[l],
                                  diff_lam_q1[l], diff_lam_k1[l], diff_lam_q2[l], diff_lam_k2[l], diff_sub_g[l],
                                  mla_q_norm_g[l], mla_w_uq[l], mla_kv_norm_g[l], mla_w_ukv[l], swa_sinks[l], w_out[l])
        h = rms_norm(x, norm_ffn_g[l]) * (1.0 + sc2) + sh2
        x = x + g2 * peer_ffn(h, peer_w_q[l], peer_sub_k1[l], peer_sub_k2[l], peer_u[l], peer_v[l])
    return rms_norm(x, final_g)
```

```python
import functools
import math

import numpy as np
import jax
import jax.numpy as jnp
from jax import lax
from jax.experimental import pallas as pl
from jax.experimental.pallas import tpu as pltpu

F32 = jnp.float32
BF16 = jnp.bfloat16

HEAD_DIM = 64
ROPE_THETA = 10000.0
EPS = 1e-6
NEG = -1e30
FORCE = 1e4

NSA_HEADS = 4
NSA_CMP_LEN = 32
NSA_CMP_STRIDE = 16
NSA_SEL_LEN = 64
NSA_TOP_N = 16
NSA_WINDOW = 512

DIFF_HEADS = 4
DIFF_QK_DIM = 32
DIFF_V_DIM = 64

MLA_HEADS = 4
MLA_Q_RANK = 256
MLA_KV_RANK = 128
MLA_NOPE_DIM = 64
MLA_ROPE_DIM = 32
MLA_V_DIM = 64

SWA_HEADS = 4
SWA_KV_HEADS = 2
SWA_WINDOW = 128

PEER_HEADS = 8
PEER_N_KEYS = 128
PEER_TOPK = 16
PEER_QUERY_DIM = 256

LANES = 128
Q_TILE = 128
KV_TILE = 128
VMEM_LIMIT = 56 * 1024 * 1024


def _dot(a, b):
    return jnp.dot(a, b, preferred_element_type=F32)


def _dot_nt(a, b):
    return lax.dot_general(a, b, (((1,), (1,)), ((), ())), preferred_element_type=F32)


def _params(*sem):
    return pltpu.CompilerParams(dimension_semantics=sem, vmem_limit_bytes=VMEM_LIMIT)


def _rms(x, g):
    return x * lax.rsqrt(jnp.mean(x * x, axis=-1, keepdims=True) + EPS) * g


def _rot_idx(base, dim):
    half = dim // 2
    idx = np.concatenate([base + half + np.arange(half), base + np.arange(half)])
    sgn = np.concatenate([-np.ones(half), np.ones(half)])
    return idx, sgn


def _in_plan():
    d = HEAD_DIM
    nsa0 = 0
    nsa_cols = NSA_HEADS * d + 6 * d + 3 * NSA_HEADS
    diff0 = nsa0 + nsa_cols
    diff_cols = 2 * DIFF_HEADS * 2 * DIFF_QK_DIM + DIFF_HEADS * DIFF_V_DIM
    mla0 = diff0 + diff_cols
    mla_cols = MLA_Q_RANK + MLA_KV_RANK + MLA_ROPE_DIM
    swa0 = mla0 + mla_cols
    idx, sgn, off = [], [], {}

    def add(name, i, s=None):
        i = np.asarray(i, np.int64)
        s = np.ones(len(i)) if s is None else np.asarray(s, np.float64)
        pad = (-len(i)) % LANES
        off[name] = sum(len(a) for a in idx)
        idx.append(np.concatenate([i, np.zeros(pad, np.int64)]))
        sgn.append(np.concatenate([s, np.zeros(pad)]))

    def heads_rot(base, nheads, dim):
        ii, ss = zip(*[_rot_idx(base + h * dim, dim) for h in range(nheads)])
        return np.concatenate(ii), np.concatenate(ss)

    nq = nsa0 + np.arange(NSA_HEADS * d)
    add("nq", nq)
    add("nqr", *heads_rot(nsa0, NSA_HEADS, d))
    kb = nsa0 + NSA_HEADS * d
    kc, vc, ksl, vsl, kw, vw = [kb + j * d for j in range(6)]
    dup = lambda b: np.concatenate([b + np.arange(d), b + np.arange(d)])
    add("nk", np.concatenate([dup(kc), dup(ksl), dup(kw)]))
    kr = [_rot_idx(b, d) for b in (kc, kc, ksl, ksl, kw, kw)]
    add("nkr", np.concatenate([a for a, _ in kr]), np.concatenate([b for _, b in kr]))
    add("nv", np.concatenate([dup(vc), dup(vsl), dup(vw)]))
    add("ng", kb + 6 * d + np.arange(3 * NSA_HEADS))
    nqk = DIFF_HEADS * 2 * DIFF_QK_DIM
    add("dq", diff0 + np.arange(nqk))
    add("dqr", *heads_rot(diff0, 2 * DIFF_HEADS, DIFF_QK_DIM))
    add("dk", diff0 + nqk + np.arange(nqk))
    add("dkr", *heads_rot(diff0 + nqk, 2 * DIFF_HEADS, DIFF_QK_DIM))
    add("dv", diff0 + 2 * nqk + np.arange(DIFF_HEADS * DIFF_V_DIM))
    add("mcq", mla0 + np.arange(MLA_Q_RANK))
    add("mckv", mla0 + MLA_Q_RANK + np.arange(MLA_KV_RANK))
    kr0 = mla0 + MLA_Q_RANK + MLA_KV_RANK
    z64 = np.zeros(MLA_NOPE_DIM, np.int64)
    add("mkr", np.concatenate([z64, kr0 + np.arange(MLA_ROPE_DIM)]),
        np.concatenate([np.zeros(MLA_NOPE_DIM), np.ones(MLA_ROPE_DIM)]))
    ri, rs = _rot_idx(kr0, MLA_ROPE_DIM)
    add("mkrr", np.concatenate([z64, ri]), np.concatenate([np.zeros(MLA_NOPE_DIM), rs]))
    order = [0, 2, 1, 3]
    add("sq", np.concatenate([swa0 + h * d + np.arange(d) for h in order]))
    sr = [_rot_idx(swa0 + h * d, d) for h in order]
    add("sqr", np.concatenate([a for a, _ in sr]), np.concatenate([b for _, b in sr]))
    sk0 = swa0 + SWA_HEADS * d
    add("sk", sk0 + np.arange(SWA_KV_HEADS * d))
    add("skr", *heads_rot(sk0, SWA_KV_HEADS, d))
    add("sv", sk0 + SWA_KV_HEADS * d + np.arange(SWA_KV_HEADS * d))
    return np.concatenate(idx), np.concatenate(sgn), off


_IN_IDX, _IN_SGN, _OFF = _in_plan()
_NCOLS = len(_IN_IDX)


def _mla_plans():
    qd = MLA_NOPE_DIM + MLA_ROPE_DIM
    qi, qs, ri, rs = [], [], [], []
    for h in range(MLA_HEADS):
        b = h * qd
        qi += [b + np.arange(qd), np.zeros(LANES - qd, np.int64)]
        qs += [np.ones(qd), np.zeros(LANES - qd)]
        a, s = _rot_idx(b + MLA_NOPE_DIM, MLA_ROPE_DIM)
        ri += [np.zeros(MLA_NOPE_DIM, np.int64), a, np.zeros(LANES - qd, np.int64)]
        rs += [np.zeros(MLA_NOPE_DIM), s, np.zeros(LANES - qd)]
    kd = MLA_NOPE_DIM + MLA_V_DIM
    ki, ks, vi = [], [], []
    for h in range(MLA_HEADS):
        ki += [h * kd + np.arange(MLA_NOPE_DIM), np.zeros(LANES - MLA_NOPE_DIM, np.int64)]
        ks += [np.ones(MLA_NOPE_DIM), np.zeros(LANES - MLA_NOPE_DIM)]
        vi += [h * kd + MLA_NOPE_DIM + np.arange(MLA_V_DIM)]
    uq_idx = np.concatenate(qi + ri)
    uq_sgn = np.concatenate(qs + rs)
    ukv_idx = np.concatenate(ki + vi)
    ukv_sgn = np.concatenate(ks + [np.ones(MLA_HEADS * MLA_V_DIM)])
    return uq_idx, uq_sgn, ukv_idx, ukv_sgn


_UQ_IDX, _UQ_SGN, _UKV_IDX, _UKV_SGN = _mla_plans()


def _take_cols(w, idx, sgn):
    return (jnp.take(w, jnp.asarray(idx, jnp.int32), axis=1) * jnp.asarray(sgn, F32)[None, :]).astype(BF16)


def _rope_table(seq):
    def cs(dim):
        inv = 1.0 / (ROPE_THETA ** (jnp.arange(0, dim, 2, dtype=F32) / dim))
        ang = jnp.arange(seq, dtype=F32)[:, None] * inv[None, :]
        c, s = jnp.cos(ang), jnp.sin(ang)
        return jnp.concatenate([c, c], 1), jnp.concatenate([s, s], 1)
    ch, sh = cs(HEAD_DIM)
    cd, sd = cs(DIFF_QK_DIM)
    cm, sm = cs(MLA_ROPE_DIM)
    one = jnp.ones((seq, MLA_NOPE_DIM), F32)
    z64 = jnp.zeros((seq, MLA_NOPE_DIM), F32)
    z32 = jnp.zeros((seq, LANES - MLA_NOPE_DIM - MLA_ROPE_DIM), F32)
    parts = [jnp.tile(ch, (1, 2)), jnp.tile(sh, (1, 2)), jnp.tile(cd, (1, 4)), jnp.tile(sd, (1, 4)),
             jnp.concatenate([one, cm, z32], 1), jnp.concatenate([z64, sm, z32], 1),
             jnp.concatenate([z64, cm, z32], 1), jnp.concatenate([z64, sm, z32], 1)]
    return jnp.concatenate(parts, 1)


def _adaln_kernel(c_ref, w_ref, b_ref, o_ref):
    c = c_ref[...]
    sc = (c * jax.nn.sigmoid(c)).astype(BF16)
    o_ref[0] = _dot(sc, w_ref[0].astype(BF16)) + b_ref[0]


def _adaln(c, ada_w, ada_b):
    nl, d, n6 = ada_w.shape
    b = c.shape[0]
    tn = 1536
    return pl.pallas_call(
        _adaln_kernel,
        grid=(nl, n6 // tn),
        in_specs=[pl.BlockSpec((b, d), lambda l, j: (0, 0)),
                  pl.BlockSpec((1, d, tn), lambda l, j: (l, 0, j)),
                  pl.BlockSpec((1, 1, tn), lambda l, j: (l, 0, j))],
        out_specs=pl.BlockSpec((1, b, tn), lambda l, j: (l, 0, j)),
        out_shape=jax.ShapeDtypeStruct((nl, b, n6), F32),
        compiler_params=_params("parallel", "parallel"),
        name="adaln",
    )(c, ada_w, ada_b.reshape(nl, 1, n6))


def _inproj_kernel(x_ref, mod_ref, ng_ref, w_ref, tab_ref, wuq_ref, wukv_ref, gq_ref, gkv_ref,
                   nq_ref, nk_ref, nv_ref, ngo_ref, dq_ref, dk_ref, dv_ref,
                   mq_ref, mk_ref, mv_ref, sq_ref, sk_ref, sv_ref):
    x = x_ref[...]
    h = _rms(x, ng_ref[...]) * (1.0 + mod_ref[0, 1:2, :]) + mod_ref[0, 0:1, :]
    hb = h.astype(BF16)

    def mm(name, width):
        o = _OFF[name]
        return _dot(hb, w_ref[:, o:o + width])

    def tab(j, reps):
        t = tab_ref[:, j * LANES:(j + 1) * LANES]
        return t if reps == 1 else jnp.concatenate([t] * reps, axis=1)

    def rope(name, rname, width, cj, scale=1.0):
        r = mm(name, width) * tab(cj, width // LANES) + mm(rname, width) * tab(cj + 1, width // LANES)
        return r if scale == 1.0 else r * scale

    d = HEAD_DIM
    nq_ref[...] = rope("nq", "nqr", 256, 0, d ** -0.5).astype(BF16)
    nk_ref[...] = rope("nk", "nkr", 384, 0).astype(BF16)
    nv_ref[...] = mm("nv", 384).astype(BF16)
    ngo_ref[...] = jax.nn.sigmoid(mm("ng", LANES))
    dq_ref[...] = rope("dq", "dqr", 256, 2, DIFF_QK_DIM ** -0.5).astype(BF16)
    dk_ref[...] = rope("dk", "dkr", 256, 2).astype(BF16)
    dv_ref[...] = mm("dv", 256).astype(BF16)
    cq = _rms(mm("mcq", MLA_Q_RANK), gq_ref[...]).astype(BF16)
    nh = MLA_HEADS * LANES
    qa = _dot(cq, wuq_ref[:, 0:nh])
    qb = _dot(cq, wuq_ref[:, nh:2 * nh])
    mq = (qa * tab(4, MLA_HEADS) + qb * tab(5, MLA_HEADS)) * ((MLA_NOPE_DIM + MLA_ROPE_DIM) ** -0.5)
    mq_ref[...] = mq.astype(BF16)
    ckv = _rms(mm("mckv", MLA_KV_RANK), gkv_ref[...]).astype(BF16)
    kk = _dot(ckv, wukv_ref[:, 0:nh])
    kr = mm("mkr", LANES) * tab(6, 1) + mm("mkrr", LANES) * tab(7, 1)
    mk_ref[...] = (kk + jnp.concatenate([kr] * MLA_HEADS, axis=1)).astype(BF16)
    mv_ref[...] = _dot(ckv, wukv_ref[:, nh:nh + MLA_HEADS * MLA_V_DIM]).astype(BF16)
    sq_ref[...] = rope("sq", "sqr", 256, 0, d ** -0.5).astype(BF16)
    sk_ref[...] = rope("sk", "skr", 128, 0).astype(BF16)
    sv_ref[...] = mm("sv", 128).astype(BF16)


def _inproj(x2d, mod_l, norm_g, w_big, table, wuq, wukv, gq, gkv, seq):
    t, d = x2d.shape
    tm = 256
    tpb = seq // tm
    widths = [256, 384, 384, 128, 256, 256, 256, 512, 512, 256, 256, 128, 128]
    dts = [BF16, BF16, BF16, F32, BF16, BF16, BF16, BF16, BF16, BF16, BF16, BF16, BF16]
    full = lambda a: pl.BlockSpec(a.shape, lambda i: (0,) * a.ndim)
    return pl.pallas_call(
        _inproj_kernel,
        grid=(t // tm,),
        in_specs=[pl.BlockSpec((tm, d), lambda i: (i, 0)),
                  pl.BlockSpec((1, 6, d), lambda i: (i // tpb, 0, 0)),
                  full(norm_g), full(w_big),
                  pl.BlockSpec((tm, table.shape[1]), lambda i: (i % tpb, 0)),
                  full(wuq), full(wukv), full(gq), full(gkv)],
        out_specs=[pl.BlockSpec((tm, w), lambda i: (i, 0)) for w in widths],
        out_shape=[jax.ShapeDtypeStruct((t, w), dt) for w, dt in zip(widths, dts)],
        compiler_params=_params("parallel"),
        name="inproj",
    )(x2d, mod_l, norm_g, w_big, table, wuq, wukv, gq, gkv)


def _lane_mask(lo, hi):
    lane = lax.broadcasted_iota(jnp.int32, (1, LANES), 1)
    return (lane >= lo) & (lane < hi)


def _masked(q, lo, hi):
    return jnp.where(_lane_mask(lo, hi), q, jnp.zeros_like(q))


def _flash(q, kfn, vfn, maskfn, lo, hi):
    m_rows = q.shape[0]

    def body(j, carry):
        m, l, acc = carry
        s = jnp.where(maskfn(j), _dot_nt(q, kfn(j)), NEG)
        m2 = jnp.maximum(m, jnp.max(s, axis=-1, keepdims=True))
        a = jnp.exp(m - m2)
        p = jnp.exp(s - m2)
        l2 = a * l + jnp.sum(p, axis=-1, keepdims=True)
        acc2 = a * acc + _dot(p.astype(BF16), vfn(j))
        return m2, l2, acc2

    init = (jnp.full((m_rows, 1), NEG, F32), jnp.zeros((m_rows, 1), F32), jnp.zeros((m_rows, LANES), F32))
    return lax.fori_loop(lo, hi, body, init)


def _rows(ref, j, c0, c1):
    return ref[pl.ds(pl.multiple_of(j * KV_TILE, KV_TILE), KV_TILE), c0:c1]


def _qpos(i, reps):
    p = i * Q_TILE + lax.broadcasted_iota(jnp.int32, (Q_TILE, 1), 0)
    return p if reps == 1 else jnp.concatenate([p] * reps, axis=0)


def _kpos(j):
    return j * KV_TILE + lax.broadcasted_iota(jnp.int32, (1, KV_TILE), 1)


def _pair(lo_val, hi_val):
    return jnp.where(_lane_mask(0, 64), lo_val, hi_val)


def _nsa_cmp_kernel(q_ref, kc_ref, vc_ref, wk_ref, wv_ref, pk_ref, pv_ref, ov_ref, oc_ref, sel_ref, *, top_n, n_sel):
    half = wk_ref.shape[1]

    def compress(x_ref, w_ref, p_ref):
        x = x_ref[...]
        a = _dot(x, w_ref[0])
        b = _dot(x, w_ref[1])
        p = jnp.broadcast_to(p_ref[...], (8, 2 * half)).astype(BF16)
        const = (_dot(p[:, 0:half], w_ref[0]) + _dot(p[:, half:2 * half], w_ref[1]))[0:1]
        return a + jnp.concatenate([b[1:], b[:1]], axis=0) + const

    kcmp = compress(kc_ref, wk_ref, pk_ref).astype(BF16)
    vcmp = compress(vc_ref, wv_ref, pv_ref).astype(BF16)
    ncp = kcmp.shape[0]
    ov = ov_ref[...]
    rb = 256
    cend = NSA_CMP_STRIDE * lax.broadcasted_iota(jnp.int32, (1, ncp), 1) + (NSA_CMP_LEN - 1)
    lane = lax.broadcasted_iota(jnp.int32, (1, LANES), 1)

    def block(r, carry):
        r0 = pl.multiple_of(r * rb, rb)
        q = q_ref[pl.ds(r0, rb), :]
        tpos = r0 + lax.broadcasted_iota(jnp.int32, (rb, 1), 0)
        vis = cend <= tpos
        psum = jnp.zeros((rb, ncp), F32)
        outs = []
        for c in range(2):
            halves = []
            for hh in range(2):
                qm = _masked(q[:, c * LANES:(c + 1) * LANES], 64 * hh, 64 * hh + 64)
                s = jnp.where(vis, _dot_nt(qm, kcmp), NEG)
                e = jnp.exp(s - jnp.max(s, axis=-1, keepdims=True))
                p = jnp.where(vis, e / jnp.sum(e, axis=-1, keepdims=True), 0.0)
                psum = psum + p
                halves.append(_dot(p.astype(BF16), vcmp))
            outs.append(_pair(halves[0], halves[1]))
        oc_ref[pl.ds(r0, rb), :] = jnp.concatenate(outs, axis=1)
        hi = psum.astype(BF16)
        lo = (psum - hi.astype(F32)).astype(BF16)
        imp = _dot(hi, ov) + _dot(lo, ov)
        qblk = tpos // NSA_SEL_LEN
        allowed = lane <= qblk
        forced = (lane == 0) | (lane == qblk) | (lane == qblk - 1)
        impf = jnp.where(allowed, jnp.where(forced, FORCE, imp), NEG)
        rank = jnp.zeros((rb, LANES), F32)
        for j in range(n_sel):
            col = impf[:, j:j + 1]
            rank = rank + jnp.where(lane > j, jnp.where(col >= impf, 1.0, 0.0), jnp.where(col > impf, 1.0, 0.0))
        sel_ref[pl.ds(r0, rb), :] = jnp.where(allowed & (rank < top_n), 1.0, 0.0).astype(BF16)
        return carry

    lax.fori_loop(0, q_ref.shape[0] // rb, block, 0)


def _nsa_cmp(nq, nk, nv, wk, wv, pos_k, pos_v, batch, seq):
    d = HEAD_DIM
    nc = seq // NSA_CMP_STRIDE
    ncp = -(-nc // LANES) * LANES
    n_sel = seq // NSA_SEL_LEN
    assert n_sel <= LANES
    top_n = min(NSA_TOP_N, n_sel)

    def seg(a):
        a = a[:, :d].reshape(batch, nc, NSA_CMP_STRIDE * d)
        return jnp.pad(a, ((0, 0), (0, ncp - nc), (0, 0))).reshape(batch * ncp, NSA_CMP_STRIDE * d)

    half = NSA_CMP_STRIDE * d
    dupw = lambda w: jnp.concatenate([w, w], axis=1).reshape(2, half, 2 * d).astype(BF16)
    cpos = NSA_CMP_STRIDE * np.arange(ncp)[:, None] + np.arange(NSA_CMP_LEN)[None, :]
    ovl = np.zeros((ncp, LANES), np.float32)
    for j in range(n_sel):
        ovl[:, j] = (cpos // NSA_SEL_LEN == j).mean(axis=1)
    ovl[nc - 1:, :] = 0.0
    full = lambda a: pl.BlockSpec(a.shape, lambda b: (0,) * a.ndim)
    wk2, wv2 = dupw(wk), dupw(wv)
    pk, pv = pos_k.reshape(1, -1), pos_v.reshape(1, -1)
    ov = jnp.asarray(ovl, BF16)
    return pl.pallas_call(
        functools.partial(_nsa_cmp_kernel, top_n=top_n, n_sel=n_sel),
        grid=(batch,),
        in_specs=[pl.BlockSpec((seq, 256), lambda b: (b, 0)),
                  pl.BlockSpec((ncp, half), lambda b: (b, 0)),
                  pl.BlockSpec((ncp, half), lambda b: (b, 0)),
                  full(wk2), full(wv2), full(pk), full(pv), full(ov)],
        out_specs=[pl.BlockSpec((seq, 256), lambda b: (b, 0)),
                   pl.BlockSpec((seq, LANES), lambda b: (b, 0))],
        out_shape=[jax.ShapeDtypeStruct((batch * seq, 256), F32),
                   jax.ShapeDtypeStruct((batch * seq, LANES), BF16)],
        compiler_params=_params("parallel"),
        name="nsa_cmp",
    )(nq, seg(nk), seg(nv), wk2, wv2, pk, pv, ov)


def _nsa_kernel(q_ref, k_ref, v_ref, sel_ref, g_ref, oc_ref, e_ref, o_ref):
    i = pl.program_id(1)
    q = q_ref[...]
    qs = jnp.concatenate([_masked(q[:, 0:128], 0, 64), _masked(q[:, 0:128], 64, 128),
                          _masked(q[:, 128:256], 0, 64), _masked(q[:, 128:256], 64, 128)], axis=0)
    qp = _qpos(i, 4)
    sel = sel_ref[...]

    def sel_mask(j):
        mv = _dot(sel, e_ref[:, pl.ds(pl.multiple_of(j * KV_TILE, KV_TILE), KV_TILE)])
        mv4 = jnp.concatenate([mv] * 4, axis=0)
        return jnp.where(_kpos(j) <= qp, mv4, 0.0) > 0.5

    _, l_s, a_s = _flash(qs, lambda j: _rows(k_ref, j, 128, 256), lambda j: _rows(v_ref, j, 128, 256),
                         sel_mask, 0, i + 1)
    o_sel = a_s / l_s

    def win_mask(j):
        dist = qp - _kpos(j)
        return jnp.where(dist >= 0, dist, NSA_WINDOW) < NSA_WINDOW

    wlo = jnp.maximum(i - NSA_WINDOW // KV_TILE, 0)
    _, l_w, a_w = _flash(qs, lambda j: _rows(k_ref, j, 256, 384), lambda j: _rows(v_ref, j, 256, 384),
                         win_mask, wlo, i + 1)
    o_win = a_w / l_w
    g = g_ref[...]
    oc = oc_ref[...]
    outs = []
    for c in range(2):
        occ = oc[:, c * LANES:(c + 1) * LANES]

        def comb(h):
            r = slice(h * Q_TILE, (h + 1) * Q_TILE)
            return g[:, 3 * h:3 * h + 1] * occ + g[:, 3 * h + 1:3 * h + 2] * o_sel[r] + g[:, 3 * h + 2:3 * h + 3] * o_win[r]

        outs.append(_pair(comb(2 * c), comb(2 * c + 1)))
    o_ref[...] = jnp.concatenate(outs, axis=1).astype(BF16)


def _nsa(nq, nk, nv, sel, gates, ocmp, batch, seq):
    nb = seq // Q_TILE
    expand = np.zeros((LANES, seq), np.float32)
    for j in range(seq // NSA_SEL_LEN):
        expand[j, j * NSA_SEL_LEN:(j + 1) * NSA_SEL_LEN] = 1.0
    e = jnp.asarray(expand, BF16)
    row = lambda w: pl.BlockSpec((Q_TILE, w), lambda b, i: (b * nb + i, 0))
    per_b = lambda w: pl.BlockSpec((seq, w), lambda b, i: (b, 0))
    return pl.pallas_call(
        _nsa_kernel,
        grid=(batch, nb),
        in_specs=[row(256), per_b(384), per_b(384), row(LANES), row(LANES), row(256),
                  pl.BlockSpec(e.shape, lambda b, i: (0, 0))],
        out_specs=row(256),
        out_shape=jax.ShapeDtypeStruct((batch * seq, 256), BF16),
        compiler_params=_params("parallel", "arbitrary"),
        name="nsa_attn",
    )(nq, nk, nv, sel, gates, ocmp, e)


def _diff_kernel(q_ref, k_ref, v_ref, lam_ref, sg_ref, o_ref, *, lam_init):
    i = pl.program_id(1)
    q = q_ref[...]
    qp = _qpos(i, 4)
    lv = lam_ref[...]
    lam = (jnp.exp(jnp.sum(lv[0:1] * lv[1:2], axis=-1, keepdims=True))
           - jnp.exp(jnp.sum(lv[2:3] * lv[3:4], axis=-1, keepdims=True)) + lam_init)
    outs = []
    for c in range(2):
        qc = q[:, c * LANES:(c + 1) * LANES]
        qs = jnp.concatenate([_masked(qc, 32 * t, 32 * t + 32) for t in range(4)], axis=0)
        _, l, a = _flash(qs, lambda j: _rows(k_ref, j, c * LANES, (c + 1) * LANES),
                         lambda j: _rows(v_ref, j, c * LANES, (c + 1) * LANES),
                         lambda j: _kpos(j) <= qp, 0, i + 1)
        o = a / l
        r = [o[t * Q_TILE:(t + 1) * Q_TILE] for t in range(4)]
        dd = _pair(r[0] - lam * r[1], r[2] - lam * r[3])
        sq = dd * dd
        lo = _lane_mask(0, 64)
        ms = _pair(jnp.sum(jnp.where(lo, sq, 0.0), axis=-1, keepdims=True),
                   jnp.sum(jnp.where(lo, 0.0, sq), axis=-1, keepdims=True)) * (1.0 / DIFF_V_DIM)
        outs.append(dd * lax.rsqrt(ms + EPS) * sg_ref[...] * (1.0 - lam_init))
    o_ref[...] = jnp.concatenate(outs, axis=1).astype(BF16)


def _diff(dq, dk, dv, lamv, sub_g2, layer, batch, seq):
    nb = seq // Q_TILE
    lam_init = 0.8 - 0.6 * math.exp(-0.3 * layer)
    row = lambda w: pl.BlockSpec((Q_TILE, w), lambda b, i: (b * nb + i, 0))
    per_b = lambda w: pl.BlockSpec((seq, w), lambda b, i: (b, 0))
    full = lambda a: pl.BlockSpec(a.shape, lambda b, i: (0,) * a.ndim)
    return pl.pallas_call(
        functools.partial(_diff_kernel, lam_init=lam_init),
        grid=(batch, nb),
        in_specs=[row(256), per_b(256), per_b(256), full(lamv), full(sub_g2)],
        out_specs=row(256),
        out_shape=jax.ShapeDtypeStruct((batch * seq, 256), BF16),
        compiler_params=_params("parallel", "arbitrary"),
        name="diff_attn",
    )(dq, dk, dv, lamv, sub_g2)


def _mla_kernel(q_ref, k_ref, v_ref, o_ref):
    i = pl.program_id(1)
    q = q_ref[...]
    qp = _qpos(i, 1)
    heads = []
    for h in range(MLA_HEADS):
        c = h // 2
        _, l, a = _flash(q[:, h * LANES:(h + 1) * LANES],
                         lambda j: _rows(k_ref, j, h * LANES, (h + 1) * LANES),
                         lambda j: _rows(v_ref, j, c * LANES, (c + 1) * LANES),
                         lambda j: _kpos(j) <= qp, 0, i + 1)
        heads.append(a / l)
    o_ref[...] = jnp.concatenate([_pair(heads[0], heads[1]), _pair(heads[2], heads[3])], axis=1).astype(BF16)


def _mla(mq, mk, mv, batch, seq):
    nb = seq // Q_TILE
    row = lambda w: pl.BlockSpec((Q_TILE, w), lambda b, i: (b * nb + i, 0))
    per_b = lambda w: pl.BlockSpec((seq, w), lambda b, i: (b, 0))
    return pl.pallas_call(
        _mla_kernel,
        grid=(batch, nb),
        in_specs=[row(512), per_b(512), per_b(256)],
        out_specs=row(256),
        out_shape=jax.ShapeDtypeStruct((batch * seq, 256), BF16),
        compiler_params=_params("parallel", "arbitrary"),
        name="mla_attn",
    )(mq, mk, mv)


def _swa_kernel(q_ref, k_ref, v_ref, sink_ref, o_ref):
    i = pl.program_id(1)
    q = q_ref[...]
    qs = jnp.concatenate([_masked(q[:, 0:128], 0, 64), _masked(q[:, 0:128], 64, 128),
                          _masked(q[:, 128:256], 0, 64), _masked(q[:, 128:256], 64, 128)], axis=0)
    qp = _qpos(i, 4)

    def win_mask(j):
        dist = qp - _kpos(j)
        return jnp.where(dist >= 0, dist, SWA_WINDOW) < SWA_WINDOW

    m, l, a = _flash(qs, lambda j: _rows(k_ref, j, 0, 128), lambda j: _rows(v_ref, j, 0, 128),
                     win_mask, jnp.maximum(i - SWA_WINDOW // KV_TILE, 0), i + 1)
    sk = sink_ref[...]
    ones = jnp.ones((Q_TILE, 1), F32)
    sink = jnp.concatenate([ones * sk[:, h:h + 1] for h in (0, 2, 1, 3)], axis=0)
    m2 = jnp.maximum(m, sink)
    sc = jnp.exp(m - m2)
    o = a * sc / (l * sc + jnp.exp(sink - m2))
    r = [o[t * Q_TILE:(t + 1) * Q_TILE] for t in range(4)]
    o_ref[...] = jnp.concatenate([_pair(r[0], r[1]), _pair(r[2], r[3])], axis=1).astype(BF16)


def _swa(sq, sk, sv, sinks, batch, seq):
    nb = seq // Q_TILE
    row = lambda w: pl.BlockSpec((Q_TILE, w), lambda b, i: (b * nb + i, 0))
    per_b = lambda w: pl.BlockSpec((seq, w), lambda b, i: (b, 0))
    return pl.pallas_call(
        _swa_kernel,
        grid=(batch, nb),
        in_specs=[row(256), per_b(128), per_b(128), pl.BlockSpec(sinks.shape, lambda b, i: (0, 0))],
        out_specs=row(256),
        out_shape=jax.ShapeDtypeStruct((batch * seq, 256), BF16),
        compiler_params=_params("parallel", "arbitrary"),
        name="swa_attn",
    )(sq, sk, sv, sinks)


def _outproj_kernel(x_ref, mod_ref, ng_ref, oa_ref, ob_ref, oc_ref, od_ref, w_ref, x1_ref, ht_ref):
    acc = _dot(oa_ref[...], w_ref[0:256, :])
    acc = acc + _dot(ob_ref[...], w_ref[256:512, :])
    acc = acc + _dot(oc_ref[...], w_ref[512:768, :])
    acc = acc + _dot(od_ref[...], w_ref[768:1024, :])
    x1 = x_ref[...] + mod_ref[0, 2:3, :] * acc
    x1_ref[...] = x1
    h = _rms(x1, ng_ref[...]) * (1.0 + mod_ref[0, 4:5, :]) + mod_ref[0, 3:4, :]
    ht_ref[...] = h.T.astype(BF16)


def _outproj(x2d, mod_l, norm_g, oa, ob, oc, od, w_out, seq):
    t, d = x2d.shape
    tm = 256
    tpb = seq // tm
    row = lambda w: pl.BlockSpec((tm, w), lambda i: (i, 0))
    full = lambda a: pl.BlockSpec(a.shape, lambda i: (0,) * a.ndim)
    return pl.pallas_call(
        _outproj_kernel,
        grid=(t // tm,),
        in_specs=[row(d), pl.BlockSpec((1, 6, d), lambda i: (i // tpb, 0, 0)), full(norm_g),
                  row(256), row(256), row(256), row(256), full(w_out)],
        out_specs=[row(d), pl.BlockSpec((d, tm), lambda i: (0, i))],
        out_shape=[jax.ShapeDtypeStruct((t, d), F32), jax.ShapeDtypeStruct((d, t), BF16)],
        compiler_params=_params("parallel"),
        name="outproj",
    )(x2d, mod_l, norm_g, oa, ob, oc, od, w_out)


_CAND_PIECES = [(0, 0, 8), (0, 8, 8), (1, 0, 8), (2, 0, 5), (3, 0, 4), (4, 0, 3), (5, 0, 2), (6, 0, 2), (7, 0, 2),
                (None, 0, 8)]


def _top16(s):
    tb = s.shape[1]
    row16 = lax.broadcasted_iota(jnp.int32, (PEER_TOPK, tb), 0)
    vals = jnp.zeros((PEER_TOPK, tb), F32)
    rank = jnp.full(s.shape, float(PEER_N_KEYS), F32)
    work = s
    for r in range(PEER_TOPK):
        m = jnp.max(work, axis=0, keepdims=True)
        hit = work == m
        rank = jnp.where(hit, float(r), rank)
        work = jnp.where(hit, -jnp.inf, work)
        vals = jnp.where(row16 == r, m, vals)
    return vals, rank


def _router_kernel(ht_ref, wq_ref, k1_ref, k2_ref, ea_ref, n1_ref, r2_ref, eb_ref):
    ht = ht_ref[...]
    tb = ht.shape[1]
    nk = PEER_N_KEYS
    row8 = lax.broadcasted_iota(jnp.int32, (8, tb), 0)
    row16 = lax.broadcasted_iota(jnp.int32, (PEER_TOPK, tb), 0)

    def head(h, carry):
        o = pl.multiple_of(h * 2 * nk, 2 * nk)
        q1 = _dot(wq_ref[pl.ds(o, nk), :], ht).astype(BF16)
        q2 = _dot(wq_ref[pl.ds(o + nk, nk), :], ht).astype(BF16)
        s1 = _dot(k1_ref[...], q1)
        s2 = _dot(k2_ref[...], q2)
        v1, rank1 = _top16(s1)
        v2, rank2 = _top16(s2)
        top = v1[0:1] + v2[0:1]
        pieces = []
        for r1, c0, valid in _CAND_PIECES:
            if r1 is None:
                p = v1[8:16] + v2[0:1]
            else:
                p = v1[r1:r1 + 1] + v2[c0:c0 + 8]
                if valid < 8:
                    p = jnp.where(row8 < valid, p, -jnp.inf)
            pieces.append(p)
        taken = [jnp.zeros((8, tb), F32) for _ in pieces]
        for _ in range(PEER_TOPK):
            m = pieces[0]
            for p in pieces[1:]:
                m = jnp.maximum(m, p)
            m = jnp.max(m, axis=0, keepdims=True)
            for idx in range(len(pieces)):
                hit = pieces[idx] == m
                taken[idx] = jnp.where(hit, 1.0, taken[idx])
                pieces[idx] = jnp.where(hit, -jnp.inf, pieces[idx])
        counts = jnp.zeros((PEER_TOPK, tb), F32)
        z = jnp.zeros((1, tb), F32)
        for idx, (r1, c0, valid) in enumerate(_CAND_PIECES):
            if r1 is None:
                cell = v1[8:16] + v2[0:1]
                counts = counts + jnp.concatenate([jnp.zeros((8, tb), F32), taken[idx]], axis=0)
            else:
                cell = v1[r1:r1 + 1] + v2[c0:c0 + 8]
                n = jnp.sum(taken[idx], axis=0, keepdims=True)
                counts = counts + jnp.where(row16 == r1, n, 0.0)
            z = z + jnp.sum(jnp.where(taken[idx] > 0.5, jnp.exp(cell - top), 0.0), axis=0, keepdims=True)
        n1 = jnp.zeros((nk, tb), F32)
        for r in range(PEER_TOPK):
            n1 = jnp.where(rank1 == float(r), counts[r:r + 1], n1)
        ea_ref[h] = jnp.exp(s1 - v1[0:1])
        n1_ref[h] = n1
        r2_ref[h] = rank2
        eb_ref[h] = jnp.exp(s2 - v2[0:1]) / z
        return carry

    lax.fori_loop(0, PEER_HEADS, head, 0)


def _router(ht, wq_t, k1, k2):
    d, t = ht.shape
    tb = 256
    full = lambda a: pl.BlockSpec(a.shape, lambda i: (0,) * a.ndim)
    out = pl.BlockSpec((PEER_HEADS, PEER_N_KEYS, tb), lambda i: (0, 0, i))
    return pl.pallas_call(
        _router_kernel,
        grid=(t // tb,),
        in_specs=[pl.BlockSpec((d, tb), lambda i: (0, i)), full(wq_t), full(k1), full(k2)],
        out_specs=[out, out, out, out],
        out_shape=[jax.ShapeDtypeStruct((PEER_HEADS, PEER_N_KEYS, t), F32)] * 4,
        compiler_params=_params("parallel"),
        name="peer_router",
    )(ht, wq_t, k1, k2)


def _gelu_tanh(x):
    return 0.5 * x * (1.0 + jnp.tanh(math.sqrt(2.0 / math.pi) * (x + 0.044715 * (x * x * x))))


def _peer_kernel(ht_ref, u_ref, vt_ref, ea_ref, n1_ref, r2_ref, eb_ref, x_ref, mod_ref, fg_ref, o_ref, acc_ref,
                 *, final):
    e = pl.program_id(1)
    n_i1 = u_ref.shape[0] // PEER_N_KEYS

    @pl.when(e == 0)
    def _():
        acc_ref[...] = jnp.zeros_like(acc_ref)

    act = _gelu_tanh(_dot(u_ref[...], ht_ref[...]))
    rows = []
    for j in range(n_i1):
        i1 = e * n_i1 + j
        m = None
        for h in range(PEER_HEADS):
            n_row = n1_ref[h, pl.ds(i1, 1), :]
            ea_row = ea_ref[h, pl.ds(i1, 1), :]
            term = jnp.where(r2_ref[h] < n_row, eb_ref[h], 0.0) * ea_row
            m = term if m is None else m + term
        rows.append((m * act[j * PEER_N_KEYS:(j + 1) * PEER_N_KEYS]).astype(BF16))
    wt = jnp.concatenate(rows, axis=0) if n_i1 > 1 else rows[0]
    acc_ref[...] += _dot(vt_ref[...], wt)

    @pl.when(e == pl.num_programs(1) - 1)
    def _():
        y = x_ref[...] + mod_ref[0, 5:6, :] * acc_ref[...].T
        if final:
            y = _rms(y, fg_ref[...])
        o_ref[...] = y


def _peer(ht, u_bf, vt_bf, ea, n1, r2, eb, x1, mod_l, final_g, seq, final):
    d, t = ht.shape
    n_exp = u_bf.shape[0]
    tb = 512
    eb_blk = 512
    tpb = seq // tb
    tok3 = pl.BlockSpec((PEER_HEADS, PEER_N_KEYS, tb), lambda i, e: (0, 0, i))
    return pl.pallas_call(
        functools.partial(_peer_kernel, final=final),
        grid=(t // tb, n_exp // eb_blk),
        in_specs=[pl.BlockSpec((d, tb), lambda i, e: (0, i)),
                  pl.BlockSpec((eb_blk, d), lambda i, e: (e, 0)),
                  pl.BlockSpec((d, eb_blk), lambda i, e: (0, e)),
                  tok3, tok3, tok3, tok3,
                  pl.BlockSpec((tb, d), lambda i, e: (i, 0)),
                  pl.BlockSpec((1, 6, d), lambda i, e: (i // tpb, 0, 0)),
                  pl.BlockSpec(final_g.shape, lambda i, e: (0, 0))],
        out_specs=pl.BlockSpec((tb, d), lambda i, e: (i, 0)),
        out_shape=jax.ShapeDtypeStruct((t, d), F32),
        scratch_shapes=[pltpu.VMEM((d, tb), F32)],
        compiler_params=_params("parallel", "arbitrary"),
        name="peer_experts",
    )(ht, u_bf, vt_bf, ea, n1, r2, eb, x1, mod_l, final_g)


def kernel(x, c, ada_w, ada_b, norm_mix_g, norm_ffn_g, w_in, nsa_cmp_pos_k, nsa_cmp_pos_v, nsa_cmp_wk, nsa_cmp_wv, diff_lam_q1, diff_lam_k1, diff_lam_q2, diff_lam_k2, diff_sub_g, mla_q_norm_g, mla_w_uq, mla_kv_norm_g, mla_w_ukv, swa_sinks, w_out, peer_w_q, peer_sub_k1, peer_sub_k2, peer_u, peer_v, final_g):
    batch, seq, d = x.shape
    depth = w_in.shape[0]
    assert seq % 256 == 0 and seq >= 2 * KV_TILE
    x2d = x.reshape(batch * seq, d)
    mod = _adaln(c, ada_w, ada_b).reshape(depth, batch, 6, d)
    table = _rope_table(seq)
    mixw = 4 * HEAD_DIM
    swa_rows = 3 * mixw + np.concatenate([h * HEAD_DIM + np.arange(HEAD_DIM) for h in (0, 2, 1, 3)])
    out_rows = jnp.asarray(np.concatenate([np.arange(3 * mixw), swa_rows]), jnp.int32)
    pad128 = lambda v: jnp.pad(v, (0, LANES - v.shape[0])).reshape(1, LANES)
    fg = final_g.reshape(1, d)
    for l in range(depth):
        w_big = _take_cols(w_in[l], _IN_IDX, _IN_SGN)
        wuq = _take_cols(mla_w_uq[l], _UQ_IDX, _UQ_SGN)
        wukv = _take_cols(mla_w_ukv[l], _UKV_IDX, _UKV_SGN)
        (nq, nk, nv, ng, dq, dk, dv, mq, mk, mv, sq, sk, sv) = _inproj(
            x2d, mod[l], norm_mix_g[l].reshape(1, d), w_big, table, wuq, wukv,
            mla_q_norm_g[l].reshape(1, -1), mla_kv_norm_g[l].reshape(1, -1), seq)
        ocmp, sel = _nsa_cmp(nq, nk, nv, nsa_cmp_wk[l], nsa_cmp_wv[l], nsa_cmp_pos_k[l], nsa_cmp_pos_v[l], batch, seq)
        o_a = _nsa(nq, nk, nv, sel, ng, ocmp, batch, seq)
        lamv = jnp.concatenate([pad128(diff_lam_q1[l]), pad128(diff_lam_k1[l]),
                                pad128(diff_lam_q2[l]), pad128(diff_lam_k2[l])], axis=0)
        sub_g2 = jnp.concatenate([diff_sub_g[l], diff_sub_g[l]]).reshape(1, LANES)
        o_b = _diff(dq, dk, dv, lamv, sub_g2, l, batch, seq)
        o_c = _mla(mq, mk, mv, batch, seq)
        o_d = _swa(sq, sk, sv, pad128(swa_sinks[l]), batch, seq)
        w_o = jnp.take(w_out[l], out_rows, axis=0).astype(BF16)
        x1, ht = _outproj(x2d, mod[l], norm_ffn_g[l].reshape(1, d), o_a, o_b, o_c, o_d, w_o, seq)
        ea, n1, r2, eb = _router(ht, peer_w_q[l].T.astype(BF16), peer_sub_k1[l].astype(BF16),
                                 peer_sub_k2[l].astype(BF16))
        x2d = _peer(ht, peer_u[l].astype(BF16), peer_v[l].T.astype(BF16), ea, n1, r2, eb, x1, mod[l], fg,
                    seq, final=(l == depth - 1))
    return x2d.reshape(batch, seq, d)
```

```python
import functools
import math

import numpy as np
import jax
import jax.numpy as jnp
from jax import lax
from jax.experimental import pallas as pl
from jax.experimental.pallas import tpu as pltpu

F32 = jnp.float32
BF16 = jnp.bfloat16

HEAD_DIM = 64
ROPE_THETA = 10000.0
EPS = 1e-6
NEG = -1e30
FORCE = 1e4

NSA_HEADS = 4
NSA_CMP_LEN = 32
NSA_CMP_STRIDE = 16
NSA_SEL_LEN = 64
NSA_TOP_N = 16
NSA_WINDOW = 512

DIFF_HEADS = 4
DIFF_QK_DIM = 32
DIFF_V_DIM = 64

MLA_HEADS = 4
MLA_Q_RANK = 256
MLA_KV_RANK = 128
MLA_NOPE_DIM = 64
MLA_ROPE_DIM = 32
MLA_V_DIM = 64

SWA_HEADS = 4
SWA_KV_HEADS = 2
SWA_WINDOW = 128

PEER_HEADS = 8
PEER_N_KEYS = 128
PEER_TOPK = 16
PEER_QUERY_DIM = 256

LANES = 128
VMEM_LIMIT = 56 * 1024 * 1024


def _dot(a, b):
    return jnp.dot(a, b, preferred_element_type=F32)


def _dot_nt(a, b):
    return lax.dot_general(a, b, (((1,), (1,)), ((), ())), preferred_element_type=F32)


def _params(*sem):
    return pltpu.CompilerParams(dimension_semantics=sem, vmem_limit_bytes=VMEM_LIMIT)


def _rms(x, g):
    return x * lax.rsqrt(jnp.mean(x * x, axis=-1, keepdims=True) + EPS) * g


def _rot_idx(base, dim):
    half = dim // 2
    idx = np.concatenate([base + half + np.arange(half), base + np.arange(half)])
    sgn = np.concatenate([-np.ones(half), np.ones(half)])
    return idx, sgn


def _in_plan():
    d = HEAD_DIM
    nsa0 = 0
    nsa_cols = NSA_HEADS * d + 6 * d + 3 * NSA_HEADS
    diff0 = nsa0 + nsa_cols
    diff_cols = 2 * DIFF_HEADS * 2 * DIFF_QK_DIM + DIFF_HEADS * DIFF_V_DIM
    mla0 = diff0 + diff_cols
    mla_cols = MLA_Q_RANK + MLA_KV_RANK + MLA_ROPE_DIM
    swa0 = mla0 + mla_cols
    idx, sgn, off = [], [], {}

    def add(name, i, s=None):
        i = np.asarray(i, np.int64)
        s = np.ones(len(i)) if s is None else np.asarray(s, np.float64)
        pad = (-len(i)) % LANES
        off[name] = sum(len(a) for a in idx)
        idx.append(np.concatenate([i, np.zeros(pad, np.int64)]))
        sgn.append(np.concatenate([s, np.zeros(pad)]))

    def heads_rot(base, nheads, dim):
        ii, ss = zip(*[_rot_idx(base + h * dim, dim) for h in range(nheads)])
        return np.concatenate(ii), np.concatenate(ss)

    nq = nsa0 + np.arange(NSA_HEADS * d)
    add("nq", nq)
    add("nqr", *heads_rot(nsa0, NSA_HEADS, d))
    kb = nsa0 + NSA_HEADS * d
    kc, vc, ksl, vsl, kw, vw = [kb + j * d for j in range(6)]
    dup = lambda b: np.concatenate([b + np.arange(d), b + np.arange(d)])
    add("nk", np.concatenate([dup(kc), dup(ksl), dup(kw)]))
    kr = [_rot_idx(b, d) for b in (kc, kc, ksl, ksl, kw, kw)]
    add("nkr", np.concatenate([a for a, _ in kr]), np.concatenate([b for _, b in kr]))
    add("nv", np.concatenate([dup(vc), dup(vsl), dup(vw)]))
    add("ng", kb + 6 * d + np.arange(3 * NSA_HEADS))
    nqk = DIFF_HEADS * 2 * DIFF_QK_DIM
    add("dq", diff0 + np.arange(nqk))
    add("dqr", *heads_rot(diff0, 2 * DIFF_HEADS, DIFF_QK_DIM))
    add("dk", diff0 + nqk + np.arange(nqk))
    add("dkr", *heads_rot(diff0 + nqk, 2 * DIFF_HEADS, DIFF_QK_DIM))
    add("dv", diff0 + 2 * nqk + np.arange(DIFF_HEADS * DIFF_V_DIM))
    add("mcq", mla0 + np.arange(MLA_Q_RANK))
    add("mckv", mla0 + MLA_Q_RANK + np.arange(MLA_KV_RANK))
    kr0 = mla0 + MLA_Q_RANK + MLA_KV_RANK
    z64 = np.zeros(MLA_NOPE_DIM, np.int64)
    add("mkr", np.concatenate([z64, kr0 + np.arange(MLA_ROPE_DIM)]),
        np.concatenate([np.zeros(MLA_NOPE_DIM), np.ones(MLA_ROPE_DIM)]))
    ri, rs = _rot_idx(kr0, MLA_ROPE_DIM)
    add("mkrr", np.concatenate([z64, ri]), np.concatenate([np.zeros(MLA_NOPE_DIM), rs]))
    order = [0, 2, 1, 3]
    add("sq", np.concatenate([swa0 + h * d + np.arange(d) for h in order]))
    sr = [_rot_idx(swa0 + h * d, d) for h in order]
    add("sqr", np.concatenate([a for a, _ in sr]), np.concatenate([b for _, b in sr]))
    sk0 = swa0 + SWA_HEADS * d
    add("sk", sk0 + np.arange(SWA_KV_HEADS * d))
    add("skr", *heads_rot(sk0, SWA_KV_HEADS, d))
    add("sv", sk0 + SWA_KV_HEADS * d + np.arange(SWA_KV_HEADS * d))
    return np.concatenate(idx), np.concatenate(sgn), off


_IN_IDX, _IN_SGN, _OFF = _in_plan()
_NCOLS = len(_IN_IDX)


def _mla_plans():
    qd = MLA_NOPE_DIM + MLA_ROPE_DIM
    qi, qs, ri, rs = [], [], [], []
    for h in range(MLA_HEADS):
        b = h * qd
        qi += [b + np.arange(qd), np.zeros(LANES - qd, np.int64)]
        qs += [np.ones(qd), np.zeros(LANES - qd)]
        a, s = _rot_idx(b + MLA_NOPE_DIM, MLA_ROPE_DIM)
        ri += [np.zeros(MLA_NOPE_DIM, np.int64), a, np.zeros(LANES - qd, np.int64)]
        rs += [np.zeros(MLA_NOPE_DIM), s, np.zeros(LANES - qd)]
    kd = MLA_NOPE_DIM + MLA_V_DIM
    ki, ks, vi = [], [], []
    for h in range(MLA_HEADS):
        ki += [h * kd + np.arange(MLA_NOPE_DIM), np.zeros(LANES - MLA_NOPE_DIM, np.int64)]
        ks += [np.ones(MLA_NOPE_DIM), np.zeros(LANES - MLA_NOPE_DIM)]
        vi += [h * kd + MLA_NOPE_DIM + np.arange(MLA_V_DIM)]
    uq_idx = np.concatenate(qi + ri)
    uq_sgn = np.concatenate(qs + rs)
    ukv_idx = np.concatenate(ki + vi)
    ukv_sgn = np.concatenate(ks + [np.ones(MLA_HEADS * MLA_V_DIM)])
    return uq_idx, uq_sgn, ukv_idx, ukv_sgn


_UQ_IDX, _UQ_SGN, _UKV_IDX, _UKV_SGN = _mla_plans()


def _take_cols(w, idx, sgn):
    return (jnp.take(w, jnp.asarray(idx, jnp.int32), axis=1) * jnp.asarray(sgn, F32)[None, :]).astype(BF16)


def _rope_table(seq):
    def cs(dim):
        inv = 1.0 / (ROPE_THETA ** (jnp.arange(0, dim, 2, dtype=F32) / dim))
        ang = jnp.arange(seq, dtype=F32)[:, None] * inv[None, :]
        c, s = jnp.cos(ang), jnp.sin(ang)
        return jnp.concatenate([c, c], 1), jnp.concatenate([s, s], 1)
    ch, sh = cs(HEAD_DIM)
    cd, sd = cs(DIFF_QK_DIM)
    cm, sm = cs(MLA_ROPE_DIM)
    one = jnp.ones((seq, MLA_NOPE_DIM), F32)
    z64 = jnp.zeros((seq, MLA_NOPE_DIM), F32)
    z32 = jnp.zeros((seq, LANES - MLA_NOPE_DIM - MLA_ROPE_DIM), F32)
    parts = [jnp.tile(ch, (1, 2)), jnp.tile(sh, (1, 2)), jnp.tile(cd, (1, 4)), jnp.tile(sd, (1, 4)),
             jnp.concatenate([one, cm, z32], 1), jnp.concatenate([z64, sm, z32], 1),
             jnp.concatenate([z64, cm, z32], 1), jnp.concatenate([z64, sm, z32], 1)]
    return jnp.concatenate(parts, 1)


def _adaln_kernel(c_ref, w_ref, b_ref, o_ref):
    c = c_ref[...]
    sc = (c * jax.nn.sigmoid(c)).astype(BF16)
    o_ref[0] = _dot(sc, w_ref[0].astype(BF16)) + b_ref[0]


def _adaln(c, ada_w, ada_b):
    nl, d, n6 = ada_w.shape
    b = c.shape[0]
    tn = 1536
    return pl.pallas_call(
        _adaln_kernel,
        grid=(nl, n6 // tn),
        in_specs=[pl.BlockSpec((b, d), lambda l, j: (0, 0)),
                  pl.BlockSpec((1, d, tn), lambda l, j: (l, 0, j)),
                  pl.BlockSpec((1, 1, tn), lambda l, j: (l, 0, j))],
        out_specs=pl.BlockSpec((1, b, tn), lambda l, j: (l, 0, j)),
        out_shape=jax.ShapeDtypeStruct((nl, b, n6), F32),
        compiler_params=_params("parallel", "parallel"),
        name="adaln",
    )(c, ada_w, ada_b.reshape(nl, 1, n6))


def _inproj_kernel(x_ref, mod_ref, ng_ref, w_ref, tab_ref, wuq_ref, wukv_ref, gq_ref, gkv_ref,
                   nq_ref, nk_ref, nv_ref, ngo_ref, dq_ref, dk_ref, dv_ref,
                   mq_ref, mk_ref, mv_ref, sq_ref, sk_ref, sv_ref):
    x = x_ref[...]
    h = _rms(x, ng_ref[...]) * (1.0 + mod_ref[0, 1:2, :]) + mod_ref[0, 0:1, :]
    hb = h.astype(BF16)

    def mm(name, width):
        o = _OFF[name]
        return _dot(hb, w_ref[:, o:o + width])

    def tab(j, reps):
        t = tab_ref[:, j * LANES:(j + 1) * LANES]
        return t if reps == 1 else jnp.concatenate([t] * reps, axis=1)

    def rope(name, rname, width, cj, scale=1.0):
        r = mm(name, width) * tab(cj, width // LANES) + mm(rname, width) * tab(cj + 1, width // LANES)
        return r if scale == 1.0 else r * scale

    d = HEAD_DIM
    nq_ref[...] = rope("nq", "nqr", 256, 0, d ** -0.5).astype(BF16)
    nk_ref[...] = rope("nk", "nkr", 384, 0).astype(BF16)
    nv_ref[...] = mm("nv", 384).astype(BF16)
    ngo_ref[...] = jax.nn.sigmoid(mm("ng", LANES))
    dq_ref[...] = rope("dq", "dqr", 256, 2, DIFF_QK_DIM ** -0.5).astype(BF16)
    dk_ref[...] = rope("dk", "dkr", 256, 2).astype(BF16)
    dv_ref[...] = mm("dv", 256).astype(BF16)
    cq = _rms(mm("mcq", MLA_Q_RANK), gq_ref[...]).astype(BF16)
    nh = MLA_HEADS * LANES
    qa = _dot(cq, wuq_ref[:, 0:nh])
    qb = _dot(cq, wuq_ref[:, nh:2 * nh])
    mq = (qa * tab(4, MLA_HEADS) + qb * tab(5, MLA_HEADS)) * ((MLA_NOPE_DIM + MLA_ROPE_DIM) ** -0.5)
    mq_ref[...] = mq.astype(BF16)
    ckv = _rms(mm("mckv", MLA_KV_RANK), gkv_ref[...]).astype(BF16)
    kk = _dot(ckv, wukv_ref[:, 0:nh])
    kr = mm("mkr", LANES) * tab(6, 1) + mm("mkrr", LANES) * tab(7, 1)
    mk_ref[...] = (kk + jnp.concatenate([kr] * MLA_HEADS, axis=1)).astype(BF16)
    mv_ref[...] = _dot(ckv, wukv_ref[:, nh:nh + MLA_HEADS * MLA_V_DIM]).astype(BF16)
    sq_ref[...] = rope("sq", "sqr", 256, 0, d ** -0.5).astype(BF16)
    sk_ref[...] = rope("sk", "skr", 128, 0).astype(BF16)
    sv_ref[...] = mm("sv", 128).astype(BF16)


def _inproj(x2d, mod_l, norm_g, w_big, table, wuq, wukv, gq, gkv, seq):
    t, d = x2d.shape
    tm = 256
    tpb = seq // tm
    widths = [256, 384, 384, 128, 256, 256, 256, 512, 512, 256, 256, 128, 128]
    dts = [BF16, BF16, BF16, F32, BF16, BF16, BF16, BF16, BF16, BF16, BF16, BF16, BF16]
    full = lambda a: pl.BlockSpec(a.shape, lambda i: (0,) * a.ndim)
    return pl.pallas_call(
        _inproj_kernel,
        grid=(t // tm,),
        in_specs=[pl.BlockSpec((tm, d), lambda i: (i, 0)),
                  pl.BlockSpec((1, 6, d), lambda i: (i // tpb, 0, 0)),
                  full(norm_g), full(w_big),
                  pl.BlockSpec((tm, table.shape[1]), lambda i: (i % tpb, 0)),
                  full(wuq), full(wukv), full(gq), full(gkv)],
        out_specs=[pl.BlockSpec((tm, w), lambda i: (i, 0)) for w in widths],
        out_shape=[jax.ShapeDtypeStruct((t, w), dt) for w, dt in zip(widths, dts)],
        compiler_params=_params("parallel"),
        name="inproj",
    )(x2d, mod_l, norm_g, w_big, table, wuq, wukv, gq, gkv)


def _lane_mask(lo, hi):
    lane = lax.broadcasted_iota(jnp.int32, (1, LANES), 1)
    return (lane >= lo) & (lane < hi)


def _masked(q, lo, hi):
    return jnp.where(_lane_mask(lo, hi), q, jnp.zeros_like(q))


def _chain(q, k, v, mask, state, acc_ref, c):
    m, l = state
    s = _dot_nt(q, k)
    if mask is not None:
        s = jnp.where(mask, s, NEG)
    m2 = jnp.maximum(m, jnp.max(s, axis=-1, keepdims=True))
    a = jnp.exp(m - m2)
    p = jnp.exp(s - m2)
    acc_ref[c] = a * acc_ref[c] + _dot(p.astype(BF16), v)
    return m2, a * l + jnp.sum(p, axis=-1, keepdims=True)


def _init_state(n, rows):
    return tuple((jnp.full((rows, 1), NEG, F32), jnp.zeros((rows, 1), F32)) for _ in range(n))


def _ktile(ref, j, tk, c0, c1):
    return ref[pl.ds(pl.multiple_of(j * tk, tk), tk), c0:c1]


def _qpos(i, tq, reps=1):
    p = i * tq + lax.broadcasted_iota(jnp.int32, (tq, 1), 0)
    return p if reps == 1 else jnp.concatenate([p] * reps, axis=0)


def _kpos(j, tk):
    return j * tk + lax.broadcasted_iota(jnp.int32, (1, tk), 1)


def _half_heads(q):
    return [_masked(q[:, c * LANES:(c + 1) * LANES], 64 * hh, 64 * hh + 64) for c in range(2) for hh in range(2)]


def _pair(lo_val, hi_val):
    return jnp.where(_lane_mask(0, 64), lo_val, hi_val)


def _nsa_cmp_kernel(q_ref, kc_ref, vc_ref, wk_ref, wv_ref, pk_ref, pv_ref, ov_ref, oc_ref, sel_ref, *, top_n, n_sel):
    half = wk_ref.shape[1]

    def compress(x_ref, w_ref, p_ref):
        x = x_ref[...]
        a = _dot(x, w_ref[0])
        b = _dot(x, w_ref[1])
        p = jnp.broadcast_to(p_ref[...], (8, 2 * half)).astype(BF16)
        const = (_dot(p[:, 0:half], w_ref[0]) + _dot(p[:, half:2 * half], w_ref[1]))[0:1]
        return a + jnp.concatenate([b[1:], b[:1]], axis=0) + const

    kcmp = compress(kc_ref, wk_ref, pk_ref).astype(BF16)
    vcmp = compress(vc_ref, wv_ref, pv_ref).astype(BF16)
    ncp = kcmp.shape[0]
    ov = ov_ref[...]
    rb = 256
    cend = NSA_CMP_STRIDE * lax.broadcasted_iota(jnp.int32, (1, ncp), 1) + (NSA_CMP_LEN - 1)
    lane = lax.broadcasted_iota(jnp.int32, (1, LANES), 1)

    def block(r, carry):
        r0 = pl.multiple_of(r * rb, rb)
        q = q_ref[pl.ds(r0, rb), :]
        tpos = r0 + lax.broadcasted_iota(jnp.int32, (rb, 1), 0)
        vis = cend <= tpos
        psum = jnp.zeros((rb, ncp), F32)
        outs = []
        for c in range(2):
            halves = []
            for hh in range(2):
                qm = _masked(q[:, c * LANES:(c + 1) * LANES], 64 * hh, 64 * hh + 64)
                s = jnp.where(vis, _dot_nt(qm, kcmp), NEG)
                e = jnp.exp(s - jnp.max(s, axis=-1, keepdims=True))
                p = jnp.where(vis, e / jnp.sum(e, axis=-1, keepdims=True), 0.0)
                psum = psum + p
                halves.append(_dot(p.astype(BF16), vcmp))
            outs.append(_pair(halves[0], halves[1]))
        oc_ref[pl.ds(r0, rb), :] = jnp.concatenate(outs, axis=1)
        hi = psum.astype(BF16)
        lo = (psum - hi.astype(F32)).astype(BF16)
        imp = _dot(hi, ov) + _dot(lo, ov)
        qblk = tpos // NSA_SEL_LEN
        allowed = lane <= qblk
        forced = (lane == 0) | (lane == qblk) | (lane == qblk - 1)
        impf = jnp.where(allowed, jnp.where(forced, FORCE, imp), NEG)
        rank = jnp.zeros((rb, LANES), F32)
        for j in range(n_sel):
            col = impf[:, j:j + 1]
            rank = rank + jnp.where(lane > j, jnp.where(col >= impf, 1.0, 0.0), jnp.where(col > impf, 1.0, 0.0))
        sel_ref[pl.ds(r0, rb), :] = jnp.where(allowed & (rank < top_n), 1.0, 0.0).astype(BF16)
        return carry

    lax.fori_loop(0, q_ref.shape[0] // rb, block, 0)


def _nsa_cmp(nq, nk, nv, wk, wv, pos_k, pos_v, batch, seq):
    d = HEAD_DIM
    nc = seq // NSA_CMP_STRIDE
    ncp = -(-nc // LANES) * LANES
    n_sel = seq // NSA_SEL_LEN
    assert n_sel <= LANES
    top_n = min(NSA_TOP_N, n_sel)

    def seg(a):
        a = a[:, :d].reshape(batch, nc, NSA_CMP_STRIDE * d)
        return jnp.pad(a, ((0, 0), (0, ncp - nc), (0, 0))).reshape(batch * ncp, NSA_CMP_STRIDE * d)

    half = NSA_CMP_STRIDE * d
    dupw = lambda w: jnp.concatenate([w, w], axis=1).reshape(2, half, 2 * d).astype(BF16)
    cpos = NSA_CMP_STRIDE * np.arange(ncp)[:, None] + np.arange(NSA_CMP_LEN)[None, :]
    ovl = np.zeros((ncp, LANES), np.float32)
    for j in range(n_sel):
        ovl[:, j] = (cpos // NSA_SEL_LEN == j).mean(axis=1)
    ovl[nc - 1:, :] = 0.0
    full = lambda a: pl.BlockSpec(a.shape, lambda b: (0,) * a.ndim)
    wk2, wv2 = dupw(wk), dupw(wv)
    pk, pv = pos_k.reshape(1, -1), pos_v.reshape(1, -1)
    ov = jnp.asarray(ovl, BF16)
    return pl.pallas_call(
        functools.partial(_nsa_cmp_kernel, top_n=top_n, n_sel=n_sel),
        grid=(batch,),
        in_specs=[pl.BlockSpec((seq, 256), lambda b: (b, 0)),
                  pl.BlockSpec((ncp, half), lambda b: (b, 0)),
                  pl.BlockSpec((ncp, half), lambda b: (b, 0)),
                  full(wk2), full(wv2), full(pk), full(pv), full(ov)],
        out_specs=[pl.BlockSpec((seq, 256), lambda b: (b, 0)),
                   pl.BlockSpec((seq, LANES), lambda b: (b, 0))],
        out_shape=[jax.ShapeDtypeStruct((batch * seq, 256), F32),
                   jax.ShapeDtypeStruct((batch * seq, LANES), BF16)],
        compiler_params=_params("parallel"),
        name="nsa_cmp",
    )(nq, seg(nk), seg(nv), wk2, wv2, pk, pv, ov)


def _nsa_kernel(q_ref, k_ref, v_ref, sel_ref, g_ref, oc_ref, e_ref, o_ref, acc_ref, *, tq, tk):
    i = pl.program_id(1)
    qh = _half_heads(q_ref[...])
    qp = _qpos(i, tq)
    sel = sel_ref[...]
    nh = NSA_HEADS
    acc_ref[...] = jnp.zeros_like(acc_ref)

    def sel_step(j, st, diag):
        mv = _dot(sel, e_ref[:, pl.ds(pl.multiple_of(j * tk, tk), tk)])
        if diag:
            mv = jnp.where(_kpos(j, tk) <= qp, mv, 0.0)
        mask = mv > 0.5
        k = _ktile(k_ref, j, tk, 128, 256)
        v = _ktile(v_ref, j, tk, 128, 256)
        return tuple(_chain(qh[h], k, v, mask, st[h], acc_ref, h) for h in range(nh))

    nfull = (i * tq) // tk
    st = lax.fori_loop(0, nfull, lambda j, s: sel_step(j, s, False), _init_state(nh, tq))
    st_sel = sel_step(nfull, st, True)

    def win_step(j, st):
        dist = qp - _kpos(j, tk)
        mask = jnp.where(dist >= 0, dist, NSA_WINDOW) < NSA_WINDOW
        k = _ktile(k_ref, j, tk, 256, 384)
        v = _ktile(v_ref, j, tk, 256, 384)
        return tuple(_chain(qh[h], k, v, mask, st[h], acc_ref, nh + h) for h in range(nh))

    wlo = jnp.maximum(i * tq - NSA_WINDOW, 0) // tk
    st_win = lax.fori_loop(wlo, nfull + 1, win_step, _init_state(nh, tq))
    g = g_ref[...]
    oc = oc_ref[...]
    outs = []
    for c in range(2):
        occ = oc[:, c * LANES:(c + 1) * LANES]

        def comb(h):
            o_sel = acc_ref[h] / st_sel[h][1]
            o_win = acc_ref[nh + h] / st_win[h][1]
            return g[:, 3 * h:3 * h + 1] * occ + g[:, 3 * h + 1:3 * h + 2] * o_sel + g[:, 3 * h + 2:3 * h + 3] * o_win

        outs.append(_pair(comb(2 * c), comb(2 * c + 1)))
    o_ref[...] = jnp.concatenate(outs, axis=1).astype(BF16)


def _nsa(nq, nk, nv, sel, gates, ocmp, batch, seq):
    tq, tk = 256, 256
    nb = seq // tq
    expand = np.zeros((LANES, seq), np.float32)
    for j in range(seq // NSA_SEL_LEN):
        expand[j, j * NSA_SEL_LEN:(j + 1) * NSA_SEL_LEN] = 1.0
    e = jnp.asarray(expand, BF16)
    row = lambda w: pl.BlockSpec((tq, w), lambda b, i: (b * nb + i, 0))
    per_b = lambda w: pl.BlockSpec((seq, w), lambda b, i: (b, 0))
    return pl.pallas_call(
        functools.partial(_nsa_kernel, tq=tq, tk=tk),
        grid=(batch, nb),
        in_specs=[row(256), per_b(384), per_b(384), row(LANES), row(LANES), row(256),
                  pl.BlockSpec(e.shape, lambda b, i: (0, 0))],
        out_specs=row(256),
        out_shape=jax.ShapeDtypeStruct((batch * seq, 256), BF16),
        scratch_shapes=[pltpu.VMEM((2 * NSA_HEADS, tq, LANES), F32)],
        compiler_params=_params("parallel", "arbitrary"),
        name="nsa_attn",
    )(nq, nk, nv, sel, gates, ocmp, e)


def _diff_kernel(q_ref, k_ref, v_ref, lam_ref, sg_ref, o_ref, acc_ref, *, lam_init, tq, tk):
    i = pl.program_id(1)
    q = q_ref[...]
    qp4 = _qpos(i, tq, 4)
    lv = lam_ref[...]
    lam = (jnp.exp(jnp.sum(lv[0:1] * lv[1:2], axis=-1, keepdims=True))
           - jnp.exp(jnp.sum(lv[2:3] * lv[3:4], axis=-1, keepdims=True)) + lam_init)
    qs = [jnp.concatenate([_masked(q[:, c * LANES:(c + 1) * LANES], 32 * t, 32 * t + 32) for t in range(4)], axis=0)
          for c in range(2)]
    acc_ref[...] = jnp.zeros_like(acc_ref)

    def step(j, st, diag):
        mask = (_kpos(j, tk) <= qp4) if diag else None
        return tuple(_chain(qs[c], _ktile(k_ref, j, tk, c * LANES, (c + 1) * LANES),
                            _ktile(v_ref, j, tk, c * LANES, (c + 1) * LANES), mask, st[c], acc_ref, c)
                     for c in range(2))

    nfull = (i * tq) // tk
    st = lax.fori_loop(0, nfull, lambda j, s: step(j, s, False), _init_state(2, 4 * tq))
    st = step(nfull, st, True)
    outs = []
    for c in range(2):
        o = acc_ref[c] / st[c][1]
        r = [o[t * tq:(t + 1) * tq] for t in range(4)]
        dd = _pair(r[0] - lam * r[1], r[2] - lam * r[3])
        sq = dd * dd
        lo = _lane_mask(0, 64)
        ms = _pair(jnp.sum(jnp.where(lo, sq, 0.0), axis=-1, keepdims=True),
                   jnp.sum(jnp.where(lo, 0.0, sq), axis=-1, keepdims=True)) * (1.0 / DIFF_V_DIM)
        outs.append(dd * lax.rsqrt(ms + EPS) * sg_ref[...] * (1.0 - lam_init))
    o_ref[...] = jnp.concatenate(outs, axis=1).astype(BF16)


def _diff(dq, dk, dv, lamv, sub_g2, layer, batch, seq):
    tq, tk = 256, 256
    nb = seq // tq
    lam_init = 0.8 - 0.6 * math.exp(-0.3 * layer)
    row = lambda w: pl.BlockSpec((tq, w), lambda b, i: (b * nb + i, 0))
    per_b = lambda w: pl.BlockSpec((seq, w), lambda b, i: (b, 0))
    full = lambda a: pl.BlockSpec(a.shape, lambda b, i: (0,) * a.ndim)
    return pl.pallas_call(
        functools.partial(_diff_kernel, lam_init=lam_init, tq=tq, tk=tk),
        grid=(batch, nb),
        in_specs=[row(256), per_b(256), per_b(256), full(lamv), full(sub_g2)],
        out_specs=row(256),
        out_shape=jax.ShapeDtypeStruct((batch * seq, 256), BF16),
        scratch_shapes=[pltpu.VMEM((2, 4 * tq, LANES), F32)],
        compiler_params=_params("parallel", "arbitrary"),
        name="diff_attn",
    )(dq, dk, dv, lamv, sub_g2)


def _mla_kernel(q_ref, k_ref, v_ref, o_ref, acc_ref, *, tq, tk):
    i = pl.program_id(1)
    qp = _qpos(i, tq)
    nh = MLA_HEADS
    acc_ref[...] = jnp.zeros_like(acc_ref)

    def step(j, st, diag):
        mask = (_kpos(j, tk) <= qp) if diag else None
        return tuple(_chain(q_ref[:, h * LANES:(h + 1) * LANES], _ktile(k_ref, j, tk, h * LANES, (h + 1) * LANES),
                            _ktile(v_ref, j, tk, (h // 2) * LANES, (h // 2 + 1) * LANES), mask, st[h], acc_ref, h)
                     for h in range(nh))

    nfull = (i * tq) // tk
    st = lax.fori_loop(0, nfull, lambda j, s: step(j, s, False), _init_state(nh, tq))
    st = step(nfull, st, True)
    o = [acc_ref[h] / st[h][1] for h in range(nh)]
    o_ref[...] = jnp.concatenate([_pair(o[0], o[1]), _pair(o[2], o[3])], axis=1).astype(BF16)


def _mla(mq, mk, mv, batch, seq):
    tq, tk = 256, 512
    nb = seq // tq
    row = lambda w: pl.BlockSpec((tq, w), lambda b, i: (b * nb + i, 0))
    per_b = lambda w: pl.BlockSpec((seq, w), lambda b, i: (b, 0))
    return pl.pallas_call(
        functools.partial(_mla_kernel, tq=tq, tk=tk),
        grid=(batch, nb),
        in_specs=[row(512), per_b(512), per_b(256)],
        out_specs=row(256),
        out_shape=jax.ShapeDtypeStruct((batch * seq, 256), BF16),
        scratch_shapes=[pltpu.VMEM((MLA_HEADS, tq, LANES), F32)],
        compiler_params=_params("parallel", "arbitrary"),
        name="mla_attn",
    )(mq, mk, mv)


def _swa_kernel(q_ref, k_ref, v_ref, sink_ref, o_ref, acc_ref, *, tq, tk):
    i = pl.program_id(1)
    qh = _half_heads(q_ref[...])
    qp = _qpos(i, tq)
    nh = SWA_HEADS
    acc_ref[...] = jnp.zeros_like(acc_ref)

    def step(j, st):
        dist = qp - _kpos(j, tk)
        mask = jnp.where(dist >= 0, dist, SWA_WINDOW) < SWA_WINDOW
        k = _ktile(k_ref, j, tk, 0, LANES)
        v = _ktile(v_ref, j, tk, 0, LANES)
        return tuple(_chain(qh[c], k, v, mask, st[c], acc_ref, c) for c in range(nh))

    lo = jnp.maximum(i * tq - SWA_WINDOW, 0) // tk
    st = lax.fori_loop(lo, (i * tq) // tk + 1, step, _init_state(nh, tq))
    sk = sink_ref[...]
    o = []
    for c, h in enumerate((0, 2, 1, 3)):
        m, l = st[c]
        sink = sk[:, h:h + 1]
        m2 = jnp.maximum(m, sink)
        sc = jnp.exp(m - m2)
        o.append(acc_ref[c] * sc / (l * sc + jnp.exp(sink - m2)))
    o_ref[...] = jnp.concatenate([_pair(o[0], o[1]), _pair(o[2], o[3])], axis=1).astype(BF16)


def _swa(sq, sk, sv, sinks, batch, seq):
    tq, tk = 256, 256
    nb = seq // tq
    row = lambda w: pl.BlockSpec((tq, w), lambda b, i: (b * nb + i, 0))
    per_b = lambda w: pl.BlockSpec((seq, w), lambda b, i: (b, 0))
    return pl.pallas_call(
        functools.partial(_swa_kernel, tq=tq, tk=tk),
        grid=(batch, nb),
        in_specs=[row(256), per_b(128), per_b(128), pl.BlockSpec(sinks.shape, lambda b, i: (0, 0))],
        out_specs=row(256),
        out_shape=jax.ShapeDtypeStruct((batch * seq, 256), BF16),
        scratch_shapes=[pltpu.VMEM((SWA_HEADS, tq, LANES), F32)],
        compiler_params=_params("parallel", "arbitrary"),
        name="swa_attn",
    )(sq, sk, sv, sinks)


def _outproj_kernel(x_ref, mod_ref, ng_ref, oa_ref, ob_ref, oc_ref, od_ref, w_ref, x1_ref, ht_ref):
    acc = _dot(oa_ref[...], w_ref[0:256, :])
    acc = acc + _dot(ob_ref[...], w_ref[256:512, :])
    acc = acc + _dot(oc_ref[...], w_ref[512:768, :])
    acc = acc + _dot(od_ref[...], w_ref[768:1024, :])
    x1 = x_ref[...] + mod_ref[0, 2:3, :] * acc
    x1_ref[...] = x1
    h = _rms(x1, ng_ref[...]) * (1.0 + mod_ref[0, 4:5, :]) + mod_ref[0, 3:4, :]
    ht_ref[...] = h.T.astype(BF16)


def _outproj(x2d, mod_l, norm_g, oa, ob, oc, od, w_out, seq):
    t, d = x2d.shape
    tm = 256
    tpb = seq // tm
    row = lambda w: pl.BlockSpec((tm, w), lambda i: (i, 0))
    full = lambda a: pl.BlockSpec(a.shape, lambda i: (0,) * a.ndim)
    return pl.pallas_call(
        _outproj_kernel,
        grid=(t // tm,),
        in_specs=[row(d), pl.BlockSpec((1, 6, d), lambda i: (i // tpb, 0, 0)), full(norm_g),
                  row(256), row(256), row(256), row(256), full(w_out)],
        out_specs=[row(d), pl.BlockSpec((d, tm), lambda i: (0, i))],
        out_shape=[jax.ShapeDtypeStruct((t, d), F32), jax.ShapeDtypeStruct((d, t), BF16)],
        compiler_params=_params("parallel"),
        name="outproj",
    )(x2d, mod_l, norm_g, oa, ob, oc, od, w_out)


_CAND_PIECES = [(0, 0, 8), (0, 8, 8), (1, 0, 8), (2, 0, 5), (3, 0, 4), (4, 0, 3), (5, 0, 2), (6, 0, 2), (7, 0, 2),
                (None, 0, 8)]


def _top16(s):
    tb = s.shape[1]
    row16 = lax.broadcasted_iota(jnp.int32, (PEER_TOPK, tb), 0)
    vals = jnp.zeros((PEER_TOPK, tb), F32)
    rank = jnp.full(s.shape, float(PEER_N_KEYS), F32)
    work = s
    for r in range(PEER_TOPK):
        m = jnp.max(work, axis=0, keepdims=True)
        hit = work == m
        rank = jnp.where(hit, float(r), rank)
        work = jnp.where(hit, -jnp.inf, work)
        vals = jnp.where(row16 == r, m, vals)
    return vals, rank


def _router_kernel(ht_ref, wq_ref, k1_ref, k2_ref, ea_ref, n1_ref, r2_ref, eb_ref):
    ht = ht_ref[...]
    tb = ht.shape[1]
    nk = PEER_N_KEYS
    row8 = lax.broadcasted_iota(jnp.int32, (8, tb), 0)
    row16 = lax.broadcasted_iota(jnp.int32, (PEER_TOPK, tb), 0)

    def head(h, carry):
        o = pl.multiple_of(h * 2 * nk, 2 * nk)
        q1 = _dot(wq_ref[pl.ds(o, nk), :], ht).astype(BF16)
        q2 = _dot(wq_ref[pl.ds(o + nk, nk), :], ht).astype(BF16)
        s1 = _dot(k1_ref[...], q1)
        s2 = _dot(k2_ref[...], q2)
        v1, rank1 = _top16(s1)
        v2, rank2 = _top16(s2)
        top = v1[0:1] + v2[0:1]
        pieces = []
        for r1, c0, valid in _CAND_PIECES:
            if r1 is None:
                p = v1[8:16] + v2[0:1]
            else:
                p = v1[r1:r1 + 1] + v2[c0:c0 + 8]
                if valid < 8:
                    p = jnp.where(row8 < valid, p, -jnp.inf)
            pieces.append(p)
        taken = [jnp.zeros((8, tb), F32) for _ in pieces]
        for _ in range(PEER_TOPK):
            m = pieces[0]
            for p in pieces[1:]:
                m = jnp.maximum(m, p)
            m = jnp.max(m, axis=0, keepdims=True)
            for idx in range(len(pieces)):
                hit = pieces[idx] == m
                taken[idx] = jnp.where(hit, 1.0, taken[idx])
                pieces[idx] = jnp.where(hit, -jnp.inf, pieces[idx])
        counts = jnp.zeros((PEER_TOPK, tb), F32)
        z = jnp.zeros((1, tb), F32)
        for idx, (r1, c0, valid) in enumerate(_CAND_PIECES):
            if r1 is None:
                cell = v1[8:16] + v2[0:1]
                counts = counts + jnp.concatenate([jnp.zeros((8, tb), F32), taken[idx]], axis=0)
            else:
                cell = v1[r1:r1 + 1] + v2[c0:c0 + 8]
                n = jnp.sum(taken[idx], axis=0, keepdims=True)
                counts = counts + jnp.where(row16 == r1, n, 0.0)
            z = z + jnp.sum(jnp.where(taken[idx] > 0.5, jnp.exp(cell - top), 0.0), axis=0, keepdims=True)
        n1 = jnp.zeros((nk, tb), F32)
        for r in range(PEER_TOPK):
            n1 = jnp.where(rank1 == float(r), counts[r:r + 1], n1)
        ea_ref[h] = jnp.exp(s1 - v1[0:1])
        n1_ref[h] = n1
        r2_ref[h] = rank2.astype(BF16)
        eb_ref[h] = (jnp.exp(s2 - v2[0:1]) / z).astype(BF16)
        return carry

    lax.fori_loop(0, PEER_HEADS, head, 0)


def _router(ht, wq_t, k1, k2):
    d, t = ht.shape
    tb = 256
    full = lambda a: pl.BlockSpec(a.shape, lambda i: (0,) * a.ndim)
    out = pl.BlockSpec((PEER_HEADS, PEER_N_KEYS, tb), lambda i: (0, 0, i))
    return pl.pallas_call(
        _router_kernel,
        grid=(t // tb,),
        in_specs=[pl.BlockSpec((d, tb), lambda i: (0, i)), full(wq_t), full(k1), full(k2)],
        out_specs=[out, out, out, out],
        out_shape=[jax.ShapeDtypeStruct((PEER_HEADS, PEER_N_KEYS, t), dt) for dt in (F32, F32, BF16, BF16)],
        compiler_params=_params("parallel"),
        name="peer_router",
    )(ht, wq_t, k1, k2)


def _gelu_tanh(x):
    k = math.sqrt(2.0 / math.pi)
    return 0.5 * x * (1.0 + jnp.tanh(x * (k + (k * 0.044715) * (x * x))))


def _peer_kernel(ht_ref, u_ref, vt_ref, ea_ref, n1_ref, r2_ref, eb_ref, x_ref, mod_ref, fg_ref, o_ref, acc_ref,
                 *, final, chunk):
    e = pl.program_id(1)
    nk = PEER_N_KEYS
    n_i1 = u_ref.shape[0] // nk

    @pl.when(e == 0)
    def _():
        acc_ref[...] = jnp.zeros_like(acc_ref)

    for c in range(ht_ref.shape[1] // chunk):
        lanes = pl.ds(c * chunk, chunk)
        act = _gelu_tanh(_dot(u_ref[...], ht_ref[:, lanes]))
        rows = []
        for j in range(n_i1):
            m = None
            for h in range(PEER_HEADS):
                n_row = n1_ref[h, 0, j:j + 1, lanes].astype(BF16)
                ea_row = ea_ref[h, 0, j:j + 1, lanes].astype(BF16)
                term = jnp.where(r2_ref[h, :, lanes] < n_row, eb_ref[h, :, lanes], jnp.zeros((), BF16)) * ea_row
                m = term if m is None else m + term
            rows.append((m.astype(F32) * act[j * nk:(j + 1) * nk]).astype(BF16))
        wt = jnp.concatenate(rows, axis=0) if n_i1 > 1 else rows[0]
        acc_ref[:, lanes] += _dot(vt_ref[...], wt)

    @pl.when(e == pl.num_programs(1) - 1)
    def _():
        y = x_ref[...] + mod_ref[0, 5:6, :] * acc_ref[...].T
        if final:
            y = _rms(y, fg_ref[...])
        o_ref[...] = y


def _peer(ht, u_bf, vt_bf, ea, n1, r2, eb, x1, mod_l, final_g, seq, final):
    d, t = ht.shape
    n_exp = u_bf.shape[0]
    tb = 1024 if seq % 1024 == 0 else 512
    eb_blk = 512
    n_i1 = eb_blk // PEER_N_KEYS
    tpb = seq // tb
    per_i1 = lambda a: a.reshape(PEER_HEADS, PEER_N_KEYS // n_i1, n_i1, t)
    i1_spec = pl.BlockSpec((PEER_HEADS, 1, n_i1, tb), lambda i, e: (0, e, 0, i))
    tok3 = pl.BlockSpec((PEER_HEADS, PEER_N_KEYS, tb), lambda i, e: (0, 0, i))
    return pl.pallas_call(
        functools.partial(_peer_kernel, final=final, chunk=512),
        grid=(t // tb, n_exp // eb_blk),
        in_specs=[pl.BlockSpec((d, tb), lambda i, e: (0, i)),
                  pl.BlockSpec((eb_blk, d), lambda i, e: (e, 0)),
                  pl.BlockSpec((d, eb_blk), lambda i, e: (0, e)),
                  i1_spec, i1_spec, tok3, tok3,
                  pl.BlockSpec((tb, d), lambda i, e: (i, 0)),
                  pl.BlockSpec((1, 6, d), lambda i, e: (i // tpb, 0, 0)),
                  pl.BlockSpec(final_g.shape, lambda i, e: (0, 0))],
        out_specs=pl.BlockSpec((tb, d), lambda i, e: (i, 0)),
        out_shape=jax.ShapeDtypeStruct((t, d), F32),
        scratch_shapes=[pltpu.VMEM((d, tb), F32)],
        compiler_params=_params("parallel", "arbitrary"),
        name="peer_experts",
    )(ht, u_bf, vt_bf, per_i1(ea), per_i1(n1), r2, eb, x1, mod_l, final_g)


def kernel(x, c, ada_w, ada_b, norm_mix_g, norm_ffn_g, w_in, nsa_cmp_pos_k, nsa_cmp_pos_v, nsa_cmp_wk, nsa_cmp_wv, diff_lam_q1, diff_lam_k1, diff_lam_q2, diff_lam_k2, diff_sub_g, mla_q_norm_g, mla_w_uq, mla_kv_norm_g, mla_w_ukv, swa_sinks, w_out, peer_w_q, peer_sub_k1, peer_sub_k2, peer_u, peer_v, final_g):
    batch, seq, d = x.shape
    depth = w_in.shape[0]
    assert seq % 512 == 0
    x2d = x.reshape(batch * seq, d)
    mod = _adaln(c, ada_w, ada_b).reshape(depth, batch, 6, d)
    table = _rope_table(seq)
    mixw = 4 * HEAD_DIM
    swa_rows = 3 * mixw + np.concatenate([h * HEAD_DIM + np.arange(HEAD_DIM) for h in (0, 2, 1, 3)])
    out_rows = jnp.asarray(np.concatenate([np.arange(3 * mixw), swa_rows]), jnp.int32)
    pad128 = lambda v: jnp.pad(v, (0, LANES - v.shape[0])).reshape(1, LANES)
    fg = final_g.reshape(1, d)
    for l in range(depth):
        w_big = _take_cols(w_in[l], _IN_IDX, _IN_SGN)
        wuq = _take_cols(mla_w_uq[l], _UQ_IDX, _UQ_SGN)
        wukv = _take_cols(mla_w_ukv[l], _UKV_IDX, _UKV_SGN)
        (nq, nk, nv, ng, dq, dk, dv, mq, mk, mv, sq, sk, sv) = _inproj(
            x2d, mod[l], norm_mix_g[l].reshape(1, d), w_big, table, wuq, wukv,
            mla_q_norm_g[l].reshape(1, -1), mla_kv_norm_g[l].reshape(1, -1), seq)
        ocmp, sel = _nsa_cmp(nq, nk, nv, nsa_cmp_wk[l], nsa_cmp_wv[l], nsa_cmp_pos_k[l], nsa_cmp_pos_v[l], batch, seq)
        o_a = _nsa(nq, nk, nv, sel, ng, ocmp, batch, seq)
        lamv = jnp.concatenate([pad128(diff_lam_q1[l]), pad128(diff_lam_k1[l]),
                                pad128(diff_lam_q2[l]), pad128(diff_lam_k2[l])], axis=0)
        sub_g2 = jnp.concatenate([diff_sub_g[l], diff_sub_g[l]]).reshape(1, LANES)
        o_b = _diff(dq, dk, dv, lamv, sub_g2, l, batch, seq)
        o_c = _mla(mq, mk, mv, batch, seq)
        o_d = _swa(sq, sk, sv, pad128(swa_sinks[l]), batch, seq)
        w_o = jnp.take(w_out[l], out_rows, axis=0).astype(BF16)
        x1, ht = _outproj(x2d, mod[l], norm_ffn_g[l].reshape(1, d), o_a, o_b, o_c, o_d, w_o, seq)
        ea, n1, r2, eb = _router(ht, peer_w_q[l].T.astype(BF16), peer_sub_k1[l].astype(BF16),
                                 peer_sub_k2[l].astype(BF16))
        x2d = _peer(ht, peer_u[l].astype(BF16), peer_v[l].T.astype(BF16), ea, n1, r2, eb, x1, mod[l], fg,
                    seq, final=(l == depth - 1))
    return x2d.reshape(batch, seq, d)
```

```python
import functools
import math

import numpy as np
import jax
import jax.numpy as jnp
from jax import lax
from jax.experimental import pallas as pl
from jax.experimental.pallas import tpu as pltpu

F32 = jnp.float32
BF16 = jnp.bfloat16

HEAD_DIM = 64
ROPE_THETA = 10000.0
EPS = 1e-6
NEG = -1e30
FORCE = 1e4

NSA_HEADS = 4
NSA_CMP_LEN = 32
NSA_CMP_STRIDE = 16
NSA_SEL_LEN = 64
NSA_TOP_N = 16
NSA_WINDOW = 512

DIFF_HEADS = 4
DIFF_QK_DIM = 32
DIFF_V_DIM = 64

MLA_HEADS = 4
MLA_Q_RANK = 256
MLA_KV_RANK = 128
MLA_NOPE_DIM = 64
MLA_ROPE_DIM = 32
MLA_V_DIM = 64

SWA_HEADS = 4
SWA_KV_HEADS = 2
SWA_WINDOW = 128

PEER_HEADS = 8
PEER_N_KEYS = 128
PEER_TOPK = 16
PEER_QUERY_DIM = 256

LANES = 128
VMEM_LIMIT = 56 * 1024 * 1024


def _dot(a, b):
    return jnp.dot(a, b, preferred_element_type=F32)


def _dot_nt(a, b):
    return lax.dot_general(a, b, (((1,), (1,)), ((), ())), preferred_element_type=F32)


def _params(*sem):
    return pltpu.CompilerParams(dimension_semantics=sem, vmem_limit_bytes=VMEM_LIMIT)


def _rms(x, g):
    return x * lax.rsqrt(jnp.mean(x * x, axis=-1, keepdims=True) + EPS) * g


def _rot_idx(base, dim):
    half = dim // 2
    idx = np.concatenate([base + half + np.arange(half), base + np.arange(half)])
    sgn = np.concatenate([-np.ones(half), np.ones(half)])
    return idx, sgn


def _in_plan():
    d = HEAD_DIM
    nsa0 = 0
    nsa_cols = NSA_HEADS * d + 6 * d + 3 * NSA_HEADS
    diff0 = nsa0 + nsa_cols
    diff_cols = 2 * DIFF_HEADS * 2 * DIFF_QK_DIM + DIFF_HEADS * DIFF_V_DIM
    mla0 = diff0 + diff_cols
    mla_cols = MLA_Q_RANK + MLA_KV_RANK + MLA_ROPE_DIM
    swa0 = mla0 + mla_cols
    idx, sgn, off = [], [], {}

    def add(name, i, s=None):
        i = np.asarray(i, np.int64)
        s = np.ones(len(i)) if s is None else np.asarray(s, np.float64)
        pad = (-len(i)) % LANES
        off[name] = sum(len(a) for a in idx)
        idx.append(np.concatenate([i, np.zeros(pad, np.int64)]))
        sgn.append(np.concatenate([s, np.zeros(pad)]))

    def heads_rot(base, nheads, dim):
        ii, ss = zip(*[_rot_idx(base + h * dim, dim) for h in range(nheads)])
        return np.concatenate(ii), np.concatenate(ss)

    nq = nsa0 + np.arange(NSA_HEADS * d)
    add("nq", nq)
    add("nqr", *heads_rot(nsa0, NSA_HEADS, d))
    kb = nsa0 + NSA_HEADS * d
    kc, vc, ksl, vsl, kw, vw = [kb + j * d for j in range(6)]
    dup = lambda b: np.concatenate([b + np.arange(d), b + np.arange(d)])
    add("nk", np.concatenate([dup(kc), dup(ksl), dup(kw)]))
    kr = [_rot_idx(b, d) for b in (kc, kc, ksl, ksl, kw, kw)]
    add("nkr", np.concatenate([a for a, _ in kr]), np.concatenate([b for _, b in kr]))
    add("nv", np.concatenate([dup(vc), dup(vsl), dup(vw)]))
    add("ng", kb + 6 * d + np.arange(3 * NSA_HEADS))
    nqk = DIFF_HEADS * 2 * DIFF_QK_DIM
    add("dq", diff0 + np.arange(nqk))
    add("dqr", *heads_rot(diff0, 2 * DIFF_HEADS, DIFF_QK_DIM))
    add("dk", diff0 + nqk + np.arange(nqk))
    add("dkr", *heads_rot(diff0 + nqk, 2 * DIFF_HEADS, DIFF_QK_DIM))
    add("dv", diff0 + 2 * nqk + np.arange(DIFF_HEADS * DIFF_V_DIM))
    add("mcq", mla0 + np.arange(MLA_Q_RANK))
    add("mckv", mla0 + MLA_Q_RANK + np.arange(MLA_KV_RANK))
    kr0 = mla0 + MLA_Q_RANK + MLA_KV_RANK
    z64 = np.zeros(MLA_NOPE_DIM, np.int64)
    add("mkr", np.concatenate([z64, kr0 + np.arange(MLA_ROPE_DIM)]),
        np.concatenate([np.zeros(MLA_NOPE_DIM), np.ones(MLA_ROPE_DIM)]))
    ri, rs = _rot_idx(kr0, MLA_ROPE_DIM)
    add("mkrr", np.concatenate([z64, ri]), np.concatenate([np.zeros(MLA_NOPE_DIM), rs]))
    order = [0, 2, 1, 3]
    add("sq", np.concatenate([swa0 + h * d + np.arange(d) for h in order]))
    sr = [_rot_idx(swa0 + h * d, d) for h in order]
    add("sqr", np.concatenate([a for a, _ in sr]), np.concatenate([b for _, b in sr]))
    sk0 = swa0 + SWA_HEADS * d
    add("sk", sk0 + np.arange(SWA_KV_HEADS * d))
    add("skr", *heads_rot(sk0, SWA_KV_HEADS, d))
    add("sv", sk0 + SWA_KV_HEADS * d + np.arange(SWA_KV_HEADS * d))
    return np.concatenate(idx), np.concatenate(sgn), off


_IN_IDX, _IN_SGN, _OFF = _in_plan()
_NCOLS = len(_IN_IDX)


def _mla_plans():
    qd = MLA_NOPE_DIM + MLA_ROPE_DIM
    qi, qs, ri, rs = [], [], [], []
    for h in range(MLA_HEADS):
        b = h * qd
        qi += [b + np.arange(qd), np.zeros(LANES - qd, np.int64)]
        qs += [np.ones(qd), np.zeros(LANES - qd)]
        a, s = _rot_idx(b + MLA_NOPE_DIM, MLA_ROPE_DIM)
        ri += [np.zeros(MLA_NOPE_DIM, np.int64), a, np.zeros(LANES - qd, np.int64)]
        rs += [np.zeros(MLA_NOPE_DIM), s, np.zeros(LANES - qd)]
    kd = MLA_NOPE_DIM + MLA_V_DIM
    ki, ks, vi = [], [], []
    for h in range(MLA_HEADS):
        ki += [h * kd + np.arange(MLA_NOPE_DIM), np.zeros(LANES - MLA_NOPE_DIM, np.int64)]
        ks += [np.ones(MLA_NOPE_DIM), np.zeros(LANES - MLA_NOPE_DIM)]
        vi += [h * kd + MLA_NOPE_DIM + np.arange(MLA_V_DIM)]
    uq_idx = np.concatenate(qi + ri)
    uq_sgn = np.concatenate(qs + rs)
    ukv_idx = np.concatenate(ki + vi)
    ukv_sgn = np.concatenate(ks + [np.ones(MLA_HEADS * MLA_V_DIM)])
    return uq_idx, uq_sgn, ukv_idx, ukv_sgn


_UQ_IDX, _UQ_SGN, _UKV_IDX, _UKV_SGN = _mla_plans()


def _take_cols(w, idx, sgn):
    return (jnp.take(w, jnp.asarray(idx, jnp.int32), axis=1) * jnp.asarray(sgn, F32)[None, :]).astype(BF16)


def _rope_table(seq):
    def cs(dim):
        inv = 1.0 / (ROPE_THETA ** (jnp.arange(0, dim, 2, dtype=F32) / dim))
        ang = jnp.arange(seq, dtype=F32)[:, None] * inv[None, :]
        c, s = jnp.cos(ang), jnp.sin(ang)
        return jnp.concatenate([c, c], 1), jnp.concatenate([s, s], 1)
    ch, sh = cs(HEAD_DIM)
    cd, sd = cs(DIFF_QK_DIM)
    cm, sm = cs(MLA_ROPE_DIM)
    one = jnp.ones((seq, MLA_NOPE_DIM), F32)
    z64 = jnp.zeros((seq, MLA_NOPE_DIM), F32)
    z32 = jnp.zeros((seq, LANES - MLA_NOPE_DIM - MLA_ROPE_DIM), F32)
    parts = [jnp.tile(ch, (1, 2)), jnp.tile(sh, (1, 2)), jnp.tile(cd, (1, 4)), jnp.tile(sd, (1, 4)),
             jnp.concatenate([one, cm, z32], 1), jnp.concatenate([z64, sm, z32], 1),
             jnp.concatenate([z64, cm, z32], 1), jnp.concatenate([z64, sm, z32], 1)]
    return jnp.concatenate(parts, 1)


def _adaln_kernel(c_ref, w_ref, b_ref, o_ref):
    c = c_ref[...]
    sc = (c * jax.nn.sigmoid(c)).astype(BF16)
    o_ref[0] = _dot(sc, w_ref[0].astype(BF16)) + b_ref[0]


def _adaln(c, ada_w, ada_b):
    nl, d, n6 = ada_w.shape
    b = c.shape[0]
    tn = 1536
    return pl.pallas_call(
        _adaln_kernel,
        grid=(nl, n6 // tn),
        in_specs=[pl.BlockSpec((b, d), lambda l, j: (0, 0)),
                  pl.BlockSpec((1, d, tn), lambda l, j: (l, 0, j)),
                  pl.BlockSpec((1, 1, tn), lambda l, j: (l, 0, j))],
        out_specs=pl.BlockSpec((1, b, tn), lambda l, j: (l, 0, j)),
        out_shape=jax.ShapeDtypeStruct((nl, b, n6), F32),
        compiler_params=_params("parallel", "parallel"),
        name="adaln",
    )(c, ada_w, ada_b.reshape(nl, 1, n6))


def _inproj_kernel(x_ref, mod_ref, ng_ref, w_ref, tab_ref, wuq_ref, wukv_ref, gq_ref, gkv_ref,
                   nq_ref, nk_ref, nv_ref, ngo_ref, dq_ref, dk_ref, dv_ref,
                   mq_ref, mk_ref, mv_ref, sq_ref, sk_ref, sv_ref):
    x = x_ref[...]
    h = _rms(x, ng_ref[...]) * (1.0 + mod_ref[0, 1:2, :]) + mod_ref[0, 0:1, :]
    hb = h.astype(BF16)

    def mm(name, width):
        o = _OFF[name]
        return _dot(hb, w_ref[:, o:o + width])

    def tab(j, reps):
        t = tab_ref[:, j * LANES:(j + 1) * LANES]
        return t if reps == 1 else jnp.concatenate([t] * reps, axis=1)

    def rope(name, rname, width, cj, scale=1.0):
        r = mm(name, width) * tab(cj, width // LANES) + mm(rname, width) * tab(cj + 1, width // LANES)
        return r if scale == 1.0 else r * scale

    d = HEAD_DIM
    nq_ref[...] = rope("nq", "nqr", 256, 0, d ** -0.5).astype(BF16)
    nk_ref[...] = rope("nk", "nkr", 384, 0).astype(BF16)
    nv_ref[...] = mm("nv", 384).astype(BF16)
    ngo_ref[...] = jax.nn.sigmoid(mm("ng", LANES))
    dq_ref[...] = rope("dq", "dqr", 256, 2, DIFF_QK_DIM ** -0.5).astype(BF16)
    dk_ref[...] = rope("dk", "dkr", 256, 2).astype(BF16)
    dv_ref[...] = mm("dv", 256).astype(BF16)
    cq = _rms(mm("mcq", MLA_Q_RANK), gq_ref[...]).astype(BF16)
    nh = MLA_HEADS * LANES
    qa = _dot(cq, wuq_ref[:, 0:nh])
    qb = _dot(cq, wuq_ref[:, nh:2 * nh])
    mq = (qa * tab(4, MLA_HEADS) + qb * tab(5, MLA_HEADS)) * ((MLA_NOPE_DIM + MLA_ROPE_DIM) ** -0.5)
    mq_ref[...] = mq.astype(BF16)
    ckv = _rms(mm("mckv", MLA_KV_RANK), gkv_ref[...]).astype(BF16)
    kk = _dot(ckv, wukv_ref[:, 0:nh])
    kr = mm("mkr", LANES) * tab(6, 1) + mm("mkrr", LANES) * tab(7, 1)
    mk_ref[...] = (kk + jnp.concatenate([kr] * MLA_HEADS, axis=1)).astype(BF16)
    mv_ref[...] = _dot(ckv, wukv_ref[:, nh:nh + MLA_HEADS * MLA_V_DIM]).astype(BF16)
    sq_ref[...] = rope("sq", "sqr", 256, 0, d ** -0.5).astype(BF16)
    sk_ref[...] = rope("sk", "skr", 128, 0).astype(BF16)
    sv_ref[...] = mm("sv", 128).astype(BF16)


def _inproj(x2d, mod_l, norm_g, w_big, table, wuq, wukv, gq, gkv, seq):
    t, d = x2d.shape
    tm = 256
    tpb = seq // tm
    widths = [256, 384, 384, 128, 256, 256, 256, 512, 512, 256, 256, 128, 128]
    dts = [BF16, BF16, BF16, F32, BF16, BF16, BF16, BF16, BF16, BF16, BF16, BF16, BF16]
    full = lambda a: pl.BlockSpec(a.shape, lambda i: (0,) * a.ndim)
    return pl.pallas_call(
        _inproj_kernel,
        grid=(t // tm,),
        in_specs=[pl.BlockSpec((tm, d), lambda i: (i, 0)),
                  pl.BlockSpec((1, 6, d), lambda i: (i // tpb, 0, 0)),
                  full(norm_g), full(w_big),
                  pl.BlockSpec((tm, table.shape[1]), lambda i: (i % tpb, 0)),
                  full(wuq), full(wukv), full(gq), full(gkv)],
        out_specs=[pl.BlockSpec((tm, w), lambda i: (i, 0)) for w in widths],
        out_shape=[jax.ShapeDtypeStruct((t, w), dt) for w, dt in zip(widths, dts)],
        compiler_params=_params("parallel"),
        name="inproj",
    )(x2d, mod_l, norm_g, w_big, table, wuq, wukv, gq, gkv)


def _lane_mask(lo, hi):
    lane = lax.broadcasted_iota(jnp.int32, (1, LANES), 1)
    return (lane >= lo) & (lane < hi)


def _masked(q, lo, hi):
    return jnp.where(_lane_mask(lo, hi), q, jnp.zeros_like(q))


def _chain(q, k, v, mask, state, acc_ref, c):
    m, l = state
    s = _dot_nt(q, k)
    if mask is not None:
        s = jnp.where(mask, s, NEG)
    m2 = jnp.maximum(m, jnp.max(s, axis=-1, keepdims=True))
    a = jnp.exp(m - m2)
    p = jnp.exp(s - m2)
    acc_ref[c] = a * acc_ref[c] + _dot(p.astype(BF16), v)
    return m2, a * l + jnp.sum(p, axis=-1, keepdims=True)


def _init_state(n, rows):
    return tuple((jnp.full((rows, 1), NEG, F32), jnp.zeros((rows, 1), F32)) for _ in range(n))


def _ktile(ref, j, tk, c0, c1):
    return ref[pl.ds(pl.multiple_of(j * tk, tk), tk), c0:c1]


def _qpos(i, tq, reps=1):
    p = i * tq + lax.broadcasted_iota(jnp.int32, (tq, 1), 0)
    return p if reps == 1 else jnp.concatenate([p] * reps, axis=0)


def _kpos(j, tk):
    return j * tk + lax.broadcasted_iota(jnp.int32, (1, tk), 1)


def _half_heads(q):
    return [_masked(q[:, c * LANES:(c + 1) * LANES], 64 * hh, 64 * hh + 64) for c in range(2) for hh in range(2)]


def _pair(lo_val, hi_val):
    return jnp.where(_lane_mask(0, 64), lo_val, hi_val)


def _nsa_cmp_kernel(q_ref, kc_ref, vc_ref, wk_ref, wv_ref, pk_ref, pv_ref, ov_ref, oc_ref, sel_ref, *, top_n, n_sel):
    half = wk_ref.shape[1]

    def compress(x_ref, w_ref, p_ref):
        x = x_ref[...]
        a = _dot(x, w_ref[0])
        b = _dot(x, w_ref[1])
        p = jnp.broadcast_to(p_ref[...], (8, 2 * half)).astype(BF16)
        const = (_dot(p[:, 0:half], w_ref[0]) + _dot(p[:, half:2 * half], w_ref[1]))[0:1]
        return a + jnp.concatenate([b[1:], b[:1]], axis=0) + const

    kcmp = compress(kc_ref, wk_ref, pk_ref).astype(BF16)
    vcmp = compress(vc_ref, wv_ref, pv_ref).astype(BF16)
    ncp = kcmp.shape[0]
    ov = ov_ref[...]
    rb = 256
    cend = NSA_CMP_STRIDE * lax.broadcasted_iota(jnp.int32, (1, ncp), 1) + (NSA_CMP_LEN - 1)
    lane = lax.broadcasted_iota(jnp.int32, (1, LANES), 1)

    def block(r, carry):
        r0 = pl.multiple_of(r * rb, rb)
        q = q_ref[pl.ds(r0, rb), :]
        tpos = r0 + lax.broadcasted_iota(jnp.int32, (rb, 1), 0)
        vis = cend <= tpos
        psum = jnp.zeros((rb, ncp), F32)
        outs = []
        for c in range(2):
            halves = []
            for hh in range(2):
                qm = _masked(q[:, c * LANES:(c + 1) * LANES], 64 * hh, 64 * hh + 64)
                s = jnp.where(vis, _dot_nt(qm, kcmp), NEG)
                e = jnp.exp(s - jnp.max(s, axis=-1, keepdims=True))
                p = jnp.where(vis, e / jnp.sum(e, axis=-1, keepdims=True), 0.0)
                psum = psum + p
                halves.append(_dot(p.astype(BF16), vcmp))
            outs.append(_pair(halves[0], halves[1]))
        oc_ref[pl.ds(r0, rb), :] = jnp.concatenate(outs, axis=1)
        hi = psum.astype(BF16)
        lo = (psum - hi.astype(F32)).astype(BF16)
        imp = _dot(hi, ov) + _dot(lo, ov)
        qblk = tpos // NSA_SEL_LEN
        allowed = lane <= qblk
        forced = (lane == 0) | (lane == qblk) | (lane == qblk - 1)
        impf = jnp.where(allowed, jnp.where(forced, FORCE, imp), NEG)
        rank = jnp.zeros((rb, LANES), F32)
        for j in range(n_sel):
            col = impf[:, j:j + 1]
            rank = rank + jnp.where(lane > j, jnp.where(col >= impf, 1.0, 0.0), jnp.where(col > impf, 1.0, 0.0))
        sel_ref[pl.ds(r0, rb), :] = jnp.where(allowed & (rank < top_n), 1.0, 0.0).astype(BF16)
        return carry

    lax.fori_loop(0, q_ref.shape[0] // rb, block, 0)


def _nsa_cmp(nq, nk, nv, wk, wv, pos_k, pos_v, batch, seq):
    d = HEAD_DIM
    nc = seq // NSA_CMP_STRIDE
    ncp = -(-nc // LANES) * LANES
    n_sel = seq // NSA_SEL_LEN
    assert n_sel <= LANES
    top_n = min(NSA_TOP_N, n_sel)

    def seg(a):
        a = a[:, :d].reshape(batch, nc, NSA_CMP_STRIDE * d)
        return jnp.pad(a, ((0, 0), (0, ncp - nc), (0, 0))).reshape(batch * ncp, NSA_CMP_STRIDE * d)

    half = NSA_CMP_STRIDE * d
    dupw = lambda w: jnp.concatenate([w, w], axis=1).reshape(2, half, 2 * d).astype(BF16)
    cpos = NSA_CMP_STRIDE * np.arange(ncp)[:, None] + np.arange(NSA_CMP_LEN)[None, :]
    ovl = np.zeros((ncp, LANES), np.float32)
    for j in range(n_sel):
        ovl[:, j] = (cpos // NSA_SEL_LEN == j).mean(axis=1)
    ovl[nc - 1:, :] = 0.0
    full = lambda a: pl.BlockSpec(a.shape, lambda b: (0,) * a.ndim)
    wk2, wv2 = dupw(wk), dupw(wv)
    pk, pv = pos_k.reshape(1, -1), pos_v.reshape(1, -1)
    ov = jnp.asarray(ovl, BF16)
    return pl.pallas_call(
        functools.partial(_nsa_cmp_kernel, top_n=top_n, n_sel=n_sel),
        grid=(batch,),
        in_specs=[pl.BlockSpec((seq, 256), lambda b: (b, 0)),
                  pl.BlockSpec((ncp, half), lambda b: (b, 0)),
                  pl.BlockSpec((ncp, half), lambda b: (b, 0)),
                  full(wk2), full(wv2), full(pk), full(pv), full(ov)],
        out_specs=[pl.BlockSpec((seq, 256), lambda b: (b, 0)),
                   pl.BlockSpec((seq, LANES), lambda b: (b, 0))],
        out_shape=[jax.ShapeDtypeStruct((batch * seq, 256), F32),
                   jax.ShapeDtypeStruct((batch * seq, LANES), BF16)],
        compiler_params=_params("parallel"),
        name="nsa_cmp",
    )(nq, seg(nk), seg(nv), wk2, wv2, pk, pv, ov)


def _nsa_kernel(q_ref, k_ref, v_ref, sel_ref, g_ref, oc_ref, e_ref, o_ref, acc_ref, *, tq, tk):
    i = pl.program_id(1)
    qh = _half_heads(q_ref[...])
    qp = _qpos(i, tq)
    sel = sel_ref[...]
    nh = NSA_HEADS
    acc_ref[...] = jnp.zeros_like(acc_ref)

    def sel_step(j, st, diag):
        mv = _dot(sel, e_ref[:, pl.ds(pl.multiple_of(j * tk, tk), tk)])
        if diag:
            mv = jnp.where(_kpos(j, tk) <= qp, mv, 0.0)
        mask = mv > 0.5
        k = _ktile(k_ref, j, tk, 128, 256)
        v = _ktile(v_ref, j, tk, 128, 256)
        return tuple(_chain(qh[h], k, v, mask, st[h], acc_ref, h) for h in range(nh))

    nfull = (i * tq) // tk
    st = lax.fori_loop(0, nfull, lambda j, s: sel_step(j, s, False), _init_state(nh, tq))
    st_sel = sel_step(nfull, st, True)

    def win_step(j, st):
        dist = qp - _kpos(j, tk)
        mask = jnp.where(dist >= 0, dist, NSA_WINDOW) < NSA_WINDOW
        k = _ktile(k_ref, j, tk, 256, 384)
        v = _ktile(v_ref, j, tk, 256, 384)
        return tuple(_chain(qh[h], k, v, mask, st[h], acc_ref, nh + h) for h in range(nh))

    wlo = jnp.maximum(i * tq - NSA_WINDOW, 0) // tk
    st_win = lax.fori_loop(wlo, nfull + 1, win_step, _init_state(nh, tq))
    g = g_ref[...]
    oc = oc_ref[...]
    outs = []
    for c in range(2):
        occ = oc[:, c * LANES:(c + 1) * LANES]

        def comb(h):
            o_sel = acc_ref[h] / st_sel[h][1]
            o_win = acc_ref[nh + h] / st_win[h][1]
            return g[:, 3 * h:3 * h + 1] * occ + g[:, 3 * h + 1:3 * h + 2] * o_sel + g[:, 3 * h + 2:3 * h + 3] * o_win

        outs.append(_pair(comb(2 * c), comb(2 * c + 1)))
    o_ref[...] = jnp.concatenate(outs, axis=1).astype(BF16)


def _nsa(nq, nk, nv, sel, gates, ocmp, batch, seq):
    tq, tk = 256, 256
    nb = seq // tq
    expand = np.zeros((LANES, seq), np.float32)
    for j in range(seq // NSA_SEL_LEN):
        expand[j, j * NSA_SEL_LEN:(j + 1) * NSA_SEL_LEN] = 1.0
    e = jnp.asarray(expand, BF16)
    row = lambda w: pl.BlockSpec((tq, w), lambda b, i: (b * nb + i, 0))
    per_b = lambda w: pl.BlockSpec((seq, w), lambda b, i: (b, 0))
    return pl.pallas_call(
        functools.partial(_nsa_kernel, tq=tq, tk=tk),
        grid=(batch, nb),
        in_specs=[row(256), per_b(384), per_b(384), row(LANES), row(LANES), row(256),
                  pl.BlockSpec(e.shape, lambda b, i: (0, 0))],
        out_specs=row(256),
        out_shape=jax.ShapeDtypeStruct((batch * seq, 256), BF16),
        scratch_shapes=[pltpu.VMEM((2 * NSA_HEADS, tq, LANES), F32)],
        compiler_params=_params("parallel", "arbitrary"),
        name="nsa_attn",
    )(nq, nk, nv, sel, gates, ocmp, e)


def _diff_kernel(q_ref, k_ref, v_ref, lam_ref, sg_ref, o_ref, acc_ref, *, lam_init, tq, tk):
    i = pl.program_id(1)
    q = q_ref[...]
    qp4 = _qpos(i, tq, 4)
    lv = lam_ref[...]
    lam = (jnp.exp(jnp.sum(lv[0:1] * lv[1:2], axis=-1, keepdims=True))
           - jnp.exp(jnp.sum(lv[2:3] * lv[3:4], axis=-1, keepdims=True)) + lam_init)
    qs = [jnp.concatenate([_masked(q[:, c * LANES:(c + 1) * LANES], 32 * t, 32 * t + 32) for t in range(4)], axis=0)
          for c in range(2)]
    acc_ref[...] = jnp.zeros_like(acc_ref)

    def step(j, st, diag):
        mask = (_kpos(j, tk) <= qp4) if diag else None
        return tuple(_chain(qs[c], _ktile(k_ref, j, tk, c * LANES, (c + 1) * LANES),
                            _ktile(v_ref, j, tk, c * LANES, (c + 1) * LANES), mask, st[c], acc_ref, c)
                     for c in range(2))

    nfull = (i * tq) // tk
    st = lax.fori_loop(0, nfull, lambda j, s: step(j, s, False), _init_state(2, 4 * tq))
    st = step(nfull, st, True)
    outs = []
    for c in range(2):
        o = acc_ref[c] / st[c][1]
        r = [o[t * tq:(t + 1) * tq] for t in range(4)]
        dd = _pair(r[0] - lam * r[1], r[2] - lam * r[3])
        sq = dd * dd
        lo = _lane_mask(0, 64)
        ms = _pair(jnp.sum(jnp.where(lo, sq, 0.0), axis=-1, keepdims=True),
                   jnp.sum(jnp.where(lo, 0.0, sq), axis=-1, keepdims=True)) * (1.0 / DIFF_V_DIM)
        outs.append(dd * lax.rsqrt(ms + EPS) * sg_ref[...] * (1.0 - lam_init))
    o_ref[...] = jnp.concatenate(outs, axis=1).astype(BF16)


def _diff(dq, dk, dv, lamv, sub_g2, layer, batch, seq):
    tq, tk = 256, 256
    nb = seq // tq
    lam_init = 0.8 - 0.6 * math.exp(-0.3 * layer)
    row = lambda w: pl.BlockSpec((tq, w), lambda b, i: (b * nb + i, 0))
    per_b = lambda w: pl.BlockSpec((seq, w), lambda b, i: (b, 0))
    full = lambda a: pl.BlockSpec(a.shape, lambda b, i: (0,) * a.ndim)
    return pl.pallas_call(
        functools.partial(_diff_kernel, lam_init=lam_init, tq=tq, tk=tk),
        grid=(batch, nb),
        in_specs=[row(256), per_b(256), per_b(256), full(lamv), full(sub_g2)],
        out_specs=row(256),
        out_shape=jax.ShapeDtypeStruct((batch * seq, 256), BF16),
        scratch_shapes=[pltpu.VMEM((2, 4 * tq, LANES), F32)],
        compiler_params=_params("parallel", "arbitrary"),
        name="diff_attn",
    )(dq, dk, dv, lamv, sub_g2)


def _mla_kernel(q_ref, k_ref, v_ref, o_ref, acc_ref, *, tq, tk):
    i = pl.program_id(1)
    qp = _qpos(i, tq)
    nh = MLA_HEADS
    acc_ref[...] = jnp.zeros_like(acc_ref)

    def step(j, st, diag):
        mask = (_kpos(j, tk) <= qp) if diag else None
        return tuple(_chain(q_ref[:, h * LANES:(h + 1) * LANES], _ktile(k_ref, j, tk, h * LANES, (h + 1) * LANES),
                            _ktile(v_ref, j, tk, (h // 2) * LANES, (h // 2 + 1) * LANES), mask, st[h], acc_ref, h)
                     for h in range(nh))

    nfull = (i * tq) // tk
    st = lax.fori_loop(0, nfull, lambda j, s: step(j, s, False), _init_state(nh, tq))
    st = step(nfull, st, True)
    o = [acc_ref[h] / st[h][1] for h in range(nh)]
    o_ref[...] = jnp.concatenate([_pair(o[0], o[1]), _pair(o[2], o[3])], axis=1).astype(BF16)


def _mla(mq, mk, mv, batch, seq):
    tq, tk = 256, 512
    nb = seq // tq
    row = lambda w: pl.BlockSpec((tq, w), lambda b, i: (b * nb + i, 0))
    per_b = lambda w: pl.BlockSpec((seq, w), lambda b, i: (b, 0))
    return pl.pallas_call(
        functools.partial(_mla_kernel, tq=tq, tk=tk),
        grid=(batch, nb),
        in_specs=[row(512), per_b(512), per_b(256)],
        out_specs=row(256),
        out_shape=jax.ShapeDtypeStruct((batch * seq, 256), BF16),
        scratch_shapes=[pltpu.VMEM((MLA_HEADS, tq, LANES), F32)],
        compiler_params=_params("parallel", "arbitrary"),
        name="mla_attn",
    )(mq, mk, mv)


def _swa_kernel(q_ref, k_ref, v_ref, sink_ref, o_ref, acc_ref, *, tq, tk):
    i = pl.program_id(1)
    qh = _half_heads(q_ref[...])
    qp = _qpos(i, tq)
    nh = SWA_HEADS
    acc_ref[...] = jnp.zeros_like(acc_ref)

    def step(j, st):
        dist = qp - _kpos(j, tk)
        mask = jnp.where(dist >= 0, dist, SWA_WINDOW) < SWA_WINDOW
        k = _ktile(k_ref, j, tk, 0, LANES)
        v = _ktile(v_ref, j, tk, 0, LANES)
        return tuple(_chain(qh[c], k, v, mask, st[c], acc_ref, c) for c in range(nh))

    lo = jnp.maximum(i * tq - SWA_WINDOW, 0) // tk
    st = lax.fori_loop(lo, (i * tq) // tk + 1, step, _init_state(nh, tq))
    sk = sink_ref[...]
    o = []
    for c, h in enumerate((0, 2, 1, 3)):
        m, l = st[c]
        sink = sk[:, h:h + 1]
        m2 = jnp.maximum(m, sink)
        sc = jnp.exp(m - m2)
        o.append(acc_ref[c] * sc / (l * sc + jnp.exp(sink - m2)))
    o_ref[...] = jnp.concatenate([_pair(o[0], o[1]), _pair(o[2], o[3])], axis=1).astype(BF16)


def _swa(sq, sk, sv, sinks, batch, seq):
    tq, tk = 256, 256
    nb = seq // tq
    row = lambda w: pl.BlockSpec((tq, w), lambda b, i: (b * nb + i, 0))
    per_b = lambda w: pl.BlockSpec((seq, w), lambda b, i: (b, 0))
    return pl.pallas_call(
        functools.partial(_swa_kernel, tq=tq, tk=tk),
        grid=(batch, nb),
        in_specs=[row(256), per_b(128), per_b(128), pl.BlockSpec(sinks.shape, lambda b, i: (0, 0))],
        out_specs=row(256),
        out_shape=jax.ShapeDtypeStruct((batch * seq, 256), BF16),
        scratch_shapes=[pltpu.VMEM((SWA_HEADS, tq, LANES), F32)],
        compiler_params=_params("parallel", "arbitrary"),
        name="swa_attn",
    )(sq, sk, sv, sinks)


def _outproj_kernel(x_ref, mod_ref, ng_ref, oa_ref, ob_ref, oc_ref, od_ref, w_ref, x1_ref, ht_ref):
    acc = _dot(oa_ref[...], w_ref[0:256, :])
    acc = acc + _dot(ob_ref[...], w_ref[256:512, :])
    acc = acc + _dot(oc_ref[...], w_ref[512:768, :])
    acc = acc + _dot(od_ref[...], w_ref[768:1024, :])
    x1 = x_ref[...] + mod_ref[0, 2:3, :] * acc
    x1_ref[...] = x1
    h = _rms(x1, ng_ref[...]) * (1.0 + mod_ref[0, 4:5, :]) + mod_ref[0, 3:4, :]
    ht_ref[...] = h.T.astype(BF16)


def _outproj(x2d, mod_l, norm_g, oa, ob, oc, od, w_out, seq):
    t, d = x2d.shape
    tm = 256
    tpb = seq // tm
    row = lambda w: pl.BlockSpec((tm, w), lambda i: (i, 0))
    full = lambda a: pl.BlockSpec(a.shape, lambda i: (0,) * a.ndim)
    return pl.pallas_call(
        _outproj_kernel,
        grid=(t // tm,),
        in_specs=[row(d), pl.BlockSpec((1, 6, d), lambda i: (i // tpb, 0, 0)), full(norm_g),
                  row(256), row(256), row(256), row(256), full(w_out)],
        out_specs=[row(d), pl.BlockSpec((d, tm), lambda i: (0, i))],
        out_shape=[jax.ShapeDtypeStruct((t, d), F32), jax.ShapeDtypeStruct((d, t), BF16)],
        compiler_params=_params("parallel"),
        name="outproj",
    )(x2d, mod_l, norm_g, oa, ob, oc, od, w_out)


_CAND_PIECES = [(0, 0, 8), (0, 8, 8), (1, 0, 8), (2, 0, 5), (3, 0, 4), (4, 0, 3), (5, 0, 2), (6, 0, 2), (7, 0, 2),
                (None, 0, 8)]


_CODE_UNIT = 2.0 ** 114
_TAKEN_BELOW = -(2.0 ** 119)
_INVALID = -(2.0 ** 100)


def _rank_code(r):
    return -(64.0 + r) * _CODE_UNIT


def _top16(s):
    tb = s.shape[1]
    row16 = lax.broadcasted_iota(jnp.int32, (PEER_TOPK, tb), 0)
    vals = jnp.zeros((PEER_TOPK, tb), F32)
    work = s
    for r in range(PEER_TOPK):
        m = jnp.max(work, axis=0, keepdims=True)
        work = jnp.where(work == m, _rank_code(r), work)
        vals = jnp.where(row16 == r, m, vals)
    return vals, work


def _router_head(h, ht, wq_ref, k1_ref, k2_ref, ea_ref, n1_ref, r2_ref, eb_ref):
    tb = ht.shape[1]
    nk = PEER_N_KEYS
    row8 = lax.broadcasted_iota(jnp.int32, (8, tb), 0)
    row16 = lax.broadcasted_iota(jnp.int32, (PEER_TOPK, tb), 0)
    o = pl.multiple_of(h * 2 * nk, 2 * nk)
    q1 = _dot(wq_ref[pl.ds(o, nk), :], ht).astype(BF16)
    q2 = _dot(wq_ref[pl.ds(o + nk, nk), :], ht).astype(BF16)
    s1 = _dot(k1_ref[...], q1)
    s2 = _dot(k2_ref[...], q2)
    v1, code1 = _top16(s1)
    v2, code2 = _top16(s2)
    top = v1[0:1] + v2[0:1]

    def cells(r1, c0):
        return v1[8:16] + v2[0:1] if r1 is None else v1[r1:r1 + 1] + v2[c0:c0 + 8]

    pieces = []
    for r1, c0, valid in _CAND_PIECES:
        p = cells(r1, c0)
        pieces.append(p if valid == 8 else jnp.where(row8 < valid, p, _INVALID))
    for _ in range(PEER_TOPK):
        m = pieces[0]
        for p in pieces[1:]:
            m = jnp.maximum(m, p)
        m = jnp.max(m, axis=0, keepdims=True)
        pieces = [jnp.where(p == m, _rank_code(0), p) for p in pieces]
    counts = jnp.zeros((PEER_TOPK, tb), F32)
    z = jnp.zeros((1, tb), F32)
    for p, (r1, c0, valid) in zip(pieces, _CAND_PIECES):
        taken = p < _TAKEN_BELOW
        if r1 is None:
            counts = counts + jnp.concatenate([jnp.zeros((8, tb), F32), jnp.where(taken, 1.0, 0.0)], axis=0)
        else:
            n = jnp.sum(jnp.where(taken, 1.0, 0.0), axis=0, keepdims=True)
            counts = counts + jnp.where(row16 == r1, n, 0.0)
        z = z + jnp.sum(jnp.where(taken, jnp.exp(cells(r1, c0) - top), 0.0), axis=0, keepdims=True)
    n1 = jnp.zeros((nk, tb), F32)
    for r in range(PEER_TOPK):
        n1 = jnp.where(code1 == _rank_code(r), counts[r:r + 1], n1)
    ea_ref[h] = jnp.exp(s1 - v1[0:1])
    n1_ref[h] = n1
    rank2 = jnp.where(code2 < _TAKEN_BELOW, code2 * (-1.0 / _CODE_UNIT) - 64.0, float(nk))
    r2_ref[h] = rank2.astype(BF16)
    eb_ref[h] = (jnp.exp(s2 - v2[0:1]) / z).astype(BF16)


def _router_kernel(ht_ref, wq_ref, k1_ref, k2_ref, ea_ref, n1_ref, r2_ref, eb_ref):
    ht = ht_ref[...]

    def pair(p, carry):
        for hh in range(2):
            _router_head(2 * p + hh, ht, wq_ref, k1_ref, k2_ref, ea_ref, n1_ref, r2_ref, eb_ref)
        return carry

    lax.fori_loop(0, PEER_HEADS // 2, pair, 0)


def _router(ht, wq_t, k1, k2):
    d, t = ht.shape
    tb = 128
    full = lambda a: pl.BlockSpec(a.shape, lambda i: (0,) * a.ndim)
    out = pl.BlockSpec((PEER_HEADS, PEER_N_KEYS, tb), lambda i: (0, 0, i))
    return pl.pallas_call(
        _router_kernel,
        grid=(t // tb,),
        in_specs=[pl.BlockSpec((d, tb), lambda i: (0, i)), full(wq_t), full(k1), full(k2)],
        out_specs=[out, out, out, out],
        out_shape=[jax.ShapeDtypeStruct((PEER_HEADS, PEER_N_KEYS, t), dt) for dt in (F32, F32, BF16, BF16)],
        compiler_params=_params("parallel"),
        name="peer_router",
    )(ht, wq_t, k1, k2)


def _gelu_tanh(x):
    k = 2.0 * math.sqrt(2.0 / math.pi) * math.log2(math.e)
    return x / (1.0 + jnp.exp2(x * (-k - (k * 0.044715) * (x * x))))


def _peer_kernel(ht_ref, u_ref, vt_ref, ea_ref, n1_ref, r2_ref, eb_ref, x_ref, mod_ref, fg_ref, o_ref, acc_ref, wa_ref,
                 wb_ref, *, final, chunk):
    e = pl.program_id(1)
    n_e = pl.num_programs(1) - 1
    nk = PEER_N_KEYS
    n_i1 = u_ref.shape[0] // nk
    sub = 16
    base = lax.rem(jnp.minimum(e, n_e - 1), 8 // n_i1) * n_i1

    @pl.when(e == 0)
    def _():
        acc_ref[...] = jnp.zeros_like(acc_ref)
        wb_ref[...] = jnp.zeros_like(wb_ref)

    def step(write_ref, read_ref):
        def tokens(c, carry):
            lanes = pl.ds(pl.multiple_of(c * chunk, chunk), chunk)
            act = _gelu_tanh(_dot(u_ref[...], ht_ref[:, lanes])).astype(BF16)
            acc_ref[:, lanes] += _dot(vt_ref[...], read_ref[:, lanes])
            for j in range(n_i1):
                m = None
                for h in range(PEER_HEADS):
                    n_row = jnp.broadcast_to(n1_ref[h, pl.ds(base + j, 1), lanes], (sub, chunk)).astype(BF16)
                    ea_row = jnp.broadcast_to(ea_ref[h, pl.ds(base + j, 1), lanes], (sub, chunk)).astype(BF16)
                    r2 = r2_ref[h, :, lanes].reshape(nk // sub, sub, chunk)
                    eb = eb_ref[h, :, lanes].reshape(nk // sub, sub, chunk)
                    term = jnp.where(r2 < n_row[None], eb, jnp.zeros((), BF16)) * ea_row[None]
                    m = term if m is None else m + term
                write_ref[j * nk:(j + 1) * nk, lanes] = m.reshape(nk, chunk) * act[j * nk:(j + 1) * nk]
            return carry

        lax.fori_loop(0, ht_ref.shape[1] // chunk, tokens, 0)

    @pl.when(lax.rem(e, 2) == 0)
    def _():
        step(wa_ref, wb_ref)

    @pl.when(lax.rem(e, 2) == 1)
    def _():
        step(wb_ref, wa_ref)

    @pl.when(e == n_e)
    def _():
        y = x_ref[...] + mod_ref[0, 5:6, :] * acc_ref[...].T
        if final:
            y = _rms(y, fg_ref[...])
        o_ref[...] = y


def _peer(ht, u_bf, vt_bf, ea, n1, r2, eb, x1, mod_l, final_g, seq, final):
    d, t = ht.shape
    n_exp = u_bf.shape[0]
    tb = 1024 if seq % 1024 == 0 else 512
    eb_blk = 512
    n_e = n_exp // eb_blk
    n_i1 = eb_blk // PEER_N_KEYS
    tpb = seq // tb
    cur = lambda e: jnp.minimum(e, n_e - 1)
    i1_spec = pl.BlockSpec((PEER_HEADS, 8, tb), lambda i, e: (0, cur(e) // (8 // n_i1), i))
    tok3 = pl.BlockSpec((PEER_HEADS, PEER_N_KEYS, tb), lambda i, e: (0, 0, i))
    return pl.pallas_call(
        functools.partial(_peer_kernel, final=final, chunk=512),
        grid=(t // tb, n_e + 1),
        in_specs=[pl.BlockSpec((d, tb), lambda i, e: (0, i)),
                  pl.BlockSpec((eb_blk, d), lambda i, e: (cur(e), 0)),
                  pl.BlockSpec((d, eb_blk), lambda i, e: (0, jnp.maximum(e - 1, 0))),
                  i1_spec, i1_spec, tok3, tok3,
                  pl.BlockSpec((tb, d), lambda i, e: (i, 0)),
                  pl.BlockSpec((1, 6, d), lambda i, e: (i // tpb, 0, 0)),
                  pl.BlockSpec(final_g.shape, lambda i, e: (0, 0))],
        out_specs=pl.BlockSpec((tb, d), lambda i, e: (i, 0)),
        out_shape=jax.ShapeDtypeStruct((t, d), F32),
        scratch_shapes=[pltpu.VMEM((d, tb), F32), pltpu.VMEM((eb_blk, tb), BF16), pltpu.VMEM((eb_blk, tb), BF16)],
        compiler_params=_params("parallel", "arbitrary"),
        name="peer_experts",
    )(ht, u_bf, vt_bf, ea, n1, r2, eb, x1, mod_l, final_g)


def kernel(x, c, ada_w, ada_b, norm_mix_g, norm_ffn_g, w_in, nsa_cmp_pos_k, nsa_cmp_pos_v, nsa_cmp_wk, nsa_cmp_wv, diff_lam_q1, diff_lam_k1, diff_lam_q2, diff_lam_k2, diff_sub_g, mla_q_norm_g, mla_w_uq, mla_kv_norm_g, mla_w_ukv, swa_sinks, w_out, peer_w_q, peer_sub_k1, peer_sub_k2, peer_u, peer_v, final_g):
    batch, seq, d = x.shape
    depth = w_in.shape[0]
    assert seq % 512 == 0
    x2d = x.reshape(batch * seq, d)
    mod = _adaln(c, ada_w, ada_b).reshape(depth, batch, 6, d)
    table = _rope_table(seq)
    mixw = 4 * HEAD_DIM
    swa_rows = 3 * mixw + np.concatenate([h * HEAD_DIM + np.arange(HEAD_DIM) for h in (0, 2, 1, 3)])
    out_rows = jnp.asarray(np.concatenate([np.arange(3 * mixw), swa_rows]), jnp.int32)
    pad128 = lambda v: jnp.pad(v, (0, LANES - v.shape[0])).reshape(1, LANES)
    fg = final_g.reshape(1, d)
    for l in range(depth):
        w_big = _take_cols(w_in[l], _IN_IDX, _IN_SGN)
        wuq = _take_cols(mla_w_uq[l], _UQ_IDX, _UQ_SGN)
        wukv = _take_cols(mla_w_ukv[l], _UKV_IDX, _UKV_SGN)
        (nq, nk, nv, ng, dq, dk, dv, mq, mk, mv, sq, sk, sv) = _inproj(
            x2d, mod[l], norm_mix_g[l].reshape(1, d), w_big, table, wuq, wukv,
            mla_q_norm_g[l].reshape(1, -1), mla_kv_norm_g[l].reshape(1, -1), seq)
        ocmp, sel = _nsa_cmp(nq, nk, nv, nsa_cmp_wk[l], nsa_cmp_wv[l], nsa_cmp_pos_k[l], nsa_cmp_pos_v[l], batch, seq)
        o_a = _nsa(nq, nk, nv, sel, ng, ocmp, batch, seq)
        lamv = jnp.concatenate([pad128(diff_lam_q1[l]), pad128(diff_lam_k1[l]),
                                pad128(diff_lam_q2[l]), pad128(diff_lam_k2[l])], axis=0)
        sub_g2 = jnp.concatenate([diff_sub_g[l], diff_sub_g[l]]).reshape(1, LANES)
        o_b = _diff(dq, dk, dv, lamv, sub_g2, l, batch, seq)
        o_c = _mla(mq, mk, mv, batch, seq)
        o_d = _swa(sq, sk, sv, pad128(swa_sinks[l]), batch, seq)
        w_o = jnp.take(w_out[l], out_rows, axis=0).astype(BF16)
        x1, ht = _outproj(x2d, mod[l], norm_ffn_g[l].reshape(1, d), o_a, o_b, o_c, o_d, w_o, seq)
        ea, n1, r2, eb = _router(ht, peer_w_q[l].T.astype(BF16), peer_sub_k1[l].astype(BF16),
                                 peer_sub_k2[l].astype(BF16))
        x2d = _peer(ht, peer_u[l].astype(BF16), peer_v[l].T.astype(BF16), ea, n1, r2, eb, x1, mod[l], fg,
                    seq, final=(l == depth - 1))
    return x2d.reshape(batch, seq, d)
```

```python
import functools
import math

import numpy as np
import jax
import jax.numpy as jnp
from jax import lax
from jax.experimental import pallas as pl
from jax.experimental.pallas import tpu as pltpu

F32 = jnp.float32
BF16 = jnp.bfloat16

HEAD_DIM = 64
ROPE_THETA = 10000.0
EPS = 1e-6
NEG = -1e30
FORCE = 1e4

NSA_HEADS = 4
NSA_CMP_LEN = 32
NSA_CMP_STRIDE = 16
NSA_SEL_LEN = 64
NSA_TOP_N = 16
NSA_WINDOW = 512

DIFF_HEADS = 4
DIFF_QK_DIM = 32
DIFF_V_DIM = 64

MLA_HEADS = 4
MLA_Q_RANK = 256
MLA_KV_RANK = 128
MLA_NOPE_DIM = 64
MLA_ROPE_DIM = 32
MLA_V_DIM = 64

SWA_HEADS = 4
SWA_KV_HEADS = 2
SWA_WINDOW = 128

PEER_HEADS = 8
PEER_N_KEYS = 128
PEER_TOPK = 16
PEER_QUERY_DIM = 256

LOG2E = math.log2(math.e)
LANES = 128
VMEM_LIMIT = 56 * 1024 * 1024


def _dot(a, b):
    return jnp.dot(a, b, preferred_element_type=F32)


def _dot_nt(a, b):
    return lax.dot_general(a, b, (((1,), (1,)), ((), ())), preferred_element_type=F32)


def _params(*sem):
    return pltpu.CompilerParams(dimension_semantics=sem, vmem_limit_bytes=VMEM_LIMIT)


def _rms(x, g):
    return x * lax.rsqrt(jnp.mean(x * x, axis=-1, keepdims=True) + EPS) * g


def _rot_idx(base, dim):
    half = dim // 2
    idx = np.concatenate([base + half + np.arange(half), base + np.arange(half)])
    sgn = np.concatenate([-np.ones(half), np.ones(half)])
    return idx, sgn


def _in_plan():
    d = HEAD_DIM
    nsa0 = 0
    nsa_cols = NSA_HEADS * d + 6 * d + 3 * NSA_HEADS
    diff0 = nsa0 + nsa_cols
    diff_cols = 2 * DIFF_HEADS * 2 * DIFF_QK_DIM + DIFF_HEADS * DIFF_V_DIM
    mla0 = diff0 + diff_cols
    mla_cols = MLA_Q_RANK + MLA_KV_RANK + MLA_ROPE_DIM
    swa0 = mla0 + mla_cols
    idx, sgn, off = [], [], {}

    def add(name, i, s=None):
        i = np.asarray(i, np.int64)
        s = np.ones(len(i)) if s is None else np.asarray(s, np.float64)
        pad = (-len(i)) % LANES
        off[name] = sum(len(a) for a in idx)
        idx.append(np.concatenate([i, np.zeros(pad, np.int64)]))
        sgn.append(np.concatenate([s, np.zeros(pad)]))

    def heads_rot(base, nheads, dim):
        ii, ss = zip(*[_rot_idx(base + h * dim, dim) for h in range(nheads)])
        return np.concatenate(ii), np.concatenate(ss)

    nq = nsa0 + np.arange(NSA_HEADS * d)
    add("nq", nq)
    add("nqr", *heads_rot(nsa0, NSA_HEADS, d))
    kb = nsa0 + NSA_HEADS * d
    kc, vc, ksl, vsl, kw, vw = [kb + j * d for j in range(6)]
    dup = lambda b: np.concatenate([b + np.arange(d), b + np.arange(d)])
    add("nk", np.concatenate([dup(kc), dup(ksl), dup(kw)]))
    kr = [_rot_idx(b, d) for b in (kc, kc, ksl, ksl, kw, kw)]
    add("nkr", np.concatenate([a for a, _ in kr]), np.concatenate([b for _, b in kr]))
    add("nv", np.concatenate([dup(vc), dup(vsl), dup(vw)]))
    add("ng", kb + 6 * d + np.arange(3 * NSA_HEADS))
    nqk = DIFF_HEADS * 2 * DIFF_QK_DIM
    add("dq", diff0 + np.arange(nqk))
    add("dqr", *heads_rot(diff0, 2 * DIFF_HEADS, DIFF_QK_DIM))
    add("dk", diff0 + nqk + np.arange(nqk))
    add("dkr", *heads_rot(diff0 + nqk, 2 * DIFF_HEADS, DIFF_QK_DIM))
    add("dv", diff0 + 2 * nqk + np.arange(DIFF_HEADS * DIFF_V_DIM))
    add("mcq", mla0 + np.arange(MLA_Q_RANK))
    add("mckv", mla0 + MLA_Q_RANK + np.arange(MLA_KV_RANK))
    kr0 = mla0 + MLA_Q_RANK + MLA_KV_RANK
    z64 = np.zeros(MLA_NOPE_DIM, np.int64)
    add("mkr", np.concatenate([z64, kr0 + np.arange(MLA_ROPE_DIM)]),
        np.concatenate([np.zeros(MLA_NOPE_DIM), np.ones(MLA_ROPE_DIM)]))
    ri, rs = _rot_idx(kr0, MLA_ROPE_DIM)
    add("mkrr", np.concatenate([z64, ri]), np.concatenate([np.zeros(MLA_NOPE_DIM), rs]))
    order = [0, 2, 1, 3]
    add("sq", np.concatenate([swa0 + h * d + np.arange(d) for h in order]))
    sr = [_rot_idx(swa0 + h * d, d) for h in order]
    add("sqr", np.concatenate([a for a, _ in sr]), np.concatenate([b for _, b in sr]))
    sk0 = swa0 + SWA_HEADS * d
    add("sk", sk0 + np.arange(SWA_KV_HEADS * d))
    add("skr", *heads_rot(sk0, SWA_KV_HEADS, d))
    add("sv", sk0 + SWA_KV_HEADS * d + np.arange(SWA_KV_HEADS * d))
    return np.concatenate(idx), np.concatenate(sgn), off


_IN_IDX, _IN_SGN, _OFF = _in_plan()
_NCOLS = len(_IN_IDX)


def _mla_plans():
    qd = MLA_NOPE_DIM + MLA_ROPE_DIM
    qi, qs, ri, rs = [], [], [], []
    for h in range(MLA_HEADS):
        b = h * qd
        qi += [b + np.arange(qd), np.zeros(LANES - qd, np.int64)]
        qs += [np.ones(qd), np.zeros(LANES - qd)]
        a, s = _rot_idx(b + MLA_NOPE_DIM, MLA_ROPE_DIM)
        ri += [np.zeros(MLA_NOPE_DIM, np.int64), a, np.zeros(LANES - qd, np.int64)]
        rs += [np.zeros(MLA_NOPE_DIM), s, np.zeros(LANES - qd)]
    kd = MLA_NOPE_DIM + MLA_V_DIM
    ki, ks, vi = [], [], []
    for h in range(MLA_HEADS):
        ki += [h * kd + np.arange(MLA_NOPE_DIM), np.zeros(LANES - MLA_NOPE_DIM, np.int64)]
        ks += [np.ones(MLA_NOPE_DIM), np.zeros(LANES - MLA_NOPE_DIM)]
        vi += [h * kd + MLA_NOPE_DIM + np.arange(MLA_V_DIM)]
    uq_idx = np.concatenate(qi + ri)
    uq_sgn = np.concatenate(qs + rs)
    ukv_idx = np.concatenate(ki + vi)
    ukv_sgn = np.concatenate(ks + [np.ones(MLA_HEADS * MLA_V_DIM)])
    return uq_idx, uq_sgn, ukv_idx, ukv_sgn


_UQ_IDX, _UQ_SGN, _UKV_IDX, _UKV_SGN = _mla_plans()


def _take_cols(w, idx, sgn):
    return (jnp.take(w, jnp.asarray(idx, jnp.int32), axis=1) * jnp.asarray(sgn, F32)[None, :]).astype(BF16)


def _rope_table(seq):
    def cs(dim):
        inv = 1.0 / (ROPE_THETA ** (jnp.arange(0, dim, 2, dtype=F32) / dim))
        ang = jnp.arange(seq, dtype=F32)[:, None] * inv[None, :]
        c, s = jnp.cos(ang), jnp.sin(ang)
        return jnp.concatenate([c, c], 1), jnp.concatenate([s, s], 1)
    ch, sh = cs(HEAD_DIM)
    cd, sd = cs(DIFF_QK_DIM)
    cm, sm = cs(MLA_ROPE_DIM)
    one = jnp.ones((seq, MLA_NOPE_DIM), F32)
    z64 = jnp.zeros((seq, MLA_NOPE_DIM), F32)
    z32 = jnp.zeros((seq, LANES - MLA_NOPE_DIM - MLA_ROPE_DIM), F32)
    parts = [jnp.tile(ch, (1, 2)), jnp.tile(sh, (1, 2)), jnp.tile(cd, (1, 4)), jnp.tile(sd, (1, 4)),
             jnp.concatenate([one, cm, z32], 1), jnp.concatenate([z64, sm, z32], 1),
             jnp.concatenate([z64, cm, z32], 1), jnp.concatenate([z64, sm, z32], 1)]
    return jnp.concatenate(parts, 1)


def _adaln_kernel(c_ref, w_ref, b_ref, o_ref):
    c = c_ref[...]
    sc = (c * jax.nn.sigmoid(c)).astype(BF16)
    o_ref[0] = _dot(sc, w_ref[0].astype(BF16)) + b_ref[0]


def _adaln(c, ada_w, ada_b):
    nl, d, n6 = ada_w.shape
    b = c.shape[0]
    tn = 1536
    return pl.pallas_call(
        _adaln_kernel,
        grid=(nl, n6 // tn),
        in_specs=[pl.BlockSpec((b, d), lambda l, j: (0, 0)),
                  pl.BlockSpec((1, d, tn), lambda l, j: (l, 0, j)),
                  pl.BlockSpec((1, 1, tn), lambda l, j: (l, 0, j))],
        out_specs=pl.BlockSpec((1, b, tn), lambda l, j: (l, 0, j)),
        out_shape=jax.ShapeDtypeStruct((nl, b, n6), F32),
        compiler_params=_params("parallel", "parallel"),
        name="adaln",
    )(c, ada_w, ada_b.reshape(nl, 1, n6))


def _inproj_kernel(x_ref, mod_ref, ng_ref, w_ref, tab_ref, wuq_ref, wukv_ref, gq_ref, gkv_ref,
                   nq_ref, nk_ref, nv_ref, ngo_ref, dq_ref, dk_ref, dv_ref,
                   mq_ref, mk_ref, mv_ref, sq_ref, sk_ref, sv_ref):
    x = x_ref[...]
    h = _rms(x, ng_ref[...]) * (1.0 + mod_ref[0, 1:2, :]) + mod_ref[0, 0:1, :]
    hb = h.astype(BF16)

    def mm(name, width):
        o = _OFF[name]
        return _dot(hb, w_ref[:, o:o + width])

    def tab(j, reps):
        t = tab_ref[:, j * LANES:(j + 1) * LANES]
        return t if reps == 1 else jnp.concatenate([t] * reps, axis=1)

    def rope(name, rname, width, cj, scale=1.0):
        r = mm(name, width) * tab(cj, width // LANES) + mm(rname, width) * tab(cj + 1, width // LANES)
        return r if scale == 1.0 else r * scale

    d = HEAD_DIM
    nq_ref[...] = rope("nq", "nqr", 256, 0, LOG2E * d ** -0.5).astype(BF16)
    nk_ref[...] = rope("nk", "nkr", 384, 0).astype(BF16)
    nv_ref[...] = mm("nv", 384).astype(BF16)
    ngo_ref[...] = jax.nn.sigmoid(mm("ng", LANES))
    dq_ref[...] = rope("dq", "dqr", 256, 2, LOG2E * DIFF_QK_DIM ** -0.5).astype(BF16)
    dk_ref[...] = rope("dk", "dkr", 256, 2).astype(BF16)
    dv_ref[...] = mm("dv", 256).astype(BF16)
    cq = _rms(mm("mcq", MLA_Q_RANK), gq_ref[...]).astype(BF16)
    nh = MLA_HEADS * LANES
    qa = _dot(cq, wuq_ref[:, 0:nh])
    qb = _dot(cq, wuq_ref[:, nh:2 * nh])
    mq = (qa * tab(4, MLA_HEADS) + qb * tab(5, MLA_HEADS)) * (LOG2E * (MLA_NOPE_DIM + MLA_ROPE_DIM) ** -0.5)
    mq_ref[...] = mq.astype(BF16)
    ckv = _rms(mm("mckv", MLA_KV_RANK), gkv_ref[...]).astype(BF16)
    kk = _dot(ckv, wukv_ref[:, 0:nh])
    kr = mm("mkr", LANES) * tab(6, 1) + mm("mkrr", LANES) * tab(7, 1)
    mk_ref[...] = (kk + jnp.concatenate([kr] * MLA_HEADS, axis=1)).astype(BF16)
    mv_ref[...] = _dot(ckv, wukv_ref[:, nh:nh + MLA_HEADS * MLA_V_DIM]).astype(BF16)
    sq_ref[...] = rope("sq", "sqr", 256, 0, LOG2E * d ** -0.5).astype(BF16)
    sk_ref[...] = rope("sk", "skr", 128, 0).astype(BF16)
    sv_ref[...] = mm("sv", 128).astype(BF16)


def _inproj(x2d, mod_l, norm_g, w_big, table, wuq, wukv, gq, gkv, seq):
    t, d = x2d.shape
    tm = 256
    tpb = seq // tm
    widths = [256, 384, 384, 128, 256, 256, 256, 512, 512, 256, 256, 128, 128]
    dts = [BF16, BF16, BF16, F32, BF16, BF16, BF16, BF16, BF16, BF16, BF16, BF16, BF16]
    full = lambda a: pl.BlockSpec(a.shape, lambda i: (0,) * a.ndim)
    return pl.pallas_call(
        _inproj_kernel,
        grid=(t // tm,),
        in_specs=[pl.BlockSpec((tm, d), lambda i: (i, 0)),
                  pl.BlockSpec((1, 6, d), lambda i: (i // tpb, 0, 0)),
                  full(norm_g), full(w_big),
                  pl.BlockSpec((tm, table.shape[1]), lambda i: (i % tpb, 0)),
                  full(wuq), full(wukv), full(gq), full(gkv)],
        out_specs=[pl.BlockSpec((tm, w), lambda i: (i, 0)) for w in widths],
        out_shape=[jax.ShapeDtypeStruct((t, w), dt) for w, dt in zip(widths, dts)],
        compiler_params=_params("parallel"),
        name="inproj",
    )(x2d, mod_l, norm_g, w_big, table, wuq, wukv, gq, gkv)


def _lane_mask(lo, hi):
    lane = lax.broadcasted_iota(jnp.int32, (1, LANES), 1)
    return (lane >= lo) & (lane < hi)


def _masked(q, lo, hi):
    return jnp.where(_lane_mask(lo, hi), q, jnp.zeros_like(q))


def _chain(q, k, v, mask, state, acc_ref, c):
    m, l = state
    s = _dot_nt(q, k)
    if mask is not None:
        s = jnp.where(mask, s, NEG)
    m2 = jnp.maximum(m, jnp.max(s, axis=-1, keepdims=True))
    a = jnp.exp2(m - m2)
    p = jnp.exp2(s - m2)
    acc_ref[c] = a * acc_ref[c] + _dot(p.astype(BF16), v)
    return m2, a * l + jnp.sum(p, axis=-1, keepdims=True)


def _init_state(n, rows):
    return tuple((jnp.full((rows, 1), NEG, F32), jnp.zeros((rows, 1), F32)) for _ in range(n))


def _ktile(ref, j, tk, c0, c1):
    return ref[pl.ds(pl.multiple_of(j * tk, tk), tk), c0:c1]


def _qpos(i, tq, reps=1):
    p = i * tq + lax.broadcasted_iota(jnp.int32, (tq, 1), 0)
    return p if reps == 1 else jnp.concatenate([p] * reps, axis=0)


def _kpos(j, tk):
    return j * tk + lax.broadcasted_iota(jnp.int32, (1, tk), 1)


def _half_heads(q):
    return [_masked(q[:, c * LANES:(c + 1) * LANES], 64 * hh, 64 * hh + 64) for c in range(2) for hh in range(2)]


def _pair(lo_val, hi_val):
    return jnp.where(_lane_mask(0, 64), lo_val, hi_val)


def _nsa_cmp_kernel(q_ref, kc_ref, vc_ref, wk_ref, wv_ref, pk_ref, pv_ref, ov_ref, oc_ref, sel_ref, *, top_n, n_sel):
    half = wk_ref.shape[1]

    def compress(x_ref, w_ref, p_ref):
        x = x_ref[...]
        a = _dot(x, w_ref[0])
        b = _dot(x, w_ref[1])
        p = jnp.broadcast_to(p_ref[...], (8, 2 * half)).astype(BF16)
        const = (_dot(p[:, 0:half], w_ref[0]) + _dot(p[:, half:2 * half], w_ref[1]))[0:1]
        return a + jnp.concatenate([b[1:], b[:1]], axis=0) + const

    kcmp = compress(kc_ref, wk_ref, pk_ref).astype(BF16)
    vcmp = compress(vc_ref, wv_ref, pv_ref).astype(BF16)
    ncp = kcmp.shape[0]
    ov = ov_ref[...]
    rb = 256
    cend = NSA_CMP_STRIDE * lax.broadcasted_iota(jnp.int32, (1, ncp), 1) + (NSA_CMP_LEN - 1)
    lane = lax.broadcasted_iota(jnp.int32, (1, LANES), 1)

    def block(r, carry):
        r0 = pl.multiple_of(r * rb, rb)
        q = q_ref[pl.ds(r0, rb), :]
        tpos = r0 + lax.broadcasted_iota(jnp.int32, (rb, 1), 0)
        vis = cend <= tpos
        psum = jnp.zeros((rb, ncp), F32)
        outs = []
        for c in range(2):
            halves = []
            for hh in range(2):
                qm = _masked(q[:, c * LANES:(c + 1) * LANES], 64 * hh, 64 * hh + 64)
                s = jnp.where(vis, _dot_nt(qm, kcmp), NEG)
                e = jnp.exp2(s - jnp.max(s, axis=-1, keepdims=True))
                p = jnp.where(vis, e / jnp.sum(e, axis=-1, keepdims=True), 0.0)
                psum = psum + p
                halves.append(_dot(p.astype(BF16), vcmp))
            outs.append(_pair(halves[0], halves[1]))
        oc_ref[pl.ds(r0, rb), :] = jnp.concatenate(outs, axis=1)
        hi = psum.astype(BF16)
        lo = (psum - hi.astype(F32)).astype(BF16)
        imp = _dot(hi, ov) + _dot(lo, ov)
        nsp = -(-n_sel // 8) * 8
        imp_t = imp.T[0:nsp]
        blk = lax.broadcasted_iota(jnp.int32, (nsp, 1), 0)
        qblk = (r0 + lax.broadcasted_iota(jnp.int32, (1, rb), 1)) // NSA_SEL_LEN
        allowed = blk <= qblk
        forced = (blk == 0) | (blk == qblk) | (blk == qblk - 1)
        impf = jnp.where(allowed, jnp.where(forced, FORCE, imp_t), NEG)
        rank = jnp.zeros((nsp, rb), F32)
        for j in range(n_sel):
            row = impf[j:j + 1, :]
            rank = rank + jnp.where(blk > j, jnp.where(row >= impf, 1.0, 0.0), jnp.where(row > impf, 1.0, 0.0))
        sel_t = jnp.where(allowed & (rank < top_n), 1.0, 0.0)
        if nsp < LANES:
            sel_t = jnp.concatenate([sel_t, jnp.zeros((LANES - nsp, rb), F32)], axis=0)
        sel_ref[pl.ds(r0, rb), :] = sel_t.T.astype(BF16)
        return carry

    lax.fori_loop(0, q_ref.shape[0] // rb, block, 0)


def _nsa_cmp(nq, nk, nv, wk, wv, pos_k, pos_v, batch, seq):
    d = HEAD_DIM
    nc = seq // NSA_CMP_STRIDE
    ncp = -(-nc // LANES) * LANES
    n_sel = seq // NSA_SEL_LEN
    assert n_sel <= LANES
    top_n = min(NSA_TOP_N, n_sel)

    def seg(a):
        a = a[:, :d].reshape(batch, nc, NSA_CMP_STRIDE * d)
        return jnp.pad(a, ((0, 0), (0, ncp - nc), (0, 0))).reshape(batch * ncp, NSA_CMP_STRIDE * d)

    half = NSA_CMP_STRIDE * d
    dupw = lambda w: jnp.concatenate([w, w], axis=1).reshape(2, half, 2 * d).astype(BF16)
    cpos = NSA_CMP_STRIDE * np.arange(ncp)[:, None] + np.arange(NSA_CMP_LEN)[None, :]
    ovl = np.zeros((ncp, LANES), np.float32)
    for j in range(n_sel):
        ovl[:, j] = (cpos // NSA_SEL_LEN == j).mean(axis=1)
    ovl[nc - 1:, :] = 0.0
    full = lambda a: pl.BlockSpec(a.shape, lambda b: (0,) * a.ndim)
    wk2, wv2 = dupw(wk), dupw(wv)
    pk, pv = pos_k.reshape(1, -1), pos_v.reshape(1, -1)
    ov = jnp.asarray(ovl, BF16)
    return pl.pallas_call(
        functools.partial(_nsa_cmp_kernel, top_n=top_n, n_sel=n_sel),
        grid=(batch,),
        in_specs=[pl.BlockSpec((seq, 256), lambda b: (b, 0)),
                  pl.BlockSpec((ncp, half), lambda b: (b, 0)),
                  pl.BlockSpec((ncp, half), lambda b: (b, 0)),
                  full(wk2), full(wv2), full(pk), full(pv), full(ov)],
        out_specs=[pl.BlockSpec((seq, 256), lambda b: (b, 0)),
                   pl.BlockSpec((seq, LANES), lambda b: (b, 0))],
        out_shape=[jax.ShapeDtypeStruct((batch * seq, 256), F32),
                   jax.ShapeDtypeStruct((batch * seq, LANES), BF16)],
        compiler_params=_params("parallel"),
        name="nsa_cmp",
    )(nq, seg(nk), seg(nv), wk2, wv2, pk, pv, ov)


def _nsa_kernel(q_ref, k_ref, v_ref, sel_ref, g_ref, oc_ref, e_ref, o_ref, acc_ref, *, tq, tk):
    i = pl.program_id(1)
    qh = _half_heads(q_ref[...])
    qp = _qpos(i, tq)
    sel = sel_ref[...]
    nh = NSA_HEADS
    acc_ref[...] = jnp.zeros_like(acc_ref)

    def sel_step(j, st, diag):
        mv = _dot(sel, e_ref[:, pl.ds(pl.multiple_of(j * tk, tk), tk)])
        if diag:
            mv = jnp.where(_kpos(j, tk) <= qp, mv, 0.0)
        mask = mv > 0.5
        k = _ktile(k_ref, j, tk, 128, 256)
        v = _ktile(v_ref, j, tk, 128, 256)
        return tuple(_chain(qh[h], k, v, mask, st[h], acc_ref, h) for h in range(nh))

    nfull = (i * tq) // tk
    st = lax.fori_loop(0, nfull, lambda j, s: sel_step(j, s, False), _init_state(nh, tq))
    st_sel = sel_step(nfull, st, True)

    def win_step(j, st):
        dist = qp - _kpos(j, tk)
        mask = jnp.where(dist >= 0, dist, NSA_WINDOW) < NSA_WINDOW
        k = _ktile(k_ref, j, tk, 256, 384)
        v = _ktile(v_ref, j, tk, 256, 384)
        return tuple(_chain(qh[h], k, v, mask, st[h], acc_ref, nh + h) for h in range(nh))

    wlo = jnp.maximum(i * tq - NSA_WINDOW, 0) // tk
    st_win = lax.fori_loop(wlo, nfull + 1, win_step, _init_state(nh, tq))
    g = g_ref[...]
    oc = oc_ref[...]
    outs = []
    for c in range(2):
        occ = oc[:, c * LANES:(c + 1) * LANES]

        def comb(h):
            o_sel = acc_ref[h] / st_sel[h][1]
            o_win = acc_ref[nh + h] / st_win[h][1]
            return g[:, 3 * h:3 * h + 1] * occ + g[:, 3 * h + 1:3 * h + 2] * o_sel + g[:, 3 * h + 2:3 * h + 3] * o_win

        outs.append(_pair(comb(2 * c), comb(2 * c + 1)))
    o_ref[...] = jnp.concatenate(outs, axis=1).astype(BF16)


def _nsa(nq, nk, nv, sel, gates, ocmp, batch, seq):
    tq, tk = 256, 256
    nb = seq // tq
    expand = np.zeros((LANES, seq), np.float32)
    for j in range(seq // NSA_SEL_LEN):
        expand[j, j * NSA_SEL_LEN:(j + 1) * NSA_SEL_LEN] = 1.0
    e = jnp.asarray(expand, BF16)
    row = lambda w: pl.BlockSpec((tq, w), lambda b, i: (b * nb + i, 0))
    per_b = lambda w: pl.BlockSpec((seq, w), lambda b, i: (b, 0))
    return pl.pallas_call(
        functools.partial(_nsa_kernel, tq=tq, tk=tk),
        grid=(batch, nb),
        in_specs=[row(256), per_b(384), per_b(384), row(LANES), row(LANES), row(256),
                  pl.BlockSpec(e.shape, lambda b, i: (0, 0))],
        out_specs=row(256),
        out_shape=jax.ShapeDtypeStruct((batch * seq, 256), BF16),
        scratch_shapes=[pltpu.VMEM((2 * NSA_HEADS, tq, LANES), F32)],
        compiler_params=_params("parallel", "arbitrary"),
        name="nsa_attn",
    )(nq, nk, nv, sel, gates, ocmp, e)


def _diff_kernel(q_ref, k_ref, v_ref, lam_ref, sg_ref, o_ref, acc_ref, *, lam_init, tq, tk):
    i = pl.program_id(1)
    q = q_ref[...]
    qp4 = _qpos(i, tq, 4)
    lv = lam_ref[...]
    lam = (jnp.exp(jnp.sum(lv[0:1] * lv[1:2], axis=-1, keepdims=True))
           - jnp.exp(jnp.sum(lv[2:3] * lv[3:4], axis=-1, keepdims=True)) + lam_init)
    qs = [jnp.concatenate([_masked(q[:, c * LANES:(c + 1) * LANES], 32 * t, 32 * t + 32) for t in range(4)], axis=0)
          for c in range(2)]
    acc_ref[...] = jnp.zeros_like(acc_ref)

    def step(j, st, diag):
        mask = (_kpos(j, tk) <= qp4) if diag else None
        return tuple(_chain(qs[c], _ktile(k_ref, j, tk, c * LANES, (c + 1) * LANES),
                            _ktile(v_ref, j, tk, c * LANES, (c + 1) * LANES), mask, st[c], acc_ref, c)
                     for c in range(2))

    nfull = (i * tq) // tk
    st = lax.fori_loop(0, nfull, lambda j, s: step(j, s, False), _init_state(2, 4 * tq))
    st = step(nfull, st, True)
    outs = []
    for c in range(2):
        o = acc_ref[c] / st[c][1]
        r = [o[t * tq:(t + 1) * tq] for t in range(4)]
        dd = _pair(r[0] - lam * r[1], r[2] - lam * r[3])
        sq = dd * dd
        lo = _lane_mask(0, 64)
        ms = _pair(jnp.sum(jnp.where(lo, sq, 0.0), axis=-1, keepdims=True),
                   jnp.sum(jnp.where(lo, 0.0, sq), axis=-1, keepdims=True)) * (1.0 / DIFF_V_DIM)
        outs.append(dd * lax.rsqrt(ms + EPS) * sg_ref[...] * (1.0 - lam_init))
    o_ref[...] = jnp.concatenate(outs, axis=1).astype(BF16)


def _diff(dq, dk, dv, lamv, sub_g2, layer, batch, seq):
    tq, tk = 256, 512
    nb = seq // tq
    lam_init = 0.8 - 0.6 * math.exp(-0.3 * layer)
    row = lambda w: pl.BlockSpec((tq, w), lambda b, i: (b * nb + i, 0))
    per_b = lambda w: pl.BlockSpec((seq, w), lambda b, i: (b, 0))
    full = lambda a: pl.BlockSpec(a.shape, lambda b, i: (0,) * a.ndim)
    return pl.pallas_call(
        functools.partial(_diff_kernel, lam_init=lam_init, tq=tq, tk=tk),
        grid=(batch, nb),
        in_specs=[row(256), per_b(256), per_b(256), full(lamv), full(sub_g2)],
        out_specs=row(256),
        out_shape=jax.ShapeDtypeStruct((batch * seq, 256), BF16),
        scratch_shapes=[pltpu.VMEM((2, 4 * tq, LANES), F32)],
        compiler_params=_params("parallel", "arbitrary"),
        name="diff_attn",
    )(dq, dk, dv, lamv, sub_g2)


def _mla_kernel(q_ref, k_ref, v_ref, o_ref, acc_ref, *, tq, tk):
    i = pl.program_id(1)
    qp = _qpos(i, tq)
    nh = MLA_HEADS
    acc_ref[...] = jnp.zeros_like(acc_ref)

    def step(j, st, diag):
        mask = (_kpos(j, tk) <= qp) if diag else None
        return tuple(_chain(q_ref[:, h * LANES:(h + 1) * LANES], _ktile(k_ref, j, tk, h * LANES, (h + 1) * LANES),
                            _ktile(v_ref, j, tk, (h // 2) * LANES, (h // 2 + 1) * LANES), mask, st[h], acc_ref, h)
                     for h in range(nh))

    nfull = (i * tq) // tk
    st = lax.fori_loop(0, nfull, lambda j, s: step(j, s, False), _init_state(nh, tq))
    st = step(nfull, st, True)
    o = [acc_ref[h] / st[h][1] for h in range(nh)]
    o_ref[...] = jnp.concatenate([_pair(o[0], o[1]), _pair(o[2], o[3])], axis=1).astype(BF16)


def _mla(mq, mk, mv, batch, seq):
    tq, tk = 512, 512
    nb = seq // tq
    row = lambda w: pl.BlockSpec((tq, w), lambda b, i: (b * nb + i, 0))
    per_b = lambda w: pl.BlockSpec((seq, w), lambda b, i: (b, 0))
    return pl.pallas_call(
        functools.partial(_mla_kernel, tq=tq, tk=tk),
        grid=(batch, nb),
        in_specs=[row(512), per_b(512), per_b(256)],
        out_specs=row(256),
        out_shape=jax.ShapeDtypeStruct((batch * seq, 256), BF16),
        scratch_shapes=[pltpu.VMEM((MLA_HEADS, tq, LANES), F32)],
        compiler_params=_params("parallel", "arbitrary"),
        name="mla_attn",
    )(mq, mk, mv)


def _swa_kernel(q_ref, k_ref, v_ref, sink_ref, o_ref, acc_ref, *, tq, tk):
    i = pl.program_id(1)
    qh = _half_heads(q_ref[...])
    qp = _qpos(i, tq)
    nh = SWA_HEADS
    acc_ref[...] = jnp.zeros_like(acc_ref)

    def step(j, st):
        dist = qp - _kpos(j, tk)
        mask = jnp.where(dist >= 0, dist, SWA_WINDOW) < SWA_WINDOW
        k = _ktile(k_ref, j, tk, 0, LANES)
        v = _ktile(v_ref, j, tk, 0, LANES)
        return tuple(_chain(qh[c], k, v, mask, st[c], acc_ref, c) for c in range(nh))

    lo = jnp.maximum(i * tq - SWA_WINDOW, 0) // tk
    st = lax.fori_loop(lo, ((i + 1) * tq - 1) // tk + 1, step, _init_state(nh, tq))
    sk = sink_ref[...]
    o = []
    for c, h in enumerate((0, 2, 1, 3)):
        m, l = st[c]
        sink = sk[:, h:h + 1] * LOG2E
        m2 = jnp.maximum(m, sink)
        sc = jnp.exp2(m - m2)
        o.append(acc_ref[c] * sc / (l * sc + jnp.exp2(sink - m2)))
    o_ref[...] = jnp.concatenate([_pair(o[0], o[1]), _pair(o[2], o[3])], axis=1).astype(BF16)


def _swa(sq, sk, sv, sinks, batch, seq):
    tq, tk = 256, 256
    nb = seq // tq
    row = lambda w: pl.BlockSpec((tq, w), lambda b, i: (b * nb + i, 0))
    per_b = lambda w: pl.BlockSpec((seq, w), lambda b, i: (b, 0))
    return pl.pallas_call(
        functools.partial(_swa_kernel, tq=tq, tk=tk),
        grid=(batch, nb),
        in_specs=[row(256), per_b(128), per_b(128), pl.BlockSpec(sinks.shape, lambda b, i: (0, 0))],
        out_specs=row(256),
        out_shape=jax.ShapeDtypeStruct((batch * seq, 256), BF16),
        scratch_shapes=[pltpu.VMEM((SWA_HEADS, tq, LANES), F32)],
        compiler_params=_params("parallel", "arbitrary"),
        name="swa_attn",
    )(sq, sk, sv, sinks)


def _outproj_kernel(x_ref, mod_ref, ng_ref, oa_ref, ob_ref, oc_ref, od_ref, w_ref, x1_ref, ht_ref):
    acc = _dot(oa_ref[...], w_ref[0:256, :])
    acc = acc + _dot(ob_ref[...], w_ref[256:512, :])
    acc = acc + _dot(oc_ref[...], w_ref[512:768, :])
    acc = acc + _dot(od_ref[...], w_ref[768:1024, :])
    x1 = x_ref[...] + mod_ref[0, 2:3, :] * acc
    x1_ref[...] = x1
    h = _rms(x1, ng_ref[...]) * (1.0 + mod_ref[0, 4:5, :]) + mod_ref[0, 3:4, :]
    ht_ref[...] = h.T.astype(BF16)


def _outproj(x2d, mod_l, norm_g, oa, ob, oc, od, w_out, seq):
    t, d = x2d.shape
    tm = 256
    tpb = seq // tm
    row = lambda w: pl.BlockSpec((tm, w), lambda i: (i, 0))
    full = lambda a: pl.BlockSpec(a.shape, lambda i: (0,) * a.ndim)
    return pl.pallas_call(
        _outproj_kernel,
        grid=(t // tm,),
        in_specs=[row(d), pl.BlockSpec((1, 6, d), lambda i: (i // tpb, 0, 0)), full(norm_g),
                  row(256), row(256), row(256), row(256), full(w_out)],
        out_specs=[row(d), pl.BlockSpec((d, tm), lambda i: (0, i))],
        out_shape=[jax.ShapeDtypeStruct((t, d), F32), jax.ShapeDtypeStruct((d, t), BF16)],
        compiler_params=_params("parallel"),
        name="outproj",
    )(x2d, mod_l, norm_g, oa, ob, oc, od, w_out)


_CAND_PIECES = [(0, 0, 8), (0, 8, 8), (1, 0, 8), (2, 0, 5), (3, 0, 4), (4, 0, 3), (5, 0, 2), (6, 0, 2), (7, 0, 2),
                (None, 0, 8)]


_CODE_UNIT = 2.0 ** 114
_TAKEN_BELOW = -(2.0 ** 119)
_INVALID = -(2.0 ** 100)


def _rank_code(r):
    return -(64.0 + r) * _CODE_UNIT


def _top16(s):
    tb = s.shape[1]
    row16 = lax.broadcasted_iota(jnp.int32, (PEER_TOPK, tb), 0)
    vals = jnp.zeros((PEER_TOPK, tb), F32)
    work = s
    for r in range(PEER_TOPK):
        m = jnp.max(work, axis=0, keepdims=True)
        work = jnp.where(work == m, _rank_code(r), work)
        vals = jnp.where(row16 == r, m, vals)
    return vals, work


def _router_head(h, ht, wq_ref, k1_ref, k2_ref, ea_ref, n1_ref, r2_ref, eb_ref):
    tb = ht.shape[1]
    nk = PEER_N_KEYS
    row8 = lax.broadcasted_iota(jnp.int32, (8, tb), 0)
    row16 = lax.broadcasted_iota(jnp.int32, (PEER_TOPK, tb), 0)
    o = pl.multiple_of(h * 2 * nk, 2 * nk)
    q1 = _dot(wq_ref[pl.ds(o, nk), :], ht).astype(BF16)
    q2 = _dot(wq_ref[pl.ds(o + nk, nk), :], ht).astype(BF16)
    s1 = _dot(k1_ref[...], q1)
    s2 = _dot(k2_ref[...], q2)
    v1, code1 = _top16(s1)
    v2, code2 = _top16(s2)
    top = v1[0:1] + v2[0:1]

    def cells(r1, c0):
        return v1[8:16] + v2[0:1] if r1 is None else v1[r1:r1 + 1] + v2[c0:c0 + 8]

    pieces = []
    for r1, c0, valid in _CAND_PIECES:
        p = cells(r1, c0)
        pieces.append(p if valid == 8 else jnp.where(row8 < valid, p, _INVALID))
    for _ in range(PEER_TOPK):
        m = pieces[0]
        for p in pieces[1:]:
            m = jnp.maximum(m, p)
        m = jnp.max(m, axis=0, keepdims=True)
        pieces = [jnp.where(p == m, _rank_code(0), p) for p in pieces]
    counts = jnp.zeros((PEER_TOPK, tb), F32)
    z = jnp.zeros((1, tb), F32)
    for p, (r1, c0, valid) in zip(pieces, _CAND_PIECES):
        taken = p < _TAKEN_BELOW
        if r1 is None:
            counts = counts + jnp.concatenate([jnp.zeros((8, tb), F32), jnp.where(taken, 1.0, 0.0)], axis=0)
        else:
            n = jnp.sum(jnp.where(taken, 1.0, 0.0), axis=0, keepdims=True)
            counts = counts + jnp.where(row16 == r1, n, 0.0)
        z = z + jnp.sum(jnp.where(taken, jnp.exp(cells(r1, c0) - top), 0.0), axis=0, keepdims=True)
    n1 = jnp.zeros((nk, tb), F32)
    for r in range(PEER_TOPK):
        n1 = jnp.where(code1 == _rank_code(r), counts[r:r + 1], n1)
    ea_ref[h] = jnp.exp(s1 - v1[0:1])
    n1_ref[h] = n1
    rank2 = jnp.where(code2 < _TAKEN_BELOW, code2 * (-1.0 / _CODE_UNIT) - 64.0, float(nk))
    r2_ref[h] = rank2.astype(BF16)
    eb_ref[h] = (jnp.exp(s2 - v2[0:1]) / z).astype(BF16)


def _router_kernel(ht_ref, wq_ref, k1_ref, k2_ref, ea_ref, n1_ref, r2_ref, eb_ref):
    ht = ht_ref[...]

    def pair(p, carry):
        for hh in range(2):
            _router_head(2 * p + hh, ht, wq_ref, k1_ref, k2_ref, ea_ref, n1_ref, r2_ref, eb_ref)
        return carry

    lax.fori_loop(0, PEER_HEADS // 2, pair, 0)


def _router(ht, wq_t, k1, k2):
    d, t = ht.shape
    tb = 128
    full = lambda a: pl.BlockSpec(a.shape, lambda i: (0,) * a.ndim)
    out = pl.BlockSpec((PEER_HEADS, PEER_N_KEYS, tb), lambda i: (0, 0, i))
    return pl.pallas_call(
        _router_kernel,
        grid=(t // tb,),
        in_specs=[pl.BlockSpec((d, tb), lambda i: (0, i)), full(wq_t), full(k1), full(k2)],
        out_specs=[out, out, out, out],
        out_shape=[jax.ShapeDtypeStruct((PEER_HEADS, PEER_N_KEYS, t), dt) for dt in (F32, F32, BF16, BF16)],
        compiler_params=_params("parallel"),
        name="peer_router",
    )(ht, wq_t, k1, k2)


def _gelu_tanh(x):
    k = 2.0 * math.sqrt(2.0 / math.pi) * math.log2(math.e)
    return x / (1.0 + jnp.exp2(x * (-k - (k * 0.044715) * (x * x))))


def _peer_kernel(ht_ref, u_ref, vt_ref, ea_ref, n1_ref, r2_ref, eb_ref, x_ref, mod_ref, fg_ref, o_ref, acc_ref, wa_ref,
                 wb_ref, *, final, chunk):
    e = pl.program_id(1)
    n_e = pl.num_programs(1) - 1
    nk = PEER_N_KEYS
    n_i1 = u_ref.shape[0] // nk
    sub = 16
    assert 2 * n_i1 == 8

    @pl.when(e == 0)
    def _():
        acc_ref[...] = jnp.zeros_like(acc_ref)
        wb_ref[...] = jnp.zeros_like(wb_ref)

    def step(write_ref, read_ref, base):
        def tokens(c, carry):
            lanes = pl.ds(pl.multiple_of(c * chunk, chunk), chunk)
            act = _gelu_tanh(_dot(u_ref[...], ht_ref[:, lanes])).astype(BF16)
            acc_ref[:, lanes] += _dot(vt_ref[...], read_ref[:, lanes])
            for j in range(n_i1):
                m = None
                for h in range(PEER_HEADS):
                    n_row = jnp.broadcast_to(n1_ref[h, base + j:base + j + 1, lanes], (sub, chunk)).astype(BF16)
                    ea_row = jnp.broadcast_to(ea_ref[h, base + j:base + j + 1, lanes], (sub, chunk)).astype(BF16)
                    r2 = r2_ref[h, :, lanes].reshape(nk // sub, sub, chunk)
                    eb = eb_ref[h, :, lanes].reshape(nk // sub, sub, chunk)
                    term = jnp.where(r2 < n_row[None], eb, jnp.zeros((), BF16)) * ea_row[None]
                    m = term if m is None else m + term
                write_ref[j * nk:(j + 1) * nk, lanes] = m.reshape(nk, chunk) * act[j * nk:(j + 1) * nk]
            return carry

        lax.fori_loop(0, ht_ref.shape[1] // chunk, tokens, 0)

    @pl.when(lax.rem(e, 2) == 0)
    def _():
        step(wa_ref, wb_ref, 0)

    @pl.when(lax.rem(e, 2) == 1)
    def _():
        step(wb_ref, wa_ref, n_i1)

    @pl.when(e == n_e)
    def _():
        y = x_ref[...] + mod_ref[0, 5:6, :] * acc_ref[...].T
        if final:
            y = _rms(y, fg_ref[...])
        o_ref[...] = y


def _peer(ht, u_bf, vt_bf, ea, n1, r2, eb, x1, mod_l, final_g, seq, final):
    d, t = ht.shape
    n_exp = u_bf.shape[0]
    tb = 1024 if seq % 1024 == 0 else 512
    eb_blk = 512
    n_e = n_exp // eb_blk
    n_i1 = eb_blk // PEER_N_KEYS
    tpb = seq // tb
    cur = lambda e: jnp.minimum(e, n_e - 1)
    i1_spec = pl.BlockSpec((PEER_HEADS, 8, tb), lambda i, e: (0, cur(e) // (8 // n_i1), i))
    tok3 = pl.BlockSpec((PEER_HEADS, PEER_N_KEYS, tb), lambda i, e: (0, 0, i))
    return pl.pallas_call(
        functools.partial(_peer_kernel, final=final, chunk=512),
        grid=(t // tb, n_e + 1),
        in_specs=[pl.BlockSpec((d, tb), lambda i, e: (0, i)),
                  pl.BlockSpec((eb_blk, d), lambda i, e: (cur(e), 0)),
                  pl.BlockSpec((d, eb_blk), lambda i, e: (0, jnp.maximum(e - 1, 0))),
                  i1_spec, i1_spec, tok3, tok3,
                  pl.BlockSpec((tb, d), lambda i, e: (i, 0)),
                  pl.BlockSpec((1, 6, d), lambda i, e: (i // tpb, 0, 0)),
                  pl.BlockSpec(final_g.shape, lambda i, e: (0, 0))],
        out_specs=pl.BlockSpec((tb, d), lambda i, e: (i, 0)),
        out_shape=jax.ShapeDtypeStruct((t, d), F32),
        scratch_shapes=[pltpu.VMEM((d, tb), F32), pltpu.VMEM((eb_blk, tb), BF16), pltpu.VMEM((eb_blk, tb), BF16)],
        compiler_params=_params("parallel", "arbitrary"),
        name="peer_experts",
    )(ht, u_bf, vt_bf, ea, n1, r2, eb, x1, mod_l, final_g)


def kernel(x, c, ada_w, ada_b, norm_mix_g, norm_ffn_g, w_in, nsa_cmp_pos_k, nsa_cmp_pos_v, nsa_cmp_wk, nsa_cmp_wv, diff_lam_q1, diff_lam_k1, diff_lam_q2, diff_lam_k2, diff_sub_g, mla_q_norm_g, mla_w_uq, mla_kv_norm_g, mla_w_ukv, swa_sinks, w_out, peer_w_q, peer_sub_k1, peer_sub_k2, peer_u, peer_v, final_g):
    batch, seq, d = x.shape
    depth = w_in.shape[0]
    assert seq % 512 == 0
    x2d = x.reshape(batch * seq, d)
    mod = _adaln(c, ada_w, ada_b).reshape(depth, batch, 6, d)
    table = _rope_table(seq)
    mixw = 4 * HEAD_DIM
    swa_rows = 3 * mixw + np.concatenate([h * HEAD_DIM + np.arange(HEAD_DIM) for h in (0, 2, 1, 3)])
    out_rows = jnp.asarray(np.concatenate([np.arange(3 * mixw), swa_rows]), jnp.int32)
    pad128 = lambda v: jnp.pad(v, (0, LANES - v.shape[0])).reshape(1, LANES)
    fg = final_g.reshape(1, d)
    for l in range(depth):
        w_big = _take_cols(w_in[l], _IN_IDX, _IN_SGN)
        wuq = _take_cols(mla_w_uq[l], _UQ_IDX, _UQ_SGN)
        wukv = _take_cols(mla_w_ukv[l], _UKV_IDX, _UKV_SGN)
        (nq, nk, nv, ng, dq, dk, dv, mq, mk, mv, sq, sk, sv) = _inproj(
            x2d, mod[l], norm_mix_g[l].reshape(1, d), w_big, table, wuq, wukv,
            mla_q_norm_g[l].reshape(1, -1), mla_kv_norm_g[l].reshape(1, -1), seq)
        ocmp, sel = _nsa_cmp(nq, nk, nv, nsa_cmp_wk[l], nsa_cmp_wv[l], nsa_cmp_pos_k[l], nsa_cmp_pos_v[l], batch, seq)
        o_a = _nsa(nq, nk, nv, sel, ng, ocmp, batch, seq)
        lamv = jnp.concatenate([pad128(diff_lam_q1[l]), pad128(diff_lam_k1[l]),
                                pad128(diff_lam_q2[l]), pad128(diff_lam_k2[l])], axis=0)
        sub_g2 = jnp.concatenate([diff_sub_g[l], diff_sub_g[l]]).reshape(1, LANES)
        o_b = _diff(dq, dk, dv, lamv, sub_g2, l, batch, seq)
        o_c = _mla(mq, mk, mv, batch, seq)
        o_d = _swa(sq, sk, sv, pad128(swa_sinks[l]), batch, seq)
        w_o = jnp.take(w_out[l], out_rows, axis=0).astype(BF16)
        x1, ht = _outproj(x2d, mod[l], norm_ffn_g[l].reshape(1, d), o_a, o_b, o_c, o_d, w_o, seq)
        ea, n1, r2, eb = _router(ht, peer_w_q[l].T.astype(BF16), peer_sub_k1[l].astype(BF16),
                                 peer_sub_k2[l].astype(BF16))
        x2d = _peer(ht, peer_u[l].astype(BF16), peer_v[l].T.astype(BF16), ea, n1, r2, eb, x1, mod[l], fg,
                    seq, final=(l == depth - 1))
    return x2d.reshape(batch, seq, d)
```

```python
import functools
import math

import numpy as np
import jax
import jax.numpy as jnp
from jax import lax
from jax.experimental import pallas as pl
from jax.experimental.pallas import tpu as pltpu

F32 = jnp.float32
BF16 = jnp.bfloat16

HEAD_DIM = 64
ROPE_THETA = 10000.0
EPS = 1e-6
NEG = -1e30
FORCE = 1e4

NSA_HEADS = 4
NSA_CMP_LEN = 32
NSA_CMP_STRIDE = 16
NSA_SEL_LEN = 64
NSA_TOP_N = 16
NSA_WINDOW = 512

DIFF_HEADS = 4
DIFF_QK_DIM = 32
DIFF_V_DIM = 64

MLA_HEADS = 4
MLA_Q_RANK = 256
MLA_KV_RANK = 128
MLA_NOPE_DIM = 64
MLA_ROPE_DIM = 32
MLA_V_DIM = 64

SWA_HEADS = 4
SWA_KV_HEADS = 2
SWA_WINDOW = 128

PEER_HEADS = 8
PEER_N_KEYS = 128
PEER_TOPK = 16
PEER_QUERY_DIM = 256

LOG2E = math.log2(math.e)
LANES = 128
PACKED_ROWS = 16
VMEM_LIMIT = 56 * 1024 * 1024

TILE_PROJ_ROWS = 256
TILE_ADALN_COLS = 1536
TILE_CMP_ROWS = 256
TILE_NSA = (512, 512)
TILE_DIFF = (256, 512)
TILE_MLA = (512, 512)
TILE_SWA = (256, 256)
TILE_ROUTER_TOKENS = 128
TILE_PEER_TOKENS = 1024
TILE_PEER_EXPERTS = 512
TILE_PEER_CHUNK = 512


def _dot(a, b):
    return jnp.dot(a, b, preferred_element_type=F32)


def _dot_nt(a, b):
    return lax.dot_general(a, b, (((1,), (1,)), ((), ())), preferred_element_type=F32)


def _params(*sem):
    return pltpu.CompilerParams(dimension_semantics=sem, vmem_limit_bytes=VMEM_LIMIT)


def _rms(x, g):
    return x * lax.rsqrt(jnp.mean(x * x, axis=-1, keepdims=True) + EPS) * g


def _rot_idx(base, dim):
    half = dim // 2
    idx = np.concatenate([base + half + np.arange(half), base + np.arange(half)])
    sgn = np.concatenate([-np.ones(half), np.ones(half)])
    return idx, sgn


def _in_plan():
    d = HEAD_DIM
    nsa0 = 0
    nsa_cols = NSA_HEADS * d + 6 * d + 3 * NSA_HEADS
    diff0 = nsa0 + nsa_cols
    diff_cols = 2 * DIFF_HEADS * 2 * DIFF_QK_DIM + DIFF_HEADS * DIFF_V_DIM
    mla0 = diff0 + diff_cols
    mla_cols = MLA_Q_RANK + MLA_KV_RANK + MLA_ROPE_DIM
    swa0 = mla0 + mla_cols
    idx, sgn, off = [], [], {}

    def add(name, i, s=None):
        i = np.asarray(i, np.int64)
        s = np.ones(len(i)) if s is None else np.asarray(s, np.float64)
        pad = (-len(i)) % LANES
        off[name] = sum(len(a) for a in idx)
        idx.append(np.concatenate([i, np.zeros(pad, np.int64)]))
        sgn.append(np.concatenate([s, np.zeros(pad)]))

    def heads_rot(base, nheads, dim):
        ii, ss = zip(*[_rot_idx(base + h * dim, dim) for h in range(nheads)])
        return np.concatenate(ii), np.concatenate(ss)

    nq = nsa0 + np.arange(NSA_HEADS * d)
    add("nq", nq)
    add("nqr", *heads_rot(nsa0, NSA_HEADS, d))
    kb = nsa0 + NSA_HEADS * d
    kc, vc, ksl, vsl, kw, vw = [kb + j * d for j in range(6)]
    dup = lambda b: np.concatenate([b + np.arange(d), b + np.arange(d)])
    add("nk", np.concatenate([dup(kc), dup(ksl), dup(kw)]))
    kr = [_rot_idx(b, d) for b in (kc, kc, ksl, ksl, kw, kw)]
    add("nkr", np.concatenate([a for a, _ in kr]), np.concatenate([b for _, b in kr]))
    add("nv", np.concatenate([dup(vc), dup(vsl), dup(vw)]))
    add("ng", kb + 6 * d + np.arange(3 * NSA_HEADS))
    nqk = DIFF_HEADS * 2 * DIFF_QK_DIM
    add("dq", diff0 + np.arange(nqk))
    add("dqr", *heads_rot(diff0, 2 * DIFF_HEADS, DIFF_QK_DIM))
    add("dk", diff0 + nqk + np.arange(nqk))
    add("dkr", *heads_rot(diff0 + nqk, 2 * DIFF_HEADS, DIFF_QK_DIM))
    add("dv", diff0 + 2 * nqk + np.arange(DIFF_HEADS * DIFF_V_DIM))
    add("mcq", mla0 + np.arange(MLA_Q_RANK))
    add("mckv", mla0 + MLA_Q_RANK + np.arange(MLA_KV_RANK))
    kr0 = mla0 + MLA_Q_RANK + MLA_KV_RANK
    z64 = np.zeros(MLA_NOPE_DIM, np.int64)
    add("mkr", np.concatenate([z64, kr0 + np.arange(MLA_ROPE_DIM)]),
        np.concatenate([np.zeros(MLA_NOPE_DIM), np.ones(MLA_ROPE_DIM)]))
    ri, rs = _rot_idx(kr0, MLA_ROPE_DIM)
    add("mkrr", np.concatenate([z64, ri]), np.concatenate([np.zeros(MLA_NOPE_DIM), rs]))
    order = [0, 2, 1, 3]
    add("sq", np.concatenate([swa0 + h * d + np.arange(d) for h in order]))
    sr = [_rot_idx(swa0 + h * d, d) for h in order]
    add("sqr", np.concatenate([a for a, _ in sr]), np.concatenate([b for _, b in sr]))
    sk0 = swa0 + SWA_HEADS * d
    add("sk", sk0 + np.arange(SWA_KV_HEADS * d))
    add("skr", *heads_rot(sk0, SWA_KV_HEADS, d))
    add("sv", sk0 + SWA_KV_HEADS * d + np.arange(SWA_KV_HEADS * d))
    return np.concatenate(idx), np.concatenate(sgn), off


_IN_IDX, _IN_SGN, _OFF = _in_plan()


def _mla_plans():
    qd = MLA_NOPE_DIM + MLA_ROPE_DIM
    qi, qs, ri, rs = [], [], [], []
    for h in range(MLA_HEADS):
        b = h * qd
        qi += [b + np.arange(qd), np.zeros(LANES - qd, np.int64)]
        qs += [np.ones(qd), np.zeros(LANES - qd)]
        a, s = _rot_idx(b + MLA_NOPE_DIM, MLA_ROPE_DIM)
        ri += [np.zeros(MLA_NOPE_DIM, np.int64), a, np.zeros(LANES - qd, np.int64)]
        rs += [np.zeros(MLA_NOPE_DIM), s, np.zeros(LANES - qd)]
    kd = MLA_NOPE_DIM + MLA_V_DIM
    ki, ks, vi = [], [], []
    for h in range(MLA_HEADS):
        ki += [h * kd + np.arange(MLA_NOPE_DIM), np.zeros(LANES - MLA_NOPE_DIM, np.int64)]
        ks += [np.ones(MLA_NOPE_DIM), np.zeros(LANES - MLA_NOPE_DIM)]
        vi += [h * kd + MLA_NOPE_DIM + np.arange(MLA_V_DIM)]
    uq_idx = np.concatenate(qi + ri)
    uq_sgn = np.concatenate(qs + rs)
    ukv_idx = np.concatenate(ki + vi)
    ukv_sgn = np.concatenate(ks + [np.ones(MLA_HEADS * MLA_V_DIM)])
    return uq_idx, uq_sgn, ukv_idx, ukv_sgn


_UQ_IDX, _UQ_SGN, _UKV_IDX, _UKV_SGN = _mla_plans()


def _take_cols(w, idx, sgn):
    pieces, start = [], 0
    for i in range(1, len(idx) + 1):
        same_run = (i < len(idx) and sgn[i] == sgn[start]
                    and (sgn[i] == 0.0 or idx[i] == idx[i - 1] + 1))
        if not same_run:
            n, s = i - start, float(sgn[start])
            run = w[:, int(idx[start]):int(idx[start]) + n]
            pieces.append(jnp.zeros((w.shape[0], n), w.dtype) if s == 0.0 else (run if s == 1.0 else -run))
            start = i
    return jnp.concatenate(pieces, axis=1).astype(BF16)


def _rope_table(seq):
    def cs(dim):
        inv = 1.0 / (ROPE_THETA ** (jnp.arange(0, dim, 2, dtype=F32) / dim))
        ang = jnp.arange(seq, dtype=F32)[:, None] * inv[None, :]
        c, s = jnp.cos(ang), jnp.sin(ang)
        return jnp.concatenate([c, c], 1), jnp.concatenate([s, s], 1)
    ch, sh = cs(HEAD_DIM)
    cd, sd = cs(DIFF_QK_DIM)
    cm, sm = cs(MLA_ROPE_DIM)
    one = jnp.ones((seq, MLA_NOPE_DIM), F32)
    z64 = jnp.zeros((seq, MLA_NOPE_DIM), F32)
    z32 = jnp.zeros((seq, LANES - MLA_NOPE_DIM - MLA_ROPE_DIM), F32)
    parts = [jnp.tile(ch, (1, 2)), jnp.tile(sh, (1, 2)), jnp.tile(cd, (1, 4)), jnp.tile(sd, (1, 4)),
             jnp.concatenate([one, cm, z32], 1), jnp.concatenate([z64, sm, z32], 1),
             jnp.concatenate([z64, cm, z32], 1), jnp.concatenate([z64, sm, z32], 1)]
    return jnp.concatenate(parts, 1)


def _adaln_kernel(c_ref, w_ref, b_ref, o_ref):
    c = c_ref[...]
    sc = (c * jax.nn.sigmoid(c)).astype(BF16)
    o_ref[0] = _dot(sc, w_ref[0].astype(BF16)) + b_ref[0]


def _adaln(c, ada_w, ada_b):
    nl, d, n6 = ada_w.shape
    b = c.shape[0]
    tn = TILE_ADALN_COLS
    return pl.pallas_call(
        _adaln_kernel,
        grid=(nl, n6 // tn),
        in_specs=[pl.BlockSpec((b, d), lambda l, j: (0, 0)),
                  pl.BlockSpec((1, d, tn), lambda l, j: (l, 0, j)),
                  pl.BlockSpec((1, 1, tn), lambda l, j: (l, 0, j))],
        out_specs=pl.BlockSpec((1, b, tn), lambda l, j: (l, 0, j)),
        out_shape=jax.ShapeDtypeStruct((nl, b, n6), F32),
        compiler_params=_params("parallel", "parallel"),
        name="adaln",
    )(c, ada_w, ada_b.reshape(nl, 1, n6))


def _inproj_kernel(x_ref, mod_ref, ng_ref, w_ref, tab_ref, wuq_ref, wukv_ref, gq_ref, gkv_ref,
                   nq_ref, nk_ref, nv_ref, ngo_ref, dq_ref, dk_ref, dv_ref,
                   mq_ref, mk_ref, mv_ref, sq_ref, sk_ref, sv_ref):
    x = x_ref[...]
    h = _rms(x, ng_ref[...]) * (1.0 + mod_ref[0, 1:2, :]) + mod_ref[0, 0:1, :]
    hb = h.astype(BF16)

    def mm(name, width):
        o = _OFF[name]
        return _dot(hb, w_ref[:, o:o + width])

    def tab(j, reps):
        t = tab_ref[:, j * LANES:(j + 1) * LANES]
        return t if reps == 1 else jnp.concatenate([t] * reps, axis=1)

    def rope(name, rname, width, cj, scale=1.0):
        r = mm(name, width) * tab(cj, width // LANES) + mm(rname, width) * tab(cj + 1, width // LANES)
        return r if scale == 1.0 else r * scale

    d = HEAD_DIM
    nq_ref[...] = rope("nq", "nqr", 256, 0, LOG2E * d ** -0.5).astype(BF16)
    nk_ref[...] = rope("nk", "nkr", 384, 0).astype(BF16)
    nv_ref[...] = mm("nv", 384).astype(BF16)
    ngo_ref[...] = jax.nn.sigmoid(mm("ng", LANES))
    dq_ref[...] = rope("dq", "dqr", 256, 2, LOG2E * DIFF_QK_DIM ** -0.5).astype(BF16)
    dk_ref[...] = rope("dk", "dkr", 256, 2).astype(BF16)
    dv_ref[...] = mm("dv", 256).astype(BF16)
    cq = _rms(mm("mcq", MLA_Q_RANK), gq_ref[...]).astype(BF16)
    nh = MLA_HEADS * LANES
    qa = _dot(cq, wuq_ref[:, 0:nh])
    qb = _dot(cq, wuq_ref[:, nh:2 * nh])
    mq = (qa * tab(4, MLA_HEADS) + qb * tab(5, MLA_HEADS)) * (LOG2E * (MLA_NOPE_DIM + MLA_ROPE_DIM) ** -0.5)
    mq_ref[...] = mq.astype(BF16)
    ckv = _rms(mm("mckv", MLA_KV_RANK), gkv_ref[...]).astype(BF16)
    kk = _dot(ckv, wukv_ref[:, 0:nh])
    kr = mm("mkr", LANES) * tab(6, 1) + mm("mkrr", LANES) * tab(7, 1)
    mk_ref[...] = (kk + jnp.concatenate([kr] * MLA_HEADS, axis=1)).astype(BF16)
    mv_ref[...] = _dot(ckv, wukv_ref[:, nh:nh + MLA_HEADS * MLA_V_DIM]).astype(BF16)
    sq_ref[...] = rope("sq", "sqr", 256, 0, LOG2E * d ** -0.5).astype(BF16)
    sk_ref[...] = rope("sk", "skr", 128, 0).astype(BF16)
    sv_ref[...] = mm("sv", 128).astype(BF16)


def _inproj(x2d, mod_l, norm_g, w_big, table, wuq, wukv, gq, gkv, seq):
    t, d = x2d.shape
    tm = TILE_PROJ_ROWS
    tpb = seq // tm
    widths = [256, 384, 384, 128, 256, 256, 256, 512, 512, 256, 256, 128, 128]
    dts = [BF16, BF16, BF16, F32, BF16, BF16, BF16, BF16, BF16, BF16, BF16, BF16, BF16]
    full = lambda a: pl.BlockSpec(a.shape, lambda i: (0,) * a.ndim)
    return pl.pallas_call(
        _inproj_kernel,
        grid=(t // tm,),
        in_specs=[pl.BlockSpec((tm, d), lambda i: (i, 0)),
                  pl.BlockSpec((1, 6, d), lambda i: (i // tpb, 0, 0)),
                  full(norm_g), full(w_big),
                  pl.BlockSpec((tm, table.shape[1]), lambda i: (i % tpb, 0)),
                  full(wuq), full(wukv), full(gq), full(gkv)],
        out_specs=[pl.BlockSpec((tm, w), lambda i: (i, 0)) for w in widths],
        out_shape=[jax.ShapeDtypeStruct((t, w), dt) for w, dt in zip(widths, dts)],
        compiler_params=_params("parallel"),
        name="inproj",
    )(x2d, mod_l, norm_g, w_big, table, wuq, wukv, gq, gkv)


def _lane_mask(lo, hi):
    lane = lax.broadcasted_iota(jnp.int32, (1, LANES), 1)
    return (lane >= lo) & (lane < hi)


def _masked(q, lo, hi):
    return jnp.where(_lane_mask(lo, hi), q, jnp.zeros_like(q))


def _chain(q, k, v, mask, state, acc_ref, c):
    m, l = state
    s = _dot_nt(q, k)
    if mask is not None:
        s = jnp.where(mask, s, NEG)
    m2 = jnp.maximum(m, jnp.max(s, axis=-1, keepdims=True))
    a = jnp.exp2(m - m2)
    p = jnp.exp2(s - m2)
    acc_ref[c] = a * acc_ref[c] + _dot(p.astype(BF16), v)
    return m2, a * l + jnp.sum(p, axis=-1, keepdims=True)


def _init_state(n, rows):
    return tuple((jnp.full((rows, 1), NEG, F32), jnp.zeros((rows, 1), F32)) for _ in range(n))


def _ktile(ref, j, tk, c0, c1):
    return ref[pl.ds(pl.multiple_of(j * tk, tk), tk), c0:c1]


def _qpos(i, tq, reps=1):
    p = i * tq + lax.broadcasted_iota(jnp.int32, (tq, 1), 0)
    return p if reps == 1 else jnp.concatenate([p] * reps, axis=0)


def _kpos(j, tk):
    return j * tk + lax.broadcasted_iota(jnp.int32, (1, tk), 1)


def _half_heads(q):
    return [_masked(q[:, c * LANES:(c + 1) * LANES], 64 * hh, 64 * hh + 64) for c in range(2) for hh in range(2)]


def _pair(lo_val, hi_val):
    return jnp.where(_lane_mask(0, 64), lo_val, hi_val)


def _nsa_cmp_kernel(q_ref, kc_ref, vc_ref, wk_ref, wv_ref, pk_ref, pv_ref, ov_ref, oc_ref, sel_ref, *, top_n, n_sel):
    half = wk_ref.shape[1]

    def compress(x_ref, w_ref, p_ref):
        x = x_ref[...]
        a = _dot(x, w_ref[0])
        b = _dot(x, w_ref[1])
        p = jnp.broadcast_to(p_ref[...], (8, 2 * half)).astype(BF16)
        const = (_dot(p[:, 0:half], w_ref[0]) + _dot(p[:, half:2 * half], w_ref[1]))[0:1]
        return a + jnp.concatenate([b[1:], b[:1]], axis=0) + const

    kcmp = compress(kc_ref, wk_ref, pk_ref).astype(BF16)
    vcmp = compress(vc_ref, wv_ref, pv_ref).astype(BF16)
    ncp = kcmp.shape[0]
    ov = ov_ref[...]
    rb = TILE_CMP_ROWS
    cend = NSA_CMP_STRIDE * lax.broadcasted_iota(jnp.int32, (1, ncp), 1) + (NSA_CMP_LEN - 1)
    lane = lax.broadcasted_iota(jnp.int32, (1, LANES), 1)

    def block(r, carry):
        r0 = pl.multiple_of(r * rb, rb)
        q = q_ref[pl.ds(r0, rb), :]
        tpos = r0 + lax.broadcasted_iota(jnp.int32, (rb, 1), 0)
        vis = cend <= tpos
        psum = jnp.zeros((rb, ncp), F32)
        outs = []
        for c in range(2):
            halves = []
            for hh in range(2):
                qm = _masked(q[:, c * LANES:(c + 1) * LANES], 64 * hh, 64 * hh + 64)
                s = jnp.where(vis, _dot_nt(qm, kcmp), NEG)
                e = jnp.exp2(s - jnp.max(s, axis=-1, keepdims=True))
                p = jnp.where(vis, e / jnp.sum(e, axis=-1, keepdims=True), 0.0)
                psum = psum + p
                halves.append(_dot(p.astype(BF16), vcmp))
            outs.append(_pair(halves[0], halves[1]))
        oc_ref[pl.ds(r0, rb), :] = jnp.concatenate(outs, axis=1)
        hi = psum.astype(BF16)
        lo = (psum - hi.astype(F32)).astype(BF16)
        imp = _dot(hi, ov) + _dot(lo, ov)
        nsp = -(-n_sel // 8) * 8
        imp_t = imp.T[0:nsp]
        blk = lax.broadcasted_iota(jnp.int32, (nsp, 1), 0)
        qblk = (r0 + lax.broadcasted_iota(jnp.int32, (1, rb), 1)) // NSA_SEL_LEN
        allowed = blk <= qblk
        forced = (blk == 0) | (blk == qblk) | (blk == qblk - 1)
        impf = jnp.where(allowed, jnp.where(forced, FORCE, imp_t), NEG)
        rank = jnp.zeros((nsp, rb), F32)
        for j in range(n_sel):
            row = impf[j:j + 1, :]
            rank = rank + jnp.where(blk > j, jnp.where(row >= impf, 1.0, 0.0), jnp.where(row > impf, 1.0, 0.0))
        sel_t = jnp.where(allowed & (rank < top_n), 1.0, 0.0)
        if nsp < LANES:
            sel_t = jnp.concatenate([sel_t, jnp.zeros((LANES - nsp, rb), F32)], axis=0)
        sel_ref[pl.ds(r0, rb), :] = sel_t.T.astype(BF16)
        return carry

    lax.fori_loop(0, q_ref.shape[0] // rb, block, 0)


def _nsa_cmp(nq, nk, nv, wk, wv, pos_k, pos_v, batch, seq):
    d = HEAD_DIM
    nc = seq // NSA_CMP_STRIDE
    ncp = -(-nc // LANES) * LANES
    n_sel = seq // NSA_SEL_LEN
    assert n_sel <= LANES
    top_n = min(NSA_TOP_N, n_sel)

    def seg(a):
        a = a[:, :d].reshape(batch, nc, NSA_CMP_STRIDE * d)
        return jnp.pad(a, ((0, 0), (0, ncp - nc), (0, 0))).reshape(batch * ncp, NSA_CMP_STRIDE * d)

    half = NSA_CMP_STRIDE * d
    dupw = lambda w: jnp.concatenate([w, w], axis=1).reshape(2, half, 2 * d).astype(BF16)
    cpos = NSA_CMP_STRIDE * np.arange(ncp)[:, None] + np.arange(NSA_CMP_LEN)[None, :]
    ovl = np.zeros((ncp, LANES), np.float32)
    for j in range(n_sel):
        ovl[:, j] = (cpos // NSA_SEL_LEN == j).mean(axis=1)
    ovl[nc - 1:, :] = 0.0
    full = lambda a: pl.BlockSpec(a.shape, lambda b: (0,) * a.ndim)
    wk2, wv2 = dupw(wk), dupw(wv)
    pk, pv = pos_k.reshape(1, -1), pos_v.reshape(1, -1)
    ov = jnp.asarray(ovl, BF16)
    return pl.pallas_call(
        functools.partial(_nsa_cmp_kernel, top_n=top_n, n_sel=n_sel),
        grid=(batch,),
        in_specs=[pl.BlockSpec((seq, 256), lambda b: (b, 0)),
                  pl.BlockSpec((ncp, half), lambda b: (b, 0)),
                  pl.BlockSpec((ncp, half), lambda b: (b, 0)),
                  full(wk2), full(wv2), full(pk), full(pv), full(ov)],
        out_specs=[pl.BlockSpec((seq, 256), lambda b: (b, 0)),
                   pl.BlockSpec((seq, LANES), lambda b: (b, 0))],
        out_shape=[jax.ShapeDtypeStruct((batch * seq, 256), F32),
                   jax.ShapeDtypeStruct((batch * seq, LANES), BF16)],
        compiler_params=_params("parallel"),
        name="nsa_cmp",
    )(nq, seg(nk), seg(nv), wk2, wv2, pk, pv, ov)


def _nsa_kernel(q_ref, k_ref, v_ref, sel_ref, g_ref, oc_ref, e_ref, o_ref, acc_ref, *, tq, tk):
    i = pl.program_id(1)
    qh = _half_heads(q_ref[...])
    qp = _qpos(i, tq)
    sel = sel_ref[...]
    nh = NSA_HEADS
    acc_ref[...] = jnp.zeros_like(acc_ref)

    def sel_step(j, st, diag):
        mv = _dot(sel, e_ref[:, pl.ds(pl.multiple_of(j * tk, tk), tk)])
        if diag:
            mv = jnp.where(_kpos(j, tk) <= qp, mv, 0.0)
        mask = mv > 0.5
        k = _ktile(k_ref, j, tk, 128, 256)
        v = _ktile(v_ref, j, tk, 128, 256)
        return tuple(_chain(qh[h], k, v, mask, st[h], acc_ref, h) for h in range(nh))

    nfull = (i * tq) // tk
    st = lax.fori_loop(0, nfull, lambda j, s: sel_step(j, s, False), _init_state(nh, tq))
    st_sel = sel_step(nfull, st, True)

    def win_step(j, st):
        dist = qp - _kpos(j, tk)
        mask = jnp.where(dist >= 0, dist, NSA_WINDOW) < NSA_WINDOW
        k = _ktile(k_ref, j, tk, 256, 384)
        v = _ktile(v_ref, j, tk, 256, 384)
        return tuple(_chain(qh[h], k, v, mask, st[h], acc_ref, nh + h) for h in range(nh))

    wlo = jnp.maximum(i * tq - NSA_WINDOW, 0) // tk
    st_win = lax.fori_loop(wlo, nfull + 1, win_step, _init_state(nh, tq))
    g = g_ref[...]
    oc = oc_ref[...]
    outs = []
    for c in range(2):
        occ = oc[:, c * LANES:(c + 1) * LANES]

        def comb(h):
            o_sel = acc_ref[h] / st_sel[h][1]
            o_win = acc_ref[nh + h] / st_win[h][1]
            return g[:, 3 * h:3 * h + 1] * occ + g[:, 3 * h + 1:3 * h + 2] * o_sel + g[:, 3 * h + 2:3 * h + 3] * o_win

        outs.append(_pair(comb(2 * c), comb(2 * c + 1)))
    o_ref[...] = jnp.concatenate(outs, axis=1).astype(BF16)


def _nsa(nq, nk, nv, sel, gates, ocmp, batch, seq):
    tq, tk = TILE_NSA
    nb = seq // tq
    expand = np.zeros((LANES, seq), np.float32)
    for j in range(seq // NSA_SEL_LEN):
        expand[j, j * NSA_SEL_LEN:(j + 1) * NSA_SEL_LEN] = 1.0
    e = jnp.asarray(expand, BF16)
    row = lambda w: pl.BlockSpec((tq, w), lambda b, i: (b * nb + i, 0))
    per_b = lambda w: pl.BlockSpec((seq, w), lambda b, i: (b, 0))
    return pl.pallas_call(
        functools.partial(_nsa_kernel, tq=tq, tk=tk),
        grid=(batch, nb),
        in_specs=[row(256), per_b(384), per_b(384), row(LANES), row(LANES), row(256),
                  pl.BlockSpec(e.shape, lambda b, i: (0, 0))],
        out_specs=row(256),
        out_shape=jax.ShapeDtypeStruct((batch * seq, 256), BF16),
        scratch_shapes=[pltpu.VMEM((2 * NSA_HEADS, tq, LANES), F32)],
        compiler_params=_params("parallel", "arbitrary"),
        name="nsa_attn",
    )(nq, nk, nv, sel, gates, ocmp, e)


def _diff_kernel(q_ref, k_ref, v_ref, lam_ref, sg_ref, o_ref, acc_ref, *, lam_init, tq, tk):
    i = pl.program_id(1)
    q = q_ref[...]
    qp4 = _qpos(i, tq, 4)
    lv = lam_ref[...]
    lam = (jnp.exp(jnp.sum(lv[0:1] * lv[1:2], axis=-1, keepdims=True))
           - jnp.exp(jnp.sum(lv[2:3] * lv[3:4], axis=-1, keepdims=True)) + lam_init)
    qs = [jnp.concatenate([_masked(q[:, c * LANES:(c + 1) * LANES], 32 * t, 32 * t + 32) for t in range(4)], axis=0)
          for c in range(2)]
    acc_ref[...] = jnp.zeros_like(acc_ref)

    def step(j, st, diag):
        mask = (_kpos(j, tk) <= qp4) if diag else None
        return tuple(_chain(qs[c], _ktile(k_ref, j, tk, c * LANES, (c + 1) * LANES),
                            _ktile(v_ref, j, tk, c * LANES, (c + 1) * LANES), mask, st[c], acc_ref, c)
                     for c in range(2))

    nfull = (i * tq) // tk
    st = lax.fori_loop(0, nfull, lambda j, s: step(j, s, False), _init_state(2, 4 * tq))
    st = step(nfull, st, True)
    outs = []
    for c in range(2):
        o = acc_ref[c] / st[c][1]
        r = [o[t * tq:(t + 1) * tq] for t in range(4)]
        dd = _pair(r[0] - lam * r[1], r[2] - lam * r[3])
        sq = dd * dd
        lo = _lane_mask(0, 64)
        ms = _pair(jnp.sum(jnp.where(lo, sq, 0.0), axis=-1, keepdims=True),
                   jnp.sum(jnp.where(lo, 0.0, sq), axis=-1, keepdims=True)) * (1.0 / DIFF_V_DIM)
        outs.append(dd * lax.rsqrt(ms + EPS) * sg_ref[...] * (1.0 - lam_init))
    o_ref[...] = jnp.concatenate(outs, axis=1).astype(BF16)


def _diff(dq, dk, dv, lamv, sub_g2, layer, batch, seq):
    tq, tk = TILE_DIFF
    nb = seq // tq
    lam_init = 0.8 - 0.6 * math.exp(-0.3 * layer)
    row = lambda w: pl.BlockSpec((tq, w), lambda b, i: (b * nb + i, 0))
    per_b = lambda w: pl.BlockSpec((seq, w), lambda b, i: (b, 0))
    full = lambda a: pl.BlockSpec(a.shape, lambda b, i: (0,) * a.ndim)
    return pl.pallas_call(
        functools.partial(_diff_kernel, lam_init=lam_init, tq=tq, tk=tk),
        grid=(batch, nb),
        in_specs=[row(256), per_b(256), per_b(256), full(lamv), full(sub_g2)],
        out_specs=row(256),
        out_shape=jax.ShapeDtypeStruct((batch * seq, 256), BF16),
        scratch_shapes=[pltpu.VMEM((2, 4 * tq, LANES), F32)],
        compiler_params=_params("parallel", "arbitrary"),
        name="diff_attn",
    )(dq, dk, dv, lamv, sub_g2)


def _mla_kernel(q_ref, k_ref, v_ref, o_ref, acc_ref, *, tq, tk):
    i = pl.program_id(1)
    qp = _qpos(i, tq)
    nh = MLA_HEADS
    acc_ref[...] = jnp.zeros_like(acc_ref)

    def step(j, st, diag):
        mask = (_kpos(j, tk) <= qp) if diag else None
        return tuple(_chain(q_ref[:, h * LANES:(h + 1) * LANES], _ktile(k_ref, j, tk, h * LANES, (h + 1) * LANES),
                            _ktile(v_ref, j, tk, (h // 2) * LANES, (h // 2 + 1) * LANES), mask, st[h], acc_ref, h)
                     for h in range(nh))

    nfull = (i * tq) // tk
    st = lax.fori_loop(0, nfull, lambda j, s: step(j, s, False), _init_state(nh, tq))
    st = step(nfull, st, True)
    o = [acc_ref[h] / st[h][1] for h in range(nh)]
    o_ref[...] = jnp.concatenate([_pair(o[0], o[1]), _pair(o[2], o[3])], axis=1).astype(BF16)


def _mla(mq, mk, mv, batch, seq):
    tq, tk = TILE_MLA
    nb = seq // tq
    row = lambda w: pl.BlockSpec((tq, w), lambda b, i: (b * nb + i, 0))
    per_b = lambda w: pl.BlockSpec((seq, w), lambda b, i: (b, 0))
    return pl.pallas_call(
        functools.partial(_mla_kernel, tq=tq, tk=tk),
        grid=(batch, nb),
        in_specs=[row(512), per_b(512), per_b(256)],
        out_specs=row(256),
        out_shape=jax.ShapeDtypeStruct((batch * seq, 256), BF16),
        scratch_shapes=[pltpu.VMEM((MLA_HEADS, tq, LANES), F32)],
        compiler_params=_params("parallel", "arbitrary"),
        name="mla_attn",
    )(mq, mk, mv)


def _swa_kernel(q_ref, k_ref, v_ref, sink_ref, o_ref, acc_ref, *, tq, tk):
    i = pl.program_id(1)
    qh = _half_heads(q_ref[...])
    qp = _qpos(i, tq)
    nh = SWA_HEADS
    acc_ref[...] = jnp.zeros_like(acc_ref)

    def step(j, st):
        dist = qp - _kpos(j, tk)
        mask = jnp.where(dist >= 0, dist, SWA_WINDOW) < SWA_WINDOW
        k = _ktile(k_ref, j, tk, 0, LANES)
        v = _ktile(v_ref, j, tk, 0, LANES)
        return tuple(_chain(qh[c], k, v, mask, st[c], acc_ref, c) for c in range(nh))

    lo = jnp.maximum(i * tq - SWA_WINDOW, 0) // tk
    st = lax.fori_loop(lo, ((i + 1) * tq - 1) // tk + 1, step, _init_state(nh, tq))
    sk = sink_ref[...]
    o = []
    for c, h in enumerate((0, 2, 1, 3)):
        m, l = st[c]
        sink = sk[:, h:h + 1] * LOG2E
        m2 = jnp.maximum(m, sink)
        sc = jnp.exp2(m - m2)
        o.append(acc_ref[c] * sc / (l * sc + jnp.exp2(sink - m2)))
    o_ref[...] = jnp.concatenate([_pair(o[0], o[1]), _pair(o[2], o[3])], axis=1).astype(BF16)


def _swa(sq, sk, sv, sinks, batch, seq):
    tq, tk = TILE_SWA
    nb = seq // tq
    row = lambda w: pl.BlockSpec((tq, w), lambda b, i: (b * nb + i, 0))
    per_b = lambda w: pl.BlockSpec((seq, w), lambda b, i: (b, 0))
    return pl.pallas_call(
        functools.partial(_swa_kernel, tq=tq, tk=tk),
        grid=(batch, nb),
        in_specs=[row(256), per_b(128), per_b(128), pl.BlockSpec(sinks.shape, lambda b, i: (0, 0))],
        out_specs=row(256),
        out_shape=jax.ShapeDtypeStruct((batch * seq, 256), BF16),
        scratch_shapes=[pltpu.VMEM((SWA_HEADS, tq, LANES), F32)],
        compiler_params=_params("parallel", "arbitrary"),
        name="swa_attn",
    )(sq, sk, sv, sinks)


def _outproj_kernel(x_ref, mod_ref, ng_ref, oa_ref, ob_ref, oc_ref, od_ref, w_ref, x1_ref, ht_ref):
    acc = _dot(oa_ref[...], w_ref[0:256, :])
    acc = acc + _dot(ob_ref[...], w_ref[256:512, :])
    acc = acc + _dot(oc_ref[...], w_ref[512:768, :])
    acc = acc + _dot(od_ref[...], w_ref[768:1024, :])
    x1 = x_ref[...] + mod_ref[0, 2:3, :] * acc
    x1_ref[...] = x1
    h = _rms(x1, ng_ref[...]) * (1.0 + mod_ref[0, 4:5, :]) + mod_ref[0, 3:4, :]
    ht_ref[...] = h.T.astype(BF16)


def _outproj(x2d, mod_l, norm_g, oa, ob, oc, od, w_out, seq):
    t, d = x2d.shape
    tm = TILE_PROJ_ROWS
    tpb = seq // tm
    row = lambda w: pl.BlockSpec((tm, w), lambda i: (i, 0))
    full = lambda a: pl.BlockSpec(a.shape, lambda i: (0,) * a.ndim)
    return pl.pallas_call(
        _outproj_kernel,
        grid=(t // tm,),
        in_specs=[row(d), pl.BlockSpec((1, 6, d), lambda i: (i // tpb, 0, 0)), full(norm_g),
                  row(256), row(256), row(256), row(256), full(w_out)],
        out_specs=[row(d), pl.BlockSpec((d, tm), lambda i: (0, i))],
        out_shape=[jax.ShapeDtypeStruct((t, d), F32), jax.ShapeDtypeStruct((d, t), BF16)],
        compiler_params=_params("parallel"),
        name="outproj",
    )(x2d, mod_l, norm_g, oa, ob, oc, od, w_out)


_CAND_PIECES = [(0, 0, 8), (0, 8, 8), (1, 0, 8), (2, 0, 5), (3, 0, 4), (4, 0, 3), (5, 0, 2), (6, 0, 2), (7, 0, 2),
                (None, 0, 8)]


_CODE_UNIT = 2.0 ** 114
_TAKEN_BELOW = -(2.0 ** 119)
_INVALID = -(2.0 ** 100)


def _rank_code(r):
    return -(64.0 + r) * _CODE_UNIT


def _top16(s):
    tb = s.shape[1]
    row16 = lax.broadcasted_iota(jnp.int32, (PEER_TOPK, tb), 0)
    vals = jnp.zeros((PEER_TOPK, tb), F32)
    work = s
    for r in range(PEER_TOPK):
        m = jnp.max(work, axis=0, keepdims=True)
        work = jnp.where(work == m, _rank_code(r), work)
        vals = jnp.where(row16 == r, m, vals)
    return vals, work


def _router_head(h, ht, wq_ref, k1_ref, k2_ref, ea_ref, n1_ref, r2_ref, eb_ref):
    tb = ht.shape[1]
    nk = PEER_N_KEYS
    row8 = lax.broadcasted_iota(jnp.int32, (8, tb), 0)
    row16 = lax.broadcasted_iota(jnp.int32, (PEER_TOPK, tb), 0)
    o = pl.multiple_of(h * 2 * nk, 2 * nk)
    q1 = _dot(wq_ref[pl.ds(o, nk), :], ht).astype(BF16)
    q2 = _dot(wq_ref[pl.ds(o + nk, nk), :], ht).astype(BF16)
    s1 = _dot(k1_ref[...], q1)
    s2 = _dot(k2_ref[...], q2)
    v1, code1 = _top16(s1)
    v2, code2 = _top16(s2)
    top = v1[0:1] + v2[0:1]

    def cells(r1, c0):
        return v1[8:16] + v2[0:1] if r1 is None else v1[r1:r1 + 1] + v2[c0:c0 + 8]

    pieces = []
    for r1, c0, valid in _CAND_PIECES:
        p = cells(r1, c0)
        pieces.append(p if valid == 8 else jnp.where(row8 < valid, p, _INVALID))
    for _ in range(PEER_TOPK):
        m = pieces[0]
        for p in pieces[1:]:
            m = jnp.maximum(m, p)
        m = jnp.max(m, axis=0, keepdims=True)
        pieces = [jnp.where(p == m, _rank_code(0), p) for p in pieces]
    counts = jnp.zeros((PEER_TOPK, tb), F32)
    z = jnp.zeros((1, tb), F32)
    for p, (r1, c0, valid) in zip(pieces, _CAND_PIECES):
        taken = p < _TAKEN_BELOW
        if r1 is None:
            counts = counts + jnp.concatenate([jnp.zeros((8, tb), F32), jnp.where(taken, 1.0, 0.0)], axis=0)
        else:
            n = jnp.sum(jnp.where(taken, 1.0, 0.0), axis=0, keepdims=True)
            counts = counts + jnp.where(row16 == r1, n, 0.0)
        z = z + jnp.sum(jnp.where(taken, jnp.exp(cells(r1, c0) - top), 0.0), axis=0, keepdims=True)
    n1 = jnp.zeros((nk, tb), F32)
    for r in range(PEER_TOPK):
        n1 = jnp.where(code1 == _rank_code(r), counts[r:r + 1], n1)
    ea_ref[h] = jnp.exp(s1 - v1[0:1])
    n1_ref[h] = n1
    rank2 = jnp.where(code2 < _TAKEN_BELOW, code2 * (-1.0 / _CODE_UNIT) - 64.0, float(nk))
    r2_ref[h] = rank2.astype(BF16)
    eb_ref[h] = (jnp.exp(s2 - v2[0:1]) / z).astype(BF16)


def _router_kernel(ht_ref, wq_ref, k1_ref, k2_ref, ea_ref, n1_ref, r2_ref, eb_ref):
    ht = ht_ref[...]

    group = 8

    def heads(p, carry):
        for hh in range(group):
            _router_head(group * p + hh, ht, wq_ref, k1_ref, k2_ref, ea_ref, n1_ref, r2_ref, eb_ref)
        return carry

    lax.fori_loop(0, PEER_HEADS // group, heads, 0)


def _router(ht, wq_t, k1, k2):
    d, t = ht.shape
    tb = TILE_ROUTER_TOKENS
    full = lambda a: pl.BlockSpec(a.shape, lambda i: (0,) * a.ndim)
    out = pl.BlockSpec((PEER_HEADS, PEER_N_KEYS, tb), lambda i: (0, 0, i))
    return pl.pallas_call(
        _router_kernel,
        grid=(t // tb,),
        in_specs=[pl.BlockSpec((d, tb), lambda i: (0, i)), full(wq_t), full(k1), full(k2)],
        out_specs=[out, out, out, out],
        out_shape=[jax.ShapeDtypeStruct((PEER_HEADS, PEER_N_KEYS, t), dt) for dt in (F32, F32, BF16, BF16)],
        compiler_params=_params("parallel"),
        name="peer_router",
    )(ht, wq_t, k1, k2)


def _gelu_tanh(x):
    k = 2.0 * math.sqrt(2.0 / math.pi) * math.log2(math.e)
    return x / (1.0 + jnp.exp2(x * (-k - (k * 0.044715) * (x * x))))


def _peer_kernel(ht_ref, u_ref, vt_ref, ea_ref, n1_ref, r2_ref, eb_ref, x_ref, mod_ref, fg_ref, o_ref, acc_ref, wa_ref,
                 wb_ref, *, final, chunk, n_e):
    e = pl.program_id(1)
    nk = PEER_N_KEYS
    n_i1 = u_ref.shape[0] // nk
    sub = PACKED_ROWS
    assert n_i1 == 4

    def step(write_ref, read_ref, base):
        def tokens(c, carry):
            lanes = pl.ds(pl.multiple_of(c * chunk, chunk), chunk)
            if write_ref is not None:
                act = _gelu_tanh(_dot(u_ref[...], ht_ref[:, lanes])).astype(BF16)
            if read_ref is not None:
                acc_ref[:, lanes] += _dot(vt_ref[...], read_ref[:, lanes])
            if write_ref is None:
                return carry
            for j in range(n_i1):
                m = None
                for h in range(PEER_HEADS):
                    row = (h, slice(base + j, base + j + 1), lanes)
                    n_row = jnp.broadcast_to(n1_ref[row], (sub, chunk)).astype(BF16)
                    ea_row = jnp.broadcast_to(ea_ref[row], (sub, chunk)).astype(BF16)
                    r2 = r2_ref[h, :, lanes].reshape(nk // sub, sub, chunk)
                    eb = eb_ref[h, :, lanes].reshape(nk // sub, sub, chunk)
                    term = jnp.where(r2 < n_row[None], eb, jnp.zeros((), BF16)) * ea_row[None]
                    m = term if m is None else m + term
                write_ref[j * nk:(j + 1) * nk, lanes] = m.reshape(nk, chunk) * act[j * nk:(j + 1) * nk]
            return carry

        lax.fori_loop(0, ht_ref.shape[1] // chunk, tokens, 0)

    @pl.when(e == 0)
    def _():
        acc_ref[...] = jnp.zeros_like(acc_ref)
        step(wa_ref, None, 0)

    @pl.when((e > 0) & (e < n_e) & (lax.rem(e, 2) == 0))
    def _():
        step(wa_ref, wb_ref, 0)

    @pl.when((e < n_e) & (lax.rem(e, 2) == 1))
    def _():
        step(wb_ref, wa_ref, n_i1)

    @pl.when(e == n_e)
    def _():
        step(None, wa_ref if n_e % 2 else wb_ref, 0)
        y = x_ref[...] + mod_ref[0, 5:6, :] * acc_ref[...].T
        if final:
            y = _rms(y, fg_ref[...])
        o_ref[...] = y


def _peer(ht, u_bf, vt_bf, ea, n1, r2, eb, x1, mod_l, final_g, seq, final):
    d, t = ht.shape
    n_exp = u_bf.shape[0]
    tb = TILE_PEER_TOKENS if seq % TILE_PEER_TOKENS == 0 else TILE_PEER_CHUNK
    eb_blk = TILE_PEER_EXPERTS
    n_e = n_exp // eb_blk
    n_i1 = eb_blk // PEER_N_KEYS
    tpb = seq // tb
    cur = lambda e: jnp.minimum(e, n_e - 1)
    i1_spec = pl.BlockSpec((PEER_HEADS, 8, tb), lambda i, e: (0, cur(e) // (8 // n_i1), i))
    tok3 = pl.BlockSpec((PEER_HEADS, PEER_N_KEYS, tb), lambda i, e: (0, 0, i))
    return pl.pallas_call(
        functools.partial(_peer_kernel, final=final, chunk=TILE_PEER_CHUNK, n_e=n_e),
        grid=(t // tb, n_e + 1),
        in_specs=[pl.BlockSpec((d, tb), lambda i, e: (0, i)),
                  pl.BlockSpec((eb_blk, d), lambda i, e: (cur(e), 0)),
                  pl.BlockSpec((d, eb_blk), lambda i, e: (0, jnp.maximum(e - 1, 0))),
                  i1_spec, i1_spec, tok3, tok3,
                  pl.BlockSpec((tb, d), lambda i, e: (i, 0)),
                  pl.BlockSpec((1, 6, d), lambda i, e: (i // tpb, 0, 0)),
                  pl.BlockSpec(final_g.shape, lambda i, e: (0, 0))],
        out_specs=pl.BlockSpec((tb, d), lambda i, e: (i, 0)),
        out_shape=jax.ShapeDtypeStruct((t, d), F32),
        scratch_shapes=[pltpu.VMEM((d, tb), F32), pltpu.VMEM((eb_blk, tb), BF16), pltpu.VMEM((eb_blk, tb), BF16)],
        compiler_params=_params("parallel", "arbitrary"),
        name="peer_experts",
    )(ht, u_bf, vt_bf, ea, n1, r2, eb, x1, mod_l, final_g)


def kernel(x, c, ada_w, ada_b, norm_mix_g, norm_ffn_g, w_in, nsa_cmp_pos_k, nsa_cmp_pos_v, nsa_cmp_wk, nsa_cmp_wv, diff_lam_q1, diff_lam_k1, diff_lam_q2, diff_lam_k2, diff_sub_g, mla_q_norm_g, mla_w_uq, mla_kv_norm_g, mla_w_ukv, swa_sinks, w_out, peer_w_q, peer_sub_k1, peer_sub_k2, peer_u, peer_v, final_g):
    batch, seq, d = x.shape
    depth = w_in.shape[0]
    assert seq % 512 == 0
    x2d = x.reshape(batch * seq, d)
    mod = _adaln(c, ada_w, ada_b).reshape(depth, batch, 6, d)
    table = _rope_table(seq)
    mixw = 4 * HEAD_DIM
    swa_rows = 3 * mixw + np.concatenate([h * HEAD_DIM + np.arange(HEAD_DIM) for h in (0, 2, 1, 3)])
    out_rows = jnp.asarray(np.concatenate([np.arange(3 * mixw), swa_rows]), jnp.int32)
    pad128 = lambda v: jnp.pad(v, (0, LANES - v.shape[0])).reshape(1, LANES)
    fg = final_g.reshape(1, d)
    for l in range(depth):
        w_big = _take_cols(w_in[l], _IN_IDX, _IN_SGN)
        wuq = _take_cols(mla_w_uq[l], _UQ_IDX, _UQ_SGN)
        wukv = _take_cols(mla_w_ukv[l], _UKV_IDX, _UKV_SGN)
        (nq, nk, nv, ng, dq, dk, dv, mq, mk, mv, sq, sk, sv) = _inproj(
            x2d, mod[l], norm_mix_g[l].reshape(1, d), w_big, table, wuq, wukv,
            mla_q_norm_g[l].reshape(1, -1), mla_kv_norm_g[l].reshape(1, -1), seq)
        ocmp, sel = _nsa_cmp(nq, nk, nv, nsa_cmp_wk[l], nsa_cmp_wv[l], nsa_cmp_pos_k[l], nsa_cmp_pos_v[l], batch, seq)
        o_a = _nsa(nq, nk, nv, sel, ng, ocmp, batch, seq)
        lamv = jnp.concatenate([pad128(diff_lam_q1[l]), pad128(diff_lam_k1[l]),
                                pad128(diff_lam_q2[l]), pad128(diff_lam_k2[l])], axis=0)
        sub_g2 = jnp.concatenate([diff_sub_g[l], diff_sub_g[l]]).reshape(1, LANES)
        o_b = _diff(dq, dk, dv, lamv, sub_g2, l, batch, seq)
        o_c = _mla(mq, mk, mv, batch, seq)
        o_d = _swa(sq, sk, sv, pad128(swa_sinks[l]), batch, seq)
        w_o = jnp.take(w_out[l], out_rows, axis=0).astype(BF16)
        x1, ht = _outproj(x2d, mod[l], norm_ffn_g[l].reshape(1, d), o_a, o_b, o_c, o_d, w_o, seq)
        ea, n1, r2, eb = _router(ht, peer_w_q[l].T.astype(BF16), peer_sub_k1[l].astype(BF16),
                                 peer_sub_k2[l].astype(BF16))
        x2d = _peer(ht, peer_u[l].astype(BF16), peer_v[l].T.astype(BF16), ea, n1, r2, eb, x1, mod[l], fg,
                    seq, final=(l == depth - 1))
    return x2d.reshape(batch, seq, d)
```

```python
import functools
import math

import numpy as np
import jax
import jax.numpy as jnp
from jax import lax
from jax.experimental import pallas as pl
from jax.experimental.pallas import tpu as pltpu

F32 = jnp.float32
BF16 = jnp.bfloat16

HEAD_DIM = 64
ROPE_THETA = 10000.0
EPS = 1e-6
NEG = -1e30
FORCE = 1e4

NSA_HEADS = 4
NSA_CMP_LEN = 32
NSA_CMP_STRIDE = 16
NSA_SEL_LEN = 64
NSA_TOP_N = 16
NSA_WINDOW = 512

DIFF_HEADS = 4
DIFF_QK_DIM = 32
DIFF_V_DIM = 64

MLA_HEADS = 4
MLA_Q_RANK = 256
MLA_KV_RANK = 128
MLA_NOPE_DIM = 64
MLA_ROPE_DIM = 32
MLA_V_DIM = 64

SWA_HEADS = 4
SWA_KV_HEADS = 2
SWA_WINDOW = 128

PEER_HEADS = 8
PEER_N_KEYS = 128
PEER_TOPK = 16
PEER_QUERY_DIM = 256

LOG2E = math.log2(math.e)
LANES = 128
PACKED_ROWS = 16
VMEM_LIMIT = 56 * 1024 * 1024

TILE_PROJ_ROWS = 256
TILE_OUTPROJ_ROWS = 512
TILE_ADALN_COLS = 1536
TILE_CMP_ROWS = 256
TILE_NSA = (512, 512)
TILE_DIFF = (256, 512)
TILE_MLA = (512, 512)
TILE_SWA = (256, 256)
TILE_ROUTER_TOKENS = 128
TILE_PEER_TOKENS = 1024
TILE_PEER_EXPERTS = 512
TILE_PEER_CHUNK = 512


def _dot(a, b):
    return jnp.dot(a, b, preferred_element_type=F32)


def _dot_nt(a, b):
    return lax.dot_general(a, b, (((1,), (1,)), ((), ())), preferred_element_type=F32)


def _params(*sem):
    return pltpu.CompilerParams(dimension_semantics=sem, vmem_limit_bytes=VMEM_LIMIT)


def _rms(x, g):
    return x * lax.rsqrt(jnp.mean(x * x, axis=-1, keepdims=True) + EPS) * g


def _rot_idx(base, dim):
    half = dim // 2
    idx = np.concatenate([base + half + np.arange(half), base + np.arange(half)])
    sgn = np.concatenate([-np.ones(half), np.ones(half)])
    return idx, sgn


def _in_plan():
    d = HEAD_DIM
    nsa0 = 0
    nsa_cols = NSA_HEADS * d + 6 * d + 3 * NSA_HEADS
    diff0 = nsa0 + nsa_cols
    diff_cols = 2 * DIFF_HEADS * 2 * DIFF_QK_DIM + DIFF_HEADS * DIFF_V_DIM
    mla0 = diff0 + diff_cols
    mla_cols = MLA_Q_RANK + MLA_KV_RANK + MLA_ROPE_DIM
    swa0 = mla0 + mla_cols
    idx, sgn, off = [], [], {}

    groups = {}

    def add(name, i, s=None):
        i = np.asarray(i, np.int64)
        s = np.ones(len(i)) if s is None else np.asarray(s, np.float64)
        pad = (-len(i)) % LANES
        groups[name] = (np.concatenate([i, np.zeros(pad, np.int64)]), np.concatenate([s, np.zeros(pad)]))

    def heads_rot(base, nheads, dim):
        ii, ss = zip(*[_rot_idx(base + h * dim, dim) for h in range(nheads)])
        return np.concatenate(ii), np.concatenate(ss)

    nq = nsa0 + np.arange(NSA_HEADS * d)
    add("nq", nq)
    add("nqr", *heads_rot(nsa0, NSA_HEADS, d))
    kb = nsa0 + NSA_HEADS * d
    kc, vc, ksl, vsl, kw, vw = [kb + j * d for j in range(6)]
    dup = lambda b: np.concatenate([b + np.arange(d), b + np.arange(d)])
    add("nk", np.concatenate([dup(kc), dup(ksl), dup(kw)]))
    kr = [_rot_idx(b, d) for b in (kc, kc, ksl, ksl, kw, kw)]
    add("nkr", np.concatenate([a for a, _ in kr]), np.concatenate([b for _, b in kr]))
    add("nv", np.concatenate([dup(vc), dup(vsl), dup(vw)]))
    add("ng", kb + 6 * d + np.arange(3 * NSA_HEADS))
    nqk = DIFF_HEADS * 2 * DIFF_QK_DIM
    add("dq", diff0 + np.arange(nqk))
    add("dqr", *heads_rot(diff0, 2 * DIFF_HEADS, DIFF_QK_DIM))
    add("dk", diff0 + nqk + np.arange(nqk))
    add("dkr", *heads_rot(diff0 + nqk, 2 * DIFF_HEADS, DIFF_QK_DIM))
    add("dv", diff0 + 2 * nqk + np.arange(DIFF_HEADS * DIFF_V_DIM))
    add("mcq", mla0 + np.arange(MLA_Q_RANK))
    add("mckv", mla0 + MLA_Q_RANK + np.arange(MLA_KV_RANK))
    kr0 = mla0 + MLA_Q_RANK + MLA_KV_RANK
    z64 = np.zeros(MLA_NOPE_DIM, np.int64)
    add("mkr", np.concatenate([z64, kr0 + np.arange(MLA_ROPE_DIM)]),
        np.concatenate([np.zeros(MLA_NOPE_DIM), np.ones(MLA_ROPE_DIM)]))
    ri, rs = _rot_idx(kr0, MLA_ROPE_DIM)
    add("mkrr", np.concatenate([z64, ri]), np.concatenate([np.zeros(MLA_NOPE_DIM), rs]))
    order = [0, 2, 1, 3]
    add("sq", np.concatenate([swa0 + h * d + np.arange(d) for h in order]))
    sr = [_rot_idx(swa0 + h * d, d) for h in order]
    add("sqr", np.concatenate([a for a, _ in sr]), np.concatenate([b for _, b in sr]))
    sk0 = swa0 + SWA_HEADS * d
    add("sk", sk0 + np.arange(SWA_KV_HEADS * d))
    add("skr", *heads_rot(sk0, SWA_KV_HEADS, d))
    add("sv", sk0 + SWA_KV_HEADS * d + np.arange(SWA_KV_HEADS * d))
    order = ["nq", "nqr", "nk", "ng", "nkr", "mckv", "nv", "sk", "dq", "dqr", "dk", "dkr", "dv", "mcq",
             "mkr", "mkrr", "skr", "sv", "sq", "sqr"]
    assert sorted(order) == sorted(groups)
    blk, run, pos = {}, [], 0
    for name in order:
        off[name] = pos
        idx.append(groups[name][0])
        sgn.append(groups[name][1])
        run.append(name)
        pos += len(groups[name][0])
        if pos % (2 * LANES) == 0:
            start = off[run[0]]
            blk.update({n: (start, pos - start) for n in run})
            run = []
    assert not run
    return np.concatenate(idx), np.concatenate(sgn), off, blk


_IN_IDX, _IN_SGN, _OFF, _BLK = _in_plan()


def _mla_plans():
    qd = MLA_NOPE_DIM + MLA_ROPE_DIM
    qi, qs, ri, rs = [], [], [], []
    for h in range(MLA_HEADS):
        b = h * qd
        qi += [b + np.arange(qd), np.zeros(LANES - qd, np.int64)]
        qs += [np.ones(qd), np.zeros(LANES - qd)]
        a, s = _rot_idx(b + MLA_NOPE_DIM, MLA_ROPE_DIM)
        ri += [np.zeros(MLA_NOPE_DIM, np.int64), a, np.zeros(LANES - qd, np.int64)]
        rs += [np.zeros(MLA_NOPE_DIM), s, np.zeros(LANES - qd)]
    kd = MLA_NOPE_DIM + MLA_V_DIM
    ki, ks, vi = [], [], []
    for h in range(MLA_HEADS):
        ki += [h * kd + np.arange(MLA_NOPE_DIM), np.zeros(LANES - MLA_NOPE_DIM, np.int64)]
        ks += [np.ones(MLA_NOPE_DIM), np.zeros(LANES - MLA_NOPE_DIM)]
        vi += [h * kd + MLA_NOPE_DIM + np.arange(MLA_V_DIM)]
    uq_idx = np.concatenate(qi + ri)
    uq_sgn = np.concatenate(qs + rs)
    ukv_idx = np.concatenate(ki + vi)
    ukv_sgn = np.concatenate(ks + [np.ones(MLA_HEADS * MLA_V_DIM)])
    return uq_idx, uq_sgn, ukv_idx, ukv_sgn


_UQ_IDX, _UQ_SGN, _UKV_IDX, _UKV_SGN = _mla_plans()


def _take_cols(w, idx, sgn):
    pieces, start = [], 0
    for i in range(1, len(idx) + 1):
        same_run = (i < len(idx) and sgn[i] == sgn[start]
                    and (sgn[i] == 0.0 or idx[i] == idx[i - 1] + 1))
        if not same_run:
            n, s = i - start, float(sgn[start])
            run = w[:, int(idx[start]):int(idx[start]) + n]
            pieces.append(jnp.zeros((w.shape[0], n), w.dtype) if s == 0.0 else (run if s == 1.0 else -run))
            start = i
    return jnp.concatenate(pieces, axis=1).astype(BF16)


def _rope_table(seq):
    def cs(dim):
        inv = 1.0 / (ROPE_THETA ** (jnp.arange(0, dim, 2, dtype=F32) / dim))
        ang = jnp.arange(seq, dtype=F32)[:, None] * inv[None, :]
        c, s = jnp.cos(ang), jnp.sin(ang)
        return jnp.concatenate([c, c], 1), jnp.concatenate([s, s], 1)
    ch, sh = cs(HEAD_DIM)
    cd, sd = cs(DIFF_QK_DIM)
    cm, sm = cs(MLA_ROPE_DIM)
    one = jnp.ones((seq, MLA_NOPE_DIM), F32)
    z64 = jnp.zeros((seq, MLA_NOPE_DIM), F32)
    z32 = jnp.zeros((seq, LANES - MLA_NOPE_DIM - MLA_ROPE_DIM), F32)
    parts = [jnp.tile(ch, (1, 2)), jnp.tile(sh, (1, 2)), jnp.tile(cd, (1, 4)), jnp.tile(sd, (1, 4)),
             jnp.concatenate([one, cm, z32], 1), jnp.concatenate([z64, sm, z32], 1),
             jnp.concatenate([z64, cm, z32], 1), jnp.concatenate([z64, sm, z32], 1)]
    return jnp.concatenate(parts, 1)


def _adaln_kernel(c_ref, w_ref, b_ref, o_ref):
    c = c_ref[...]
    sc = (c * jax.nn.sigmoid(c)).astype(BF16)
    o_ref[0] = _dot(sc, w_ref[0].astype(BF16)) + b_ref[0]


def _adaln(c, ada_w, ada_b):
    nl, d, n6 = ada_w.shape
    b = c.shape[0]
    tn = TILE_ADALN_COLS
    return pl.pallas_call(
        _adaln_kernel,
        grid=(nl, n6 // tn),
        in_specs=[pl.BlockSpec((b, d), lambda l, j: (0, 0)),
                  pl.BlockSpec((1, d, tn), lambda l, j: (l, 0, j)),
                  pl.BlockSpec((1, 1, tn), lambda l, j: (l, 0, j))],
        out_specs=pl.BlockSpec((1, b, tn), lambda l, j: (l, 0, j)),
        out_shape=jax.ShapeDtypeStruct((nl, b, n6), F32),
        compiler_params=_params("parallel", "parallel"),
        name="adaln",
    )(c, ada_w, ada_b.reshape(nl, 1, n6))


def _inproj_kernel(x_ref, mod_ref, ng_ref, w_ref, tab_ref, wuq_ref, wukv_ref, gq_ref, gkv_ref,
                   nq_ref, nk_ref, nv_ref, ngo_ref, dq_ref, dk_ref, dv_ref,
                   mq_ref, mk_ref, mv_ref, sq_ref, sk_ref, sv_ref):
    x = x_ref[...]
    h = _rms(x, ng_ref[...]) * (1.0 + mod_ref[0, 1:2, :]) + mod_ref[0, 0:1, :]
    hb = h.astype(BF16)

    runs = {}

    def mm(name, width):
        start, size = _BLK[name]
        if start not in runs:
            runs[start] = _dot(hb, w_ref[:, start:start + size])
        o = _OFF[name] - start
        return runs[start][:, o:o + width]

    def tab(j, reps):
        t = tab_ref[:, j * LANES:(j + 1) * LANES]
        return t if reps == 1 else jnp.concatenate([t] * reps, axis=1)

    def rope(name, rname, width, cj, scale=1.0):
        r = mm(name, width) * tab(cj, width // LANES) + mm(rname, width) * tab(cj + 1, width // LANES)
        return r if scale == 1.0 else r * scale

    d = HEAD_DIM
    nq_ref[...] = rope("nq", "nqr", 256, 0, LOG2E * d ** -0.5).astype(BF16)
    nk_ref[...] = rope("nk", "nkr", 384, 0).astype(BF16)
    nv_ref[...] = mm("nv", 384).astype(BF16)
    ngo_ref[...] = jax.nn.sigmoid(mm("ng", LANES))
    dq_ref[...] = rope("dq", "dqr", 256, 2, LOG2E * DIFF_QK_DIM ** -0.5).astype(BF16)
    dk_ref[...] = rope("dk", "dkr", 256, 2).astype(BF16)
    dv_ref[...] = mm("dv", 256).astype(BF16)
    cq = _rms(mm("mcq", MLA_Q_RANK), gq_ref[...]).astype(BF16)
    nh = MLA_HEADS * LANES
    qa = _dot(cq, wuq_ref[:, 0:nh])
    qb = _dot(cq, wuq_ref[:, nh:2 * nh])
    mq = (qa * tab(4, MLA_HEADS) + qb * tab(5, MLA_HEADS)) * (LOG2E * (MLA_NOPE_DIM + MLA_ROPE_DIM) ** -0.5)
    mq_ref[...] = mq.astype(BF16)
    ckv = _rms(mm("mckv", MLA_KV_RANK), gkv_ref[...]).astype(BF16)
    kk = _dot(ckv, wukv_ref[:, 0:nh])
    kr = mm("mkr", LANES) * tab(6, 1) + mm("mkrr", LANES) * tab(7, 1)
    mk_ref[...] = (kk + jnp.concatenate([kr] * MLA_HEADS, axis=1)).astype(BF16)
    mv_ref[...] = _dot(ckv, wukv_ref[:, nh:nh + MLA_HEADS * MLA_V_DIM]).astype(BF16)
    sq_ref[...] = rope("sq", "sqr", 256, 0, LOG2E * d ** -0.5).astype(BF16)
    sk_ref[...] = rope("sk", "skr", 128, 0).astype(BF16)
    sv_ref[...] = mm("sv", 128).astype(BF16)


def _inproj(x2d, mod_l, norm_g, w_big, table, wuq, wukv, gq, gkv, seq):
    t, d = x2d.shape
    tm = TILE_PROJ_ROWS
    tpb = seq // tm
    widths = [256, 384, 384, 128, 256, 256, 256, 512, 512, 256, 256, 128, 128]
    dts = [BF16, BF16, BF16, F32, BF16, BF16, BF16, BF16, BF16, BF16, BF16, BF16, BF16]
    full = lambda a: pl.BlockSpec(a.shape, lambda i: (0,) * a.ndim)
    return pl.pallas_call(
        _inproj_kernel,
        grid=(t // tm,),
        in_specs=[pl.BlockSpec((tm, d), lambda i: (i, 0)),
                  pl.BlockSpec((1, 6, d), lambda i: (i // tpb, 0, 0)),
                  full(norm_g), full(w_big),
                  pl.BlockSpec((tm, table.shape[1]), lambda i: (i % tpb, 0)),
                  full(wuq), full(wukv), full(gq), full(gkv)],
        out_specs=[pl.BlockSpec((tm, w), lambda i: (i, 0)) for w in widths],
        out_shape=[jax.ShapeDtypeStruct((t, w), dt) for w, dt in zip(widths, dts)],
        compiler_params=_params("parallel"),
        name="inproj",
    )(x2d, mod_l, norm_g, w_big, table, wuq, wukv, gq, gkv)


def _lane_mask(lo, hi):
    lane = lax.broadcasted_iota(jnp.int32, (1, LANES), 1)
    return (lane >= lo) & (lane < hi)


def _masked(q, lo, hi):
    return jnp.where(_lane_mask(lo, hi), q, jnp.zeros_like(q))


def _chain(q, k, v, mask, state, acc_ref, c):
    m, l = state
    s = _dot_nt(q, k)
    if mask is not None:
        s = jnp.where(mask, s, NEG)
    m2 = jnp.maximum(m, jnp.max(s, axis=-1, keepdims=True))
    a = jnp.exp2(m - m2)
    p = jnp.exp2(s - m2)
    acc_ref[c] = a * acc_ref[c] + _dot(p.astype(BF16), v)
    return m2, a * l + jnp.sum(p, axis=-1, keepdims=True)


def _init_state(n, rows):
    return tuple((jnp.full((rows, 1), NEG, F32), jnp.zeros((rows, 1), F32)) for _ in range(n))


def _ktile(ref, j, tk, c0, c1):
    return ref[pl.ds(pl.multiple_of(j * tk, tk), tk), c0:c1]


def _qpos(i, tq, reps=1):
    p = i * tq + lax.broadcasted_iota(jnp.int32, (tq, 1), 0)
    return p if reps == 1 else jnp.concatenate([p] * reps, axis=0)


def _kpos(j, tk):
    return j * tk + lax.broadcasted_iota(jnp.int32, (1, tk), 1)


def _half_heads(q):
    return [_masked(q[:, c * LANES:(c + 1) * LANES], 64 * hh, 64 * hh + 64) for c in range(2) for hh in range(2)]


def _pair(lo_val, hi_val):
    return jnp.where(_lane_mask(0, 64), lo_val, hi_val)


def _nsa_cmp_kernel(q_ref, kc_ref, vc_ref, wk_ref, wv_ref, pk_ref, pv_ref, ov_ref, oc_ref, sel_ref, *, top_n, n_sel):
    half = wk_ref.shape[1]

    def compress(x_ref, w_ref, p_ref):
        x = x_ref[...]
        a = _dot(x, w_ref[0])
        b = _dot(x, w_ref[1])
        p = jnp.broadcast_to(p_ref[...], (8, 2 * half)).astype(BF16)
        const = (_dot(p[:, 0:half], w_ref[0]) + _dot(p[:, half:2 * half], w_ref[1]))[0:1]
        return a + jnp.concatenate([b[1:], b[:1]], axis=0) + const

    kcmp = compress(kc_ref, wk_ref, pk_ref).astype(BF16)
    vcmp = compress(vc_ref, wv_ref, pv_ref).astype(BF16)
    ncp = kcmp.shape[0]
    ov = ov_ref[...]
    rb = TILE_CMP_ROWS
    cend = NSA_CMP_STRIDE * lax.broadcasted_iota(jnp.int32, (1, ncp), 1) + (NSA_CMP_LEN - 1)
    lane = lax.broadcasted_iota(jnp.int32, (1, LANES), 1)

    def block(r, carry):
        r0 = pl.multiple_of(r * rb, rb)
        q = q_ref[pl.ds(r0, rb), :]
        tpos = r0 + lax.broadcasted_iota(jnp.int32, (rb, 1), 0)
        vis = cend <= tpos
        psum = jnp.zeros((rb, ncp), F32)
        outs = []
        for c in range(2):
            halves = []
            for hh in range(2):
                qm = _masked(q[:, c * LANES:(c + 1) * LANES], 64 * hh, 64 * hh + 64)
                s = jnp.where(vis, _dot_nt(qm, kcmp), NEG)
                e = jnp.exp2(s - jnp.max(s, axis=-1, keepdims=True))
                p = jnp.where(vis, e / jnp.sum(e, axis=-1, keepdims=True), 0.0)
                psum = psum + p
                halves.append(_dot(p.astype(BF16), vcmp))
            outs.append(_pair(halves[0], halves[1]))
        oc_ref[pl.ds(r0, rb), :] = jnp.concatenate(outs, axis=1)
        hi = psum.astype(BF16)
        lo = (psum - hi.astype(F32)).astype(BF16)
        imp = _dot(hi, ov) + _dot(lo, ov)
        nsp = -(-n_sel // 8) * 8
        imp_t = imp.T[0:nsp]
        blk = lax.broadcasted_iota(jnp.int32, (nsp, 1), 0)
        qblk = (r0 + lax.broadcasted_iota(jnp.int32, (1, rb), 1)) // NSA_SEL_LEN
        allowed = blk <= qblk
        forced = (blk == 0) | (blk == qblk) | (blk == qblk - 1)
        impf = jnp.where(allowed, jnp.where(forced, FORCE, imp_t), NEG)
        rank = jnp.zeros((nsp, rb), F32)
        for j in range(n_sel):
            row = impf[j:j + 1, :]
            rank = rank + jnp.where(blk > j, jnp.where(row >= impf, 1.0, 0.0), jnp.where(row > impf, 1.0, 0.0))
        sel_t = jnp.where(allowed & (rank < top_n), 1.0, 0.0)
        if nsp < LANES:
            sel_t = jnp.concatenate([sel_t, jnp.zeros((LANES - nsp, rb), F32)], axis=0)
        sel_ref[pl.ds(r0, rb), :] = sel_t.T.astype(BF16)
        return carry

    lax.fori_loop(0, q_ref.shape[0] // rb, block, 0)


def _nsa_cmp(nq, nk, nv, wk, wv, pos_k, pos_v, batch, seq):
    d = HEAD_DIM
    nc = seq // NSA_CMP_STRIDE
    ncp = -(-nc // LANES) * LANES
    n_sel = seq // NSA_SEL_LEN
    assert n_sel <= LANES
    top_n = min(NSA_TOP_N, n_sel)

    def seg(a):
        a = a[:, :d].reshape(batch, nc, NSA_CMP_STRIDE * d)
        return jnp.pad(a, ((0, 0), (0, ncp - nc), (0, 0))).reshape(batch * ncp, NSA_CMP_STRIDE * d)

    half = NSA_CMP_STRIDE * d
    dupw = lambda w: jnp.concatenate([w, w], axis=1).reshape(2, half, 2 * d).astype(BF16)
    cpos = NSA_CMP_STRIDE * np.arange(ncp)[:, None] + np.arange(NSA_CMP_LEN)[None, :]
    ovl = np.zeros((ncp, LANES), np.float32)
    for j in range(n_sel):
        ovl[:, j] = (cpos // NSA_SEL_LEN == j).mean(axis=1)
    ovl[nc - 1:, :] = 0.0
    full = lambda a: pl.BlockSpec(a.shape, lambda b: (0,) * a.ndim)
    wk2, wv2 = dupw(wk), dupw(wv)
    pk, pv = pos_k.reshape(1, -1), pos_v.reshape(1, -1)
    ov = jnp.asarray(ovl, BF16)
    return pl.pallas_call(
        functools.partial(_nsa_cmp_kernel, top_n=top_n, n_sel=n_sel),
        grid=(batch,),
        in_specs=[pl.BlockSpec((seq, 256), lambda b: (b, 0)),
                  pl.BlockSpec((ncp, half), lambda b: (b, 0)),
                  pl.BlockSpec((ncp, half), lambda b: (b, 0)),
                  full(wk2), full(wv2), full(pk), full(pv), full(ov)],
        out_specs=[pl.BlockSpec((seq, 256), lambda b: (b, 0)),
                   pl.BlockSpec((seq, LANES), lambda b: (b, 0))],
        out_shape=[jax.ShapeDtypeStruct((batch * seq, 256), F32),
                   jax.ShapeDtypeStruct((batch * seq, LANES), BF16)],
        compiler_params=_params("parallel"),
        name="nsa_cmp",
    )(nq, seg(nk), seg(nv), wk2, wv2, pk, pv, ov)


def _nsa_kernel(q_ref, k_ref, v_ref, sel_ref, g_ref, oc_ref, e_ref, o_ref, acc_ref, *, tq, tk):
    i = pl.program_id(1)
    qh = _half_heads(q_ref[...])
    qp = _qpos(i, tq)
    sel = sel_ref[...]
    nh = NSA_HEADS
    acc_ref[...] = jnp.zeros_like(acc_ref)

    def sel_step(j, st, diag):
        mv = _dot(sel, e_ref[:, pl.ds(pl.multiple_of(j * tk, tk), tk)])
        if diag:
            mv = jnp.where(_kpos(j, tk) <= qp, mv, 0.0)
        mask = mv > 0.5
        k = _ktile(k_ref, j, tk, 128, 256)
        v = _ktile(v_ref, j, tk, 128, 256)
        return tuple(_chain(qh[h], k, v, mask, st[h], acc_ref, h) for h in range(nh))

    nfull = (i * tq) // tk
    st = lax.fori_loop(0, nfull, lambda j, s: sel_step(j, s, False), _init_state(nh, tq))
    st_sel = sel_step(nfull, st, True)

    def win_step(j, st):
        dist = qp - _kpos(j, tk)
        mask = jnp.where(dist >= 0, dist, NSA_WINDOW) < NSA_WINDOW
        k = _ktile(k_ref, j, tk, 256, 384)
        v = _ktile(v_ref, j, tk, 256, 384)
        return tuple(_chain(qh[h], k, v, mask, st[h], acc_ref, nh + h) for h in range(nh))

    wlo = jnp.maximum(i * tq - NSA_WINDOW, 0) // tk
    st_win = lax.fori_loop(wlo, nfull + 1, win_step, _init_state(nh, tq))
    g = g_ref[...]
    oc = oc_ref[...]
    outs = []
    for c in range(2):
        occ = oc[:, c * LANES:(c + 1) * LANES]

        def comb(h):
            o_sel = acc_ref[h] / st_sel[h][1]
            o_win = acc_ref[nh + h] / st_win[h][1]
            return g[:, 3 * h:3 * h + 1] * occ + g[:, 3 * h + 1:3 * h + 2] * o_sel + g[:, 3 * h + 2:3 * h + 3] * o_win

        outs.append(_pair(comb(2 * c), comb(2 * c + 1)))
    o_ref[...] = jnp.concatenate(outs, axis=1).astype(BF16)


def _nsa(nq, nk, nv, sel, gates, ocmp, batch, seq):
    tq, tk = TILE_NSA
    nb = seq // tq
    expand = np.zeros((LANES, seq), np.float32)
    for j in range(seq // NSA_SEL_LEN):
        expand[j, j * NSA_SEL_LEN:(j + 1) * NSA_SEL_LEN] = 1.0
    e = jnp.asarray(expand, BF16)
    row = lambda w: pl.BlockSpec((tq, w), lambda b, i: (b * nb + i, 0))
    per_b = lambda w: pl.BlockSpec((seq, w), lambda b, i: (b, 0))
    return pl.pallas_call(
        functools.partial(_nsa_kernel, tq=tq, tk=tk),
        grid=(batch, nb),
        in_specs=[row(256), per_b(384), per_b(384), row(LANES), row(LANES), row(256),
                  pl.BlockSpec(e.shape, lambda b, i: (0, 0))],
        out_specs=row(256),
        out_shape=jax.ShapeDtypeStruct((batch * seq, 256), BF16),
        scratch_shapes=[pltpu.VMEM((2 * NSA_HEADS, tq, LANES), F32)],
        compiler_params=_params("parallel", "arbitrary"),
        name="nsa_attn",
    )(nq, nk, nv, sel, gates, ocmp, e)


def _diff_kernel(q_ref, k_ref, v_ref, lam_ref, sg_ref, o_ref, acc_ref, *, lam_init, tq, tk):
    i = pl.program_id(1)
    q = q_ref[...]
    qp4 = _qpos(i, tq, 4)
    lv = lam_ref[...]
    lam = (jnp.exp(jnp.sum(lv[0:1] * lv[1:2], axis=-1, keepdims=True))
           - jnp.exp(jnp.sum(lv[2:3] * lv[3:4], axis=-1, keepdims=True)) + lam_init)
    qs = [jnp.concatenate([_masked(q[:, c * LANES:(c + 1) * LANES], 32 * t, 32 * t + 32) for t in range(4)], axis=0)
          for c in range(2)]
    acc_ref[...] = jnp.zeros_like(acc_ref)

    def step(j, st, diag):
        mask = (_kpos(j, tk) <= qp4) if diag else None
        return tuple(_chain(qs[c], _ktile(k_ref, j, tk, c * LANES, (c + 1) * LANES),
                            _ktile(v_ref, j, tk, c * LANES, (c + 1) * LANES), mask, st[c], acc_ref, c)
                     for c in range(2))

    nfull = (i * tq) // tk
    st = lax.fori_loop(0, nfull, lambda j, s: step(j, s, False), _init_state(2, 4 * tq))
    st = step(nfull, st, True)
    outs = []
    for c in range(2):
        o = acc_ref[c] / st[c][1]
        r = [o[t * tq:(t + 1) * tq] for t in range(4)]
        dd = _pair(r[0] - lam * r[1], r[2] - lam * r[3])
        sq = dd * dd
        lo = _lane_mask(0, 64)
        ms = _pair(jnp.sum(jnp.where(lo, sq, 0.0), axis=-1, keepdims=True),
                   jnp.sum(jnp.where(lo, 0.0, sq), axis=-1, keepdims=True)) * (1.0 / DIFF_V_DIM)
        outs.append(dd * lax.rsqrt(ms + EPS) * sg_ref[...] * (1.0 - lam_init))
    o_ref[...] = jnp.concatenate(outs, axis=1).astype(BF16)


def _diff(dq, dk, dv, lamv, sub_g2, layer, batch, seq):
    tq, tk = TILE_DIFF
    nb = seq // tq
    lam_init = 0.8 - 0.6 * math.exp(-0.3 * layer)
    row = lambda w: pl.BlockSpec((tq, w), lambda b, i: (b * nb + i, 0))
    per_b = lambda w: pl.BlockSpec((seq, w), lambda b, i: (b, 0))
    full = lambda a: pl.BlockSpec(a.shape, lambda b, i: (0,) * a.ndim)
    return pl.pallas_call(
        functools.partial(_diff_kernel, lam_init=lam_init, tq=tq, tk=tk),
        grid=(batch, nb),
        in_specs=[row(256), per_b(256), per_b(256), full(lamv), full(sub_g2)],
        out_specs=row(256),
        out_shape=jax.ShapeDtypeStruct((batch * seq, 256), BF16),
        scratch_shapes=[pltpu.VMEM((2, 4 * tq, LANES), F32)],
        compiler_params=_params("parallel", "arbitrary"),
        name="diff_attn",
    )(dq, dk, dv, lamv, sub_g2)


def _mla_kernel(q_ref, k_ref, v_ref, o_ref, acc_ref, *, tq, tk):
    i = pl.program_id(1)
    qp = _qpos(i, tq)
    nh = MLA_HEADS
    acc_ref[...] = jnp.zeros_like(acc_ref)

    def step(j, st, diag):
        mask = (_kpos(j, tk) <= qp) if diag else None
        return tuple(_chain(q_ref[:, h * LANES:(h + 1) * LANES], _ktile(k_ref, j, tk, h * LANES, (h + 1) * LANES),
                            _ktile(v_ref, j, tk, (h // 2) * LANES, (h // 2 + 1) * LANES), mask, st[h], acc_ref, h)
                     for h in range(nh))

    nfull = (i * tq) // tk
    st = lax.fori_loop(0, nfull, lambda j, s: step(j, s, False), _init_state(nh, tq))
    st = step(nfull, st, True)
    o = [acc_ref[h] / st[h][1] for h in range(nh)]
    o_ref[...] = jnp.concatenate([_pair(o[0], o[1]), _pair(o[2], o[3])], axis=1).astype(BF16)


def _mla(mq, mk, mv, batch, seq):
    tq, tk = TILE_MLA
    nb = seq // tq
    row = lambda w: pl.BlockSpec((tq, w), lambda b, i: (b * nb + i, 0))
    per_b = lambda w: pl.BlockSpec((seq, w), lambda b, i: (b, 0))
    return pl.pallas_call(
        functools.partial(_mla_kernel, tq=tq, tk=tk),
        grid=(batch, nb),
        in_specs=[row(512), per_b(512), per_b(256)],
        out_specs=row(256),
        out_shape=jax.ShapeDtypeStruct((batch * seq, 256), BF16),
        scratch_shapes=[pltpu.VMEM((MLA_HEADS, tq, LANES), F32)],
        compiler_params=_params("parallel", "arbitrary"),
        name="mla_attn",
    )(mq, mk, mv)


def _swa_kernel(q_ref, k_ref, v_ref, sink_ref, o_ref, acc_ref, *, tq, tk):
    i = pl.program_id(1)
    qh = _half_heads(q_ref[...])
    qp = _qpos(i, tq)
    nh = SWA_HEADS
    acc_ref[...] = jnp.zeros_like(acc_ref)

    def step(j, st):
        dist = qp - _kpos(j, tk)
        mask = jnp.where(dist >= 0, dist, SWA_WINDOW) < SWA_WINDOW
        k = _ktile(k_ref, j, tk, 0, LANES)
        v = _ktile(v_ref, j, tk, 0, LANES)
        return tuple(_chain(qh[c], k, v, mask, st[c], acc_ref, c) for c in range(nh))

    lo = jnp.maximum(i * tq - SWA_WINDOW, 0) // tk
    st = lax.fori_loop(lo, ((i + 1) * tq - 1) // tk + 1, step, _init_state(nh, tq))
    sk = sink_ref[...]
    o = []
    for c, h in enumerate((0, 2, 1, 3)):
        m, l = st[c]
        sink = sk[:, h:h + 1] * LOG2E
        m2 = jnp.maximum(m, sink)
        sc = jnp.exp2(m - m2)
        o.append(acc_ref[c] * sc / (l * sc + jnp.exp2(sink - m2)))
    o_ref[...] = jnp.concatenate([_pair(o[0], o[1]), _pair(o[2], o[3])], axis=1).astype(BF16)


def _swa(sq, sk, sv, sinks, batch, seq):
    tq, tk = TILE_SWA
    nb = seq // tq
    row = lambda w: pl.BlockSpec((tq, w), lambda b, i: (b * nb + i, 0))
    per_b = lambda w: pl.BlockSpec((seq, w), lambda b, i: (b, 0))
    return pl.pallas_call(
        functools.partial(_swa_kernel, tq=tq, tk=tk),
        grid=(batch, nb),
        in_specs=[row(256), per_b(128), per_b(128), pl.BlockSpec(sinks.shape, lambda b, i: (0, 0))],
        out_specs=row(256),
        out_shape=jax.ShapeDtypeStruct((batch * seq, 256), BF16),
        scratch_shapes=[pltpu.VMEM((SWA_HEADS, tq, LANES), F32)],
        compiler_params=_params("parallel", "arbitrary"),
        name="swa_attn",
    )(sq, sk, sv, sinks)


def _outproj_kernel(x_ref, mod_ref, ng_ref, oa_ref, ob_ref, oc_ref, od_ref, w_ref, x1_ref, ht_ref):
    acc = _dot(oa_ref[...], w_ref[0:256, :])
    acc = acc + _dot(ob_ref[...], w_ref[256:512, :])
    acc = acc + _dot(oc_ref[...], w_ref[512:768, :])
    acc = acc + _dot(od_ref[...], w_ref[768:1024, :])
    x1 = x_ref[...] + mod_ref[0, 2:3, :] * acc
    x1_ref[...] = x1
    h = _rms(x1, ng_ref[...]) * (1.0 + mod_ref[0, 4:5, :]) + mod_ref[0, 3:4, :]
    ht_ref[...] = h.T.astype(BF16)


def _outproj(x2d, mod_l, norm_g, oa, ob, oc, od, w_out, seq):
    t, d = x2d.shape
    tm = TILE_OUTPROJ_ROWS
    tpb = seq // tm
    row = lambda w: pl.BlockSpec((tm, w), lambda i: (i, 0))
    full = lambda a: pl.BlockSpec(a.shape, lambda i: (0,) * a.ndim)
    return pl.pallas_call(
        _outproj_kernel,
        grid=(t // tm,),
        in_specs=[row(d), pl.BlockSpec((1, 6, d), lambda i: (i // tpb, 0, 0)), full(norm_g),
                  row(256), row(256), row(256), row(256), full(w_out)],
        out_specs=[row(d), pl.BlockSpec((d, tm), lambda i: (0, i))],
        out_shape=[jax.ShapeDtypeStruct((t, d), F32), jax.ShapeDtypeStruct((d, t), BF16)],
        compiler_params=_params("parallel"),
        name="outproj",
    )(x2d, mod_l, norm_g, oa, ob, oc, od, w_out)


_CAND_PIECES = [(0, 0, 8), (0, 8, 8), (1, 0, 8), (2, 0, 5), (3, 0, 4), (4, 0, 3), (5, 0, 2), (6, 0, 2), (7, 0, 2),
                (None, 0, 8)]


_CODE_UNIT = 2.0 ** 114
_TAKEN_BELOW = -(2.0 ** 119)
_INVALID = -(2.0 ** 100)


def _rank_code(r):
    return -(64.0 + r) * _CODE_UNIT


def _top16(s):
    tb = s.shape[1]
    row16 = lax.broadcasted_iota(jnp.int32, (PEER_TOPK, tb), 0)
    vals = jnp.zeros((PEER_TOPK, tb), F32)
    work = s
    for r in range(PEER_TOPK):
        m = jnp.max(work, axis=0, keepdims=True)
        work = jnp.where(work == m, _rank_code(r), work)
        vals = jnp.where(row16 == r, m, vals)
    return vals, work


def _router_head(h, ht, wq_ref, k1_ref, k2_ref, ea_ref, n1_ref, r2_ref, eb_ref):
    tb = ht.shape[1]
    nk = PEER_N_KEYS
    row8 = lax.broadcasted_iota(jnp.int32, (8, tb), 0)
    row16 = lax.broadcasted_iota(jnp.int32, (PEER_TOPK, tb), 0)
    o = pl.multiple_of(h * 2 * nk, 2 * nk)
    q1 = _dot(wq_ref[pl.ds(o, nk), :], ht).astype(BF16)
    q2 = _dot(wq_ref[pl.ds(o + nk, nk), :], ht).astype(BF16)
    s1 = _dot(k1_ref[...], q1)
    s2 = _dot(k2_ref[...], q2)
    v1, code1 = _top16(s1)
    v2, code2 = _top16(s2)
    top = v1[0:1] + v2[0:1]

    def cells(r1, c0):
        return v1[8:16] + v2[0:1] if r1 is None else v1[r1:r1 + 1] + v2[c0:c0 + 8]

    pieces = []
    for r1, c0, valid in _CAND_PIECES:
        p = cells(r1, c0)
        pieces.append(p if valid == 8 else jnp.where(row8 < valid, p, _INVALID))
    for _ in range(PEER_TOPK):
        m = pieces[0]
        for p in pieces[1:]:
            m = jnp.maximum(m, p)
        m = jnp.max(m, axis=0, keepdims=True)
        pieces = [jnp.where(p == m, _rank_code(0), p) for p in pieces]
    counts = jnp.zeros((PEER_TOPK, tb), F32)
    z = jnp.zeros((1, tb), F32)
    for p, (r1, c0, valid) in zip(pieces, _CAND_PIECES):
        taken = p < _TAKEN_BELOW
        if r1 is None:
            counts = counts + jnp.concatenate([jnp.zeros((8, tb), F32), jnp.where(taken, 1.0, 0.0)], axis=0)
        else:
            n = jnp.sum(jnp.where(taken, 1.0, 0.0), axis=0, keepdims=True)
            counts = counts + jnp.where(row16 == r1, n, 0.0)
        z = z + jnp.sum(jnp.where(taken, jnp.exp(cells(r1, c0) - top), 0.0), axis=0, keepdims=True)
    n1 = jnp.zeros((nk, tb), F32)
    for r in range(PEER_TOPK):
        n1 = jnp.where(code1 == _rank_code(r), counts[r:r + 1], n1)
    ea_ref[h] = jnp.exp(s1 - v1[0:1])
    n1_ref[h] = n1
    rank2 = jnp.where(code2 < _TAKEN_BELOW, code2 * (-1.0 / _CODE_UNIT) - 64.0, float(nk))
    r2_ref[h] = rank2.astype(BF16)
    eb_ref[h] = (jnp.exp(s2 - v2[0:1]) / z).astype(BF16)


def _router_kernel(ht_ref, wq_ref, k1_ref, k2_ref, ea_ref, n1_ref, r2_ref, eb_ref):
    ht = ht_ref[...]

    group = 8

    def heads(p, carry):
        for hh in range(group):
            _router_head(group * p + hh, ht, wq_ref, k1_ref, k2_ref, ea_ref, n1_ref, r2_ref, eb_ref)
        return carry

    lax.fori_loop(0, PEER_HEADS // group, heads, 0)


def _router(ht, wq_t, k1, k2):
    d, t = ht.shape
    tb = TILE_ROUTER_TOKENS
    full = lambda a: pl.BlockSpec(a.shape, lambda i: (0,) * a.ndim)
    out = pl.BlockSpec((PEER_HEADS, PEER_N_KEYS, tb), lambda i: (0, 0, i))
    return pl.pallas_call(
        _router_kernel,
        grid=(t // tb,),
        in_specs=[pl.BlockSpec((d, tb), lambda i: (0, i)), full(wq_t), full(k1), full(k2)],
        out_specs=[out, out, out, out],
        out_shape=[jax.ShapeDtypeStruct((PEER_HEADS, PEER_N_KEYS, t), dt) for dt in (F32, F32, BF16, BF16)],
        compiler_params=_params("parallel"),
        name="peer_router",
    )(ht, wq_t, k1, k2)


def _gelu_tanh(x):
    k = 2.0 * math.sqrt(2.0 / math.pi) * math.log2(math.e)
    return x / (1.0 + jnp.exp2(x * (-k - (k * 0.044715) * (x * x))))


def _peer_kernel(ht_ref, u_ref, vt_ref, ea_ref, n1_ref, r2_ref, eb_ref, x_ref, mod_ref, fg_ref, o_ref, acc_ref, wa_ref,
                 wb_ref, *, final, chunk, n_e):
    e = pl.program_id(1)
    nk = PEER_N_KEYS
    n_i1 = u_ref.shape[0] // nk
    sub = PACKED_ROWS
    assert n_i1 == 4

    def step(write_ref, read_ref, base):
        def tokens(c, carry):
            lanes = pl.ds(pl.multiple_of(c * chunk, chunk), chunk)
            if write_ref is not None:
                act = _gelu_tanh(_dot(u_ref[...], ht_ref[:, lanes])).astype(BF16)
            if read_ref is not None:
                acc_ref[:, lanes] += _dot(vt_ref[...], read_ref[:, lanes])
            if write_ref is None:
                return carry
            for j in range(n_i1):
                m = None
                for h in range(PEER_HEADS):
                    row = (h, slice(base + j, base + j + 1), lanes)
                    n_row = jnp.broadcast_to(n1_ref[row], (sub, chunk)).astype(BF16)
                    ea_row = jnp.broadcast_to(ea_ref[row], (sub, chunk)).astype(BF16)
                    r2 = r2_ref[h, :, lanes].reshape(nk // sub, sub, chunk)
                    eb = eb_ref[h, :, lanes].reshape(nk // sub, sub, chunk)
                    term = jnp.where(r2 < n_row[None], eb, jnp.zeros((), BF16)) * ea_row[None]
                    m = term if m is None else m + term
                write_ref[j * nk:(j + 1) * nk, lanes] = m.reshape(nk, chunk) * act[j * nk:(j + 1) * nk]
            return carry

        lax.fori_loop(0, ht_ref.shape[1] // chunk, tokens, 0)

    @pl.when(e == 0)
    def _():
        acc_ref[...] = jnp.zeros_like(acc_ref)
        step(wa_ref, None, 0)

    @pl.when((e > 0) & (e < n_e) & (lax.rem(e, 2) == 0))
    def _():
        step(wa_ref, wb_ref, 0)

    @pl.when((e < n_e) & (lax.rem(e, 2) == 1))
    def _():
        step(wb_ref, wa_ref, n_i1)

    @pl.when(e == n_e)
    def _():
        step(None, wa_ref if n_e % 2 else wb_ref, 0)
        y = x_ref[...] + mod_ref[0, 5:6, :] * acc_ref[...].T
        if final:
            y = _rms(y, fg_ref[...])
        o_ref[...] = y


def _peer(ht, u_bf, vt_bf, ea, n1, r2, eb, x1, mod_l, final_g, seq, final):
    d, t = ht.shape
    n_exp = u_bf.shape[0]
    tb = TILE_PEER_TOKENS if seq % TILE_PEER_TOKENS == 0 else TILE_PEER_CHUNK
    eb_blk = TILE_PEER_EXPERTS
    n_e = n_exp // eb_blk
    n_i1 = eb_blk // PEER_N_KEYS
    tpb = seq // tb
    cur = lambda e: jnp.minimum(e, n_e - 1)
    i1_spec = pl.BlockSpec((PEER_HEADS, 8, tb), lambda i, e: (0, cur(e) // (8 // n_i1), i))
    tok3 = pl.BlockSpec((PEER_HEADS, PEER_N_KEYS, tb), lambda i, e: (0, 0, i))
    return pl.pallas_call(
        functools.partial(_peer_kernel, final=final, chunk=TILE_PEER_CHUNK, n_e=n_e),
        grid=(t // tb, n_e + 1),
        in_specs=[pl.BlockSpec((d, tb), lambda i, e: (0, i)),
                  pl.BlockSpec((eb_blk, d), lambda i, e: (cur(e), 0)),
                  pl.BlockSpec((d, eb_blk), lambda i, e: (0, jnp.maximum(e - 1, 0))),
                  i1_spec, i1_spec, tok3, tok3,
                  pl.BlockSpec((tb, d), lambda i, e: (i, 0)),
                  pl.BlockSpec((1, 6, d), lambda i, e: (i // tpb, 0, 0)),
                  pl.BlockSpec(final_g.shape, lambda i, e: (0, 0))],
        out_specs=pl.BlockSpec((tb, d), lambda i, e: (i, 0)),
        out_shape=jax.ShapeDtypeStruct((t, d), F32),
        scratch_shapes=[pltpu.VMEM((d, tb), F32), pltpu.VMEM((eb_blk, tb), BF16), pltpu.VMEM((eb_blk, tb), BF16)],
        compiler_params=_params("parallel", "arbitrary"),
        name="peer_experts",
    )(ht, u_bf, vt_bf, ea, n1, r2, eb, x1, mod_l, final_g)


def kernel(x, c, ada_w, ada_b, norm_mix_g, norm_ffn_g, w_in, nsa_cmp_pos_k, nsa_cmp_pos_v, nsa_cmp_wk, nsa_cmp_wv, diff_lam_q1, diff_lam_k1, diff_lam_q2, diff_lam_k2, diff_sub_g, mla_q_norm_g, mla_w_uq, mla_kv_norm_g, mla_w_ukv, swa_sinks, w_out, peer_w_q, peer_sub_k1, peer_sub_k2, peer_u, peer_v, final_g):
    batch, seq, d = x.shape
    depth = w_in.shape[0]
    assert seq % 512 == 0
    x2d = x.reshape(batch * seq, d)
    mod = _adaln(c, ada_w, ada_b).reshape(depth, batch, 6, d)
    table = _rope_table(seq)
    mixw = 4 * HEAD_DIM
    swa_rows = 3 * mixw + np.concatenate([h * HEAD_DIM + np.arange(HEAD_DIM) for h in (0, 2, 1, 3)])
    out_rows = jnp.asarray(np.concatenate([np.arange(3 * mixw), swa_rows]), jnp.int32)
    pad128 = lambda v: jnp.pad(v, (0, LANES - v.shape[0])).reshape(1, LANES)
    fg = final_g.reshape(1, d)
    for l in range(depth):
        w_big = _take_cols(w_in[l], _IN_IDX, _IN_SGN)
        wuq = _take_cols(mla_w_uq[l], _UQ_IDX, _UQ_SGN)
        wukv = _take_cols(mla_w_ukv[l], _UKV_IDX, _UKV_SGN)
        (nq, nk, nv, ng, dq, dk, dv, mq, mk, mv, sq, sk, sv) = _inproj(
            x2d, mod[l], norm_mix_g[l].reshape(1, d), w_big, table, wuq, wukv,
            mla_q_norm_g[l].reshape(1, -1), mla_kv_norm_g[l].reshape(1, -1), seq)
        ocmp, sel = _nsa_cmp(nq, nk, nv, nsa_cmp_wk[l], nsa_cmp_wv[l], nsa_cmp_pos_k[l], nsa_cmp_pos_v[l], batch, seq)
        o_a = _nsa(nq, nk, nv, sel, ng, ocmp, batch, seq)
        lamv = jnp.concatenate([pad128(diff_lam_q1[l]), pad128(diff_lam_k1[l]),
                                pad128(diff_lam_q2[l]), pad128(diff_lam_k2[l])], axis=0)
        sub_g2 = jnp.concatenate([diff_sub_g[l], diff_sub_g[l]]).reshape(1, LANES)
        o_b = _diff(dq, dk, dv, lamv, sub_g2, l, batch, seq)
        o_c = _mla(mq, mk, mv, batch, seq)
        o_d = _swa(sq, sk, sv, pad128(swa_sinks[l]), batch, seq)
        w_o = jnp.take(w_out[l], out_rows, axis=0).astype(BF16)
        x1, ht = _outproj(x2d, mod[l], norm_ffn_g[l].reshape(1, d), o_a, o_b, o_c, o_d, w_o, seq)
        ea, n1, r2, eb = _router(ht, peer_w_q[l].T.astype(BF16), peer_sub_k1[l].astype(BF16),
                                 peer_sub_k2[l].astype(BF16))
        x2d = _peer(ht, peer_u[l].astype(BF16), peer_v[l].T.astype(BF16), ea, n1, r2, eb, x1, mod[l], fg,
                    seq, final=(l == depth - 1))
    return x2d.reshape(batch, seq, d)
```

```python
import functools
import math

import numpy as np
import jax
import jax.numpy as jnp
from jax import lax
from jax.experimental import pallas as pl
from jax.experimental.pallas import tpu as pltpu

F32 = jnp.float32
BF16 = jnp.bfloat16

HEAD_DIM = 64
ROPE_THETA = 10000.0
EPS = 1e-6
NEG = -1e30
FORCE = 1e4

NSA_HEADS = 4
NSA_CMP_LEN = 32
NSA_CMP_STRIDE = 16
NSA_SEL_LEN = 64
NSA_TOP_N = 16
NSA_WINDOW = 512

DIFF_HEADS = 4
DIFF_QK_DIM = 32
DIFF_V_DIM = 64

MLA_HEADS = 4
MLA_Q_RANK = 256
MLA_KV_RANK = 128
MLA_NOPE_DIM = 64
MLA_ROPE_DIM = 32
MLA_V_DIM = 64

SWA_HEADS = 4
SWA_KV_HEADS = 2
SWA_WINDOW = 128

PEER_HEADS = 8
PEER_N_KEYS = 128
PEER_TOPK = 16
PEER_QUERY_DIM = 256

LOG2E = math.log2(math.e)
LANES = 128
PACKED_ROWS = 16
VMEM_LIMIT = 56 * 1024 * 1024

TILE_PROJ_ROWS = 256
TILE_OUTPROJ_ROWS = 512
TILE_ADALN_COLS = 1536
TILE_CMP_ROWS = 256
TILE_NSA = (512, 512)
TILE_DIFF = (256, 512)
TILE_MLA = (512, 512)
TILE_SWA = (512, 128)
TILE_ROUTER_TOKENS = 128
TILE_PEER_TOKENS = 1024
TILE_PEER_EXPERTS = 512
TILE_PEER_CHUNK = 512


def _dot(a, b):
    return jnp.dot(a, b, preferred_element_type=F32)


def _dot_nt(a, b):
    return lax.dot_general(a, b, (((1,), (1,)), ((), ())), preferred_element_type=F32)


def _params(*sem):
    return pltpu.CompilerParams(dimension_semantics=sem, vmem_limit_bytes=VMEM_LIMIT)


def _rms(x, g):
    return x * lax.rsqrt(jnp.mean(x * x, axis=-1, keepdims=True) + EPS) * g


def _rot_idx(base, dim):
    half = dim // 2
    idx = np.concatenate([base + half + np.arange(half), base + np.arange(half)])
    sgn = np.concatenate([-np.ones(half), np.ones(half)])
    return idx, sgn


def _in_plan():
    d = HEAD_DIM
    nsa0 = 0
    nsa_cols = NSA_HEADS * d + 6 * d + 3 * NSA_HEADS
    diff0 = nsa0 + nsa_cols
    diff_cols = 2 * DIFF_HEADS * 2 * DIFF_QK_DIM + DIFF_HEADS * DIFF_V_DIM
    mla0 = diff0 + diff_cols
    mla_cols = MLA_Q_RANK + MLA_KV_RANK + MLA_ROPE_DIM
    swa0 = mla0 + mla_cols
    idx, sgn, off = [], [], {}

    groups = {}

    def add(name, i, s=None):
        i = np.asarray(i, np.int64)
        s = np.ones(len(i)) if s is None else np.asarray(s, np.float64)
        pad = (-len(i)) % LANES
        groups[name] = (np.concatenate([i, np.zeros(pad, np.int64)]), np.concatenate([s, np.zeros(pad)]))

    def heads_rot(base, nheads, dim):
        ii, ss = zip(*[_rot_idx(base + h * dim, dim) for h in range(nheads)])
        return np.concatenate(ii), np.concatenate(ss)

    nq = nsa0 + np.arange(NSA_HEADS * d)
    add("nq", nq)
    add("nqr", *heads_rot(nsa0, NSA_HEADS, d))
    kb = nsa0 + NSA_HEADS * d
    kc, vc, ksl, vsl, kw, vw = [kb + j * d for j in range(6)]
    dup = lambda b: np.concatenate([b + np.arange(d), b + np.arange(d)])
    add("nk", np.concatenate([dup(kc), dup(ksl), dup(kw)]))
    kr = [_rot_idx(b, d) for b in (kc, kc, ksl, ksl, kw, kw)]
    add("nkr", np.concatenate([a for a, _ in kr]), np.concatenate([b for _, b in kr]))
    add("nv", np.concatenate([dup(vc), dup(vsl), dup(vw)]))
    add("ng", kb + 6 * d + np.arange(3 * NSA_HEADS))
    nqk = DIFF_HEADS * 2 * DIFF_QK_DIM
    add("dq", diff0 + np.arange(nqk))
    add("dqr", *heads_rot(diff0, 2 * DIFF_HEADS, DIFF_QK_DIM))
    add("dk", diff0 + nqk + np.arange(nqk))
    add("dkr", *heads_rot(diff0 + nqk, 2 * DIFF_HEADS, DIFF_QK_DIM))
    add("dv", diff0 + 2 * nqk + np.arange(DIFF_HEADS * DIFF_V_DIM))
    add("mcq", mla0 + np.arange(MLA_Q_RANK))
    add("mckv", mla0 + MLA_Q_RANK + np.arange(MLA_KV_RANK))
    kr0 = mla0 + MLA_Q_RANK + MLA_KV_RANK
    z64 = np.zeros(MLA_NOPE_DIM, np.int64)
    add("mkr", np.concatenate([z64, kr0 + np.arange(MLA_ROPE_DIM)]),
        np.concatenate([np.zeros(MLA_NOPE_DIM), np.ones(MLA_ROPE_DIM)]))
    ri, rs = _rot_idx(kr0, MLA_ROPE_DIM)
    add("mkrr", np.concatenate([z64, ri]), np.concatenate([np.zeros(MLA_NOPE_DIM), rs]))
    order = [0, 2, 1, 3]
    add("sq", np.concatenate([swa0 + h * d + np.arange(d) for h in order]))
    sr = [_rot_idx(swa0 + h * d, d) for h in order]
    add("sqr", np.concatenate([a for a, _ in sr]), np.concatenate([b for _, b in sr]))
    sk0 = swa0 + SWA_HEADS * d
    add("sk", sk0 + np.arange(SWA_KV_HEADS * d))
    add("skr", *heads_rot(sk0, SWA_KV_HEADS, d))
    add("sv", sk0 + SWA_KV_HEADS * d + np.arange(SWA_KV_HEADS * d))
    order = ["nq", "nqr", "nk", "ng", "nkr", "mckv", "nv", "sk", "dq", "dqr", "dk", "dkr", "dv", "mcq",
             "mkr", "mkrr", "skr", "sv", "sq", "sqr"]
    assert sorted(order) == sorted(groups)
    blk, run, pos = {}, [], 0
    for name in order:
        off[name] = pos
        idx.append(groups[name][0])
        sgn.append(groups[name][1])
        run.append(name)
        pos += len(groups[name][0])
        if pos % (2 * LANES) == 0:
            start = off[run[0]]
            blk.update({n: (start, pos - start) for n in run})
            run = []
    assert not run
    return np.concatenate(idx), np.concatenate(sgn), off, blk


_IN_IDX, _IN_SGN, _OFF, _BLK = _in_plan()


def _mla_plans():
    qd = MLA_NOPE_DIM + MLA_ROPE_DIM
    qi, qs, ri, rs = [], [], [], []
    for h in range(MLA_HEADS):
        b = h * qd
        qi += [b + np.arange(qd), np.zeros(LANES - qd, np.int64)]
        qs += [np.ones(qd), np.zeros(LANES - qd)]
        a, s = _rot_idx(b + MLA_NOPE_DIM, MLA_ROPE_DIM)
        ri += [np.zeros(MLA_NOPE_DIM, np.int64), a, np.zeros(LANES - qd, np.int64)]
        rs += [np.zeros(MLA_NOPE_DIM), s, np.zeros(LANES - qd)]
    kd = MLA_NOPE_DIM + MLA_V_DIM
    ki, ks, vi = [], [], []
    for h in range(MLA_HEADS):
        ki += [h * kd + np.arange(MLA_NOPE_DIM), np.zeros(LANES - MLA_NOPE_DIM, np.int64)]
        ks += [np.ones(MLA_NOPE_DIM), np.zeros(LANES - MLA_NOPE_DIM)]
        vi += [h * kd + MLA_NOPE_DIM + np.arange(MLA_V_DIM)]
    uq_idx = np.concatenate(qi + ri)
    uq_sgn = np.concatenate(qs + rs)
    ukv_idx = np.concatenate(ki + vi)
    ukv_sgn = np.concatenate(ks + [np.ones(MLA_HEADS * MLA_V_DIM)])
    return uq_idx, uq_sgn, ukv_idx, ukv_sgn


_UQ_IDX, _UQ_SGN, _UKV_IDX, _UKV_SGN = _mla_plans()


def _take_cols(w, idx, sgn):
    pieces, start = [], 0
    for i in range(1, len(idx) + 1):
        same_run = (i < len(idx) and sgn[i] == sgn[start]
                    and (sgn[i] == 0.0 or idx[i] == idx[i - 1] + 1))
        if not same_run:
            n, s = i - start, float(sgn[start])
            run = w[:, int(idx[start]):int(idx[start]) + n]
            pieces.append(jnp.zeros((w.shape[0], n), w.dtype) if s == 0.0 else (run if s == 1.0 else -run))
            start = i
    return jnp.concatenate(pieces, axis=1).astype(BF16)


def _rope_table(seq):
    def cs(dim):
        inv = 1.0 / (ROPE_THETA ** (jnp.arange(0, dim, 2, dtype=F32) / dim))
        ang = jnp.arange(seq, dtype=F32)[:, None] * inv[None, :]
        c, s = jnp.cos(ang), jnp.sin(ang)
        return jnp.concatenate([c, c], 1), jnp.concatenate([s, s], 1)
    ch, sh = cs(HEAD_DIM)
    cd, sd = cs(DIFF_QK_DIM)
    cm, sm = cs(MLA_ROPE_DIM)
    one = jnp.ones((seq, MLA_NOPE_DIM), F32)
    z64 = jnp.zeros((seq, MLA_NOPE_DIM), F32)
    z32 = jnp.zeros((seq, LANES - MLA_NOPE_DIM - MLA_ROPE_DIM), F32)
    parts = [jnp.tile(ch, (1, 2)), jnp.tile(sh, (1, 2)), jnp.tile(cd, (1, 4)), jnp.tile(sd, (1, 4)),
             jnp.concatenate([one, cm, z32], 1), jnp.concatenate([z64, sm, z32], 1),
             jnp.concatenate([z64, cm, z32], 1), jnp.concatenate([z64, sm, z32], 1)]
    return jnp.concatenate(parts, 1)


def _adaln_kernel(c_ref, w_ref, b_ref, o_ref):
    c = c_ref[...]
    sc = (c * jax.nn.sigmoid(c)).astype(BF16)
    o_ref[0] = _dot(sc, w_ref[0].astype(BF16)) + b_ref[0]


def _adaln(c, ada_w, ada_b):
    nl, d, n6 = ada_w.shape
    b = c.shape[0]
    tn = TILE_ADALN_COLS
    return pl.pallas_call(
        _adaln_kernel,
        grid=(nl, n6 // tn),
        in_specs=[pl.BlockSpec((b, d), lambda l, j: (0, 0)),
                  pl.BlockSpec((1, d, tn), lambda l, j: (l, 0, j)),
                  pl.BlockSpec((1, 1, tn), lambda l, j: (l, 0, j))],
        out_specs=pl.BlockSpec((1, b, tn), lambda l, j: (l, 0, j)),
        out_shape=jax.ShapeDtypeStruct((nl, b, n6), F32),
        compiler_params=_params("parallel", "parallel"),
        name="adaln",
    )(c, ada_w, ada_b.reshape(nl, 1, n6))


def _inproj_kernel(x_ref, mod_ref, ng_ref, w_ref, tab_ref, wuq_ref, wukv_ref, gq_ref, gkv_ref,
                   nq_ref, nk_ref, nv_ref, ngo_ref, dq_ref, dk_ref, dv_ref,
                   mq_ref, mk_ref, mv_ref, sq_ref, sk_ref, sv_ref):
    x = x_ref[...]
    h = _rms(x, ng_ref[...]) * (1.0 + mod_ref[0, 1:2, :]) + mod_ref[0, 0:1, :]
    hb = h.astype(BF16)

    runs = {}

    def mm(name, width):
        start, size = _BLK[name]
        if start not in runs:
            runs[start] = _dot(hb, w_ref[:, start:start + size])
        o = _OFF[name] - start
        return runs[start][:, o:o + width]

    def tab(j, reps):
        t = tab_ref[:, j * LANES:(j + 1) * LANES]
        return t if reps == 1 else jnp.concatenate([t] * reps, axis=1)

    def rope(name, rname, width, cj, scale=1.0):
        r = mm(name, width) * tab(cj, width // LANES) + mm(rname, width) * tab(cj + 1, width // LANES)
        return r if scale == 1.0 else r * scale

    d = HEAD_DIM
    nq_ref[...] = rope("nq", "nqr", 256, 0, LOG2E * d ** -0.5).astype(BF16)
    nk_ref[...] = rope("nk", "nkr", 384, 0).astype(BF16)
    nv_ref[...] = mm("nv", 384).astype(BF16)
    ngo_ref[...] = jax.nn.sigmoid(mm("ng", LANES))
    dq_ref[...] = rope("dq", "dqr", 256, 2, LOG2E * DIFF_QK_DIM ** -0.5).astype(BF16)
    dk_ref[...] = rope("dk", "dkr", 256, 2).astype(BF16)
    dv_ref[...] = mm("dv", 256).astype(BF16)
    cq = _rms(mm("mcq", MLA_Q_RANK), gq_ref[...]).astype(BF16)
    nh = MLA_HEADS * LANES
    qa = _dot(cq, wuq_ref[:, 0:nh])
    qb = _dot(cq, wuq_ref[:, nh:2 * nh])
    mq = (qa * tab(4, MLA_HEADS) + qb * tab(5, MLA_HEADS)) * (LOG2E * (MLA_NOPE_DIM + MLA_ROPE_DIM) ** -0.5)
    mq_ref[...] = mq.astype(BF16)
    ckv = _rms(mm("mckv", MLA_KV_RANK), gkv_ref[...]).astype(BF16)
    kk = _dot(ckv, wukv_ref[:, 0:nh])
    kr = mm("mkr", LANES) * tab(6, 1) + mm("mkrr", LANES) * tab(7, 1)
    mk_ref[...] = (kk + jnp.concatenate([kr] * MLA_HEADS, axis=1)).astype(BF16)
    mv_ref[...] = _dot(ckv, wukv_ref[:, nh:nh + MLA_HEADS * MLA_V_DIM]).astype(BF16)
    sq_ref[...] = rope("sq", "sqr", 256, 0, LOG2E * d ** -0.5).astype(BF16)
    sk_ref[...] = rope("sk", "skr", 128, 0).astype(BF16)
    sv_ref[...] = mm("sv", 128).astype(BF16)


def _inproj(x2d, mod_l, norm_g, w_big, table, wuq, wukv, gq, gkv, seq):
    t, d = x2d.shape
    tm = TILE_PROJ_ROWS
    tpb = seq // tm
    widths = [256, 384, 384, 128, 256, 256, 256, 512, 512, 256, 256, 128, 128]
    dts = [BF16, BF16, BF16, F32, BF16, BF16, BF16, BF16, BF16, BF16, BF16, BF16, BF16]
    full = lambda a: pl.BlockSpec(a.shape, lambda i: (0,) * a.ndim)
    return pl.pallas_call(
        _inproj_kernel,
        grid=(t // tm,),
        in_specs=[pl.BlockSpec((tm, d), lambda i: (i, 0)),
                  pl.BlockSpec((1, 6, d), lambda i: (i // tpb, 0, 0)),
                  full(norm_g), full(w_big),
                  pl.BlockSpec((tm, table.shape[1]), lambda i: (i % tpb, 0)),
                  full(wuq), full(wukv), full(gq), full(gkv)],
        out_specs=[pl.BlockSpec((tm, w), lambda i: (i, 0)) for w in widths],
        out_shape=[jax.ShapeDtypeStruct((t, w), dt) for w, dt in zip(widths, dts)],
        compiler_params=_params("parallel"),
        name="inproj",
    )(x2d, mod_l, norm_g, w_big, table, wuq, wukv, gq, gkv)


def _lane_mask(lo, hi):
    lane = lax.broadcasted_iota(jnp.int32, (1, LANES), 1)
    return (lane >= lo) & (lane < hi)


def _masked(q, lo, hi):
    return jnp.where(_lane_mask(lo, hi), q, jnp.zeros_like(q))


def _chain(q, k, v, mask, state, acc_ref, c):
    m, l = state
    s = _dot_nt(q, k)
    if mask is not None:
        s = jnp.where(mask, s, NEG)
    m2 = jnp.maximum(m, jnp.max(s, axis=-1, keepdims=True))
    a = jnp.exp2(m - m2)
    p = jnp.exp2(s - m2)
    acc_ref[c] = a * acc_ref[c] + _dot(p.astype(BF16), v)
    return m2, a * l + jnp.sum(p, axis=-1, keepdims=True)


def _init_state(n, rows):
    return tuple((jnp.full((rows, 1), NEG, F32), jnp.zeros((rows, 1), F32)) for _ in range(n))


def _ktile(ref, j, tk, c0, c1):
    return ref[pl.ds(pl.multiple_of(j * tk, tk), tk), c0:c1]


def _qpos(i, tq, reps=1):
    p = i * tq + lax.broadcasted_iota(jnp.int32, (tq, 1), 0)
    return p if reps == 1 else jnp.concatenate([p] * reps, axis=0)


def _kpos(j, tk):
    return j * tk + lax.broadcasted_iota(jnp.int32, (1, tk), 1)


def _half_heads(q):
    return [_masked(q[:, c * LANES:(c + 1) * LANES], 64 * hh, 64 * hh + 64) for c in range(2) for hh in range(2)]


def _pair(lo_val, hi_val):
    return jnp.where(_lane_mask(0, 64), lo_val, hi_val)


def _nsa_cmp_kernel(q_ref, kc_ref, vc_ref, wk_ref, wv_ref, pk_ref, pv_ref, ov_ref, oc_ref, sel_ref, *, top_n, n_sel):
    half = wk_ref.shape[1]

    def compress(x_ref, w_ref, p_ref):
        x = x_ref[...]
        a = _dot(x, w_ref[0])
        b = _dot(x, w_ref[1])
        p = jnp.broadcast_to(p_ref[...], (8, 2 * half)).astype(BF16)
        const = (_dot(p[:, 0:half], w_ref[0]) + _dot(p[:, half:2 * half], w_ref[1]))[0:1]
        return a + jnp.concatenate([b[1:], b[:1]], axis=0) + const

    kcmp = compress(kc_ref, wk_ref, pk_ref).astype(BF16)
    vcmp = compress(vc_ref, wv_ref, pv_ref).astype(BF16)
    ncp = kcmp.shape[0]
    ov = ov_ref[...]
    rb = TILE_CMP_ROWS
    cend = NSA_CMP_STRIDE * lax.broadcasted_iota(jnp.int32, (1, ncp), 1) + (NSA_CMP_LEN - 1)
    lane = lax.broadcasted_iota(jnp.int32, (1, LANES), 1)

    def block(r, carry):
        r0 = pl.multiple_of(r * rb, rb)
        q = q_ref[pl.ds(r0, rb), :]
        tpos = r0 + lax.broadcasted_iota(jnp.int32, (rb, 1), 0)
        vis = cend <= tpos
        psum = jnp.zeros((rb, ncp), F32)
        outs = []
        for c in range(2):
            halves = []
            for hh in range(2):
                qm = _masked(q[:, c * LANES:(c + 1) * LANES], 64 * hh, 64 * hh + 64)
                s = jnp.where(vis, _dot_nt(qm, kcmp), NEG)
                e = jnp.exp2(s - jnp.max(s, axis=-1, keepdims=True))
                p = jnp.where(vis, e / jnp.sum(e, axis=-1, keepdims=True), 0.0)
                psum = psum + p
                halves.append(_dot(p.astype(BF16), vcmp))
            outs.append(_pair(halves[0], halves[1]))
        oc_ref[pl.ds(r0, rb), :] = jnp.concatenate(outs, axis=1)
        hi = psum.astype(BF16)
        lo = (psum - hi.astype(F32)).astype(BF16)
        imp = _dot(hi, ov) + _dot(lo, ov)
        nsp = -(-n_sel // 8) * 8
        imp_t = imp.T[0:nsp]
        blk = lax.broadcasted_iota(jnp.int32, (nsp, 1), 0)
        qblk = (r0 + lax.broadcasted_iota(jnp.int32, (1, rb), 1)) // NSA_SEL_LEN
        allowed = blk <= qblk
        forced = (blk == 0) | (blk == qblk) | (blk == qblk - 1)
        impf = jnp.where(allowed, jnp.where(forced, FORCE, imp_t), NEG)
        rank = jnp.zeros((nsp, rb), F32)
        for j in range(n_sel):
            row = impf[j:j + 1, :]
            rank = rank + jnp.where(blk > j, jnp.where(row >= impf, 1.0, 0.0), jnp.where(row > impf, 1.0, 0.0))
        sel_t = jnp.where(allowed & (rank < top_n), 1.0, 0.0)
        if nsp < LANES:
            sel_t = jnp.concatenate([sel_t, jnp.zeros((LANES - nsp, rb), F32)], axis=0)
        sel_ref[pl.ds(r0, rb), :] = sel_t.T.astype(BF16)
        return carry

    lax.fori_loop(0, q_ref.shape[0] // rb, block, 0)


def _nsa_cmp(nq, nk, nv, wk, wv, pos_k, pos_v, batch, seq):
    d = HEAD_DIM
    nc = seq // NSA_CMP_STRIDE
    ncp = -(-nc // LANES) * LANES
    n_sel = seq // NSA_SEL_LEN
    assert n_sel <= LANES
    top_n = min(NSA_TOP_N, n_sel)

    def seg(a):
        a = a[:, :d].reshape(batch, nc, NSA_CMP_STRIDE * d)
        return jnp.pad(a, ((0, 0), (0, ncp - nc), (0, 0))).reshape(batch * ncp, NSA_CMP_STRIDE * d)

    half = NSA_CMP_STRIDE * d
    dupw = lambda w: jnp.concatenate([w, w], axis=1).reshape(2, half, 2 * d).astype(BF16)
    cpos = NSA_CMP_STRIDE * np.arange(ncp)[:, None] + np.arange(NSA_CMP_LEN)[None, :]
    ovl = np.zeros((ncp, LANES), np.float32)
    for j in range(n_sel):
        ovl[:, j] = (cpos // NSA_SEL_LEN == j).mean(axis=1)
    ovl[nc - 1:, :] = 0.0
    full = lambda a: pl.BlockSpec(a.shape, lambda b: (0,) * a.ndim)
    wk2, wv2 = dupw(wk), dupw(wv)
    pk, pv = pos_k.reshape(1, -1), pos_v.reshape(1, -1)
    ov = jnp.asarray(ovl, BF16)
    return pl.pallas_call(
        functools.partial(_nsa_cmp_kernel, top_n=top_n, n_sel=n_sel),
        grid=(batch,),
        in_specs=[pl.BlockSpec((seq, 256), lambda b: (b, 0)),
                  pl.BlockSpec((ncp, half), lambda b: (b, 0)),
                  pl.BlockSpec((ncp, half), lambda b: (b, 0)),
                  full(wk2), full(wv2), full(pk), full(pv), full(ov)],
        out_specs=[pl.BlockSpec((seq, 256), lambda b: (b, 0)),
                   pl.BlockSpec((seq, LANES), lambda b: (b, 0))],
        out_shape=[jax.ShapeDtypeStruct((batch * seq, 256), F32),
                   jax.ShapeDtypeStruct((batch * seq, LANES), BF16)],
        compiler_params=_params("parallel"),
        name="nsa_cmp",
    )(nq, seg(nk), seg(nv), wk2, wv2, pk, pv, ov)


def _nsa_kernel(q_ref, k_ref, v_ref, sel_ref, g_ref, oc_ref, e_ref, o_ref, acc_ref, *, tq, tk):
    i = pl.program_id(1)
    qh = _half_heads(q_ref[...])
    qp = _qpos(i, tq)
    sel = sel_ref[...]
    nh = NSA_HEADS
    acc_ref[...] = jnp.zeros_like(acc_ref)

    def sel_step(j, st, diag):
        mv = _dot(sel, e_ref[:, pl.ds(pl.multiple_of(j * tk, tk), tk)])
        if diag:
            mv = jnp.where(_kpos(j, tk) <= qp, mv, 0.0)
        mask = mv > 0.5
        k = _ktile(k_ref, j, tk, 128, 256)
        v = _ktile(v_ref, j, tk, 128, 256)
        return tuple(_chain(qh[h], k, v, mask, st[h], acc_ref, h) for h in range(nh))

    nfull = (i * tq) // tk
    st = lax.fori_loop(0, nfull, lambda j, s: sel_step(j, s, False), _init_state(nh, tq))
    st_sel = sel_step(nfull, st, True)

    def win_step(j, st):
        dist = qp - _kpos(j, tk)
        mask = jnp.where(dist >= 0, dist, NSA_WINDOW) < NSA_WINDOW
        k = _ktile(k_ref, j, tk, 256, 384)
        v = _ktile(v_ref, j, tk, 256, 384)
        return tuple(_chain(qh[h], k, v, mask, st[h], acc_ref, nh + h) for h in range(nh))

    wlo = jnp.maximum(i * tq - NSA_WINDOW, 0) // tk
    st_win = lax.fori_loop(wlo, nfull + 1, win_step, _init_state(nh, tq))
    g = g_ref[...]
    oc = oc_ref[...]
    outs = []
    for c in range(2):
        occ = oc[:, c * LANES:(c + 1) * LANES]

        def comb(h):
            o_sel = acc_ref[h] / st_sel[h][1]
            o_win = acc_ref[nh + h] / st_win[h][1]
            return g[:, 3 * h:3 * h + 1] * occ + g[:, 3 * h + 1:3 * h + 2] * o_sel + g[:, 3 * h + 2:3 * h + 3] * o_win

        outs.append(_pair(comb(2 * c), comb(2 * c + 1)))
    o_ref[...] = jnp.concatenate(outs, axis=1).astype(BF16)


def _nsa(nq, nk, nv, sel, gates, ocmp, batch, seq):
    tq, tk = TILE_NSA
    nb = seq // tq
    expand = np.zeros((LANES, seq), np.float32)
    for j in range(seq // NSA_SEL_LEN):
        expand[j, j * NSA_SEL_LEN:(j + 1) * NSA_SEL_LEN] = 1.0
    e = jnp.asarray(expand, BF16)
    row = lambda w: pl.BlockSpec((tq, w), lambda b, i: (b * nb + i, 0))
    per_b = lambda w: pl.BlockSpec((seq, w), lambda b, i: (b, 0))
    return pl.pallas_call(
        functools.partial(_nsa_kernel, tq=tq, tk=tk),
        grid=(batch, nb),
        in_specs=[row(256), per_b(384), per_b(384), row(LANES), row(LANES), row(256),
                  pl.BlockSpec(e.shape, lambda b, i: (0, 0))],
        out_specs=row(256),
        out_shape=jax.ShapeDtypeStruct((batch * seq, 256), BF16),
        scratch_shapes=[pltpu.VMEM((2 * NSA_HEADS, tq, LANES), F32)],
        compiler_params=_params("parallel", "arbitrary"),
        name="nsa_attn",
    )(nq, nk, nv, sel, gates, ocmp, e)


def _diff_kernel(q_ref, k_ref, v_ref, lam_ref, sg_ref, o_ref, acc_ref, *, lam_init, tq, tk):
    i = pl.program_id(1)
    q = q_ref[...]
    qp4 = _qpos(i, tq, 4)
    lv = lam_ref[...]
    lam = (jnp.exp(jnp.sum(lv[0:1] * lv[1:2], axis=-1, keepdims=True))
           - jnp.exp(jnp.sum(lv[2:3] * lv[3:4], axis=-1, keepdims=True)) + lam_init)
    qs = [jnp.concatenate([_masked(q[:, c * LANES:(c + 1) * LANES], 32 * t, 32 * t + 32) for t in range(4)], axis=0)
          for c in range(2)]
    acc_ref[...] = jnp.zeros_like(acc_ref)

    def step(j, st, diag):
        mask = (_kpos(j, tk) <= qp4) if diag else None
        return tuple(_chain(qs[c], _ktile(k_ref, j, tk, c * LANES, (c + 1) * LANES),
                            _ktile(v_ref, j, tk, c * LANES, (c + 1) * LANES), mask, st[c], acc_ref, c)
                     for c in range(2))

    nfull = (i * tq) // tk
    st = lax.fori_loop(0, nfull, lambda j, s: step(j, s, False), _init_state(2, 4 * tq))
    st = step(nfull, st, True)
    outs = []
    for c in range(2):
        o = acc_ref[c] / st[c][1]
        r = [o[t * tq:(t + 1) * tq] for t in range(4)]
        dd = _pair(r[0] - lam * r[1], r[2] - lam * r[3])
        sq = dd * dd
        lo = _lane_mask(0, 64)
        ms = _pair(jnp.sum(jnp.where(lo, sq, 0.0), axis=-1, keepdims=True),
                   jnp.sum(jnp.where(lo, 0.0, sq), axis=-1, keepdims=True)) * (1.0 / DIFF_V_DIM)
        outs.append(dd * lax.rsqrt(ms + EPS) * sg_ref[...] * (1.0 - lam_init))
    o_ref[...] = jnp.concatenate(outs, axis=1).astype(BF16)


def _diff(dq, dk, dv, lamv, sub_g2, layer, batch, seq):
    tq, tk = TILE_DIFF
    nb = seq // tq
    lam_init = 0.8 - 0.6 * math.exp(-0.3 * layer)
    row = lambda w: pl.BlockSpec((tq, w), lambda b, i: (b * nb + i, 0))
    per_b = lambda w: pl.BlockSpec((seq, w), lambda b, i: (b, 0))
    full = lambda a: pl.BlockSpec(a.shape, lambda b, i: (0,) * a.ndim)
    return pl.pallas_call(
        functools.partial(_diff_kernel, lam_init=lam_init, tq=tq, tk=tk),
        grid=(batch, nb),
        in_specs=[row(256), per_b(256), per_b(256), full(lamv), full(sub_g2)],
        out_specs=row(256),
        out_shape=jax.ShapeDtypeStruct((batch * seq, 256), BF16),
        scratch_shapes=[pltpu.VMEM((2, 4 * tq, LANES), F32)],
        compiler_params=_params("parallel", "arbitrary"),
        name="diff_attn",
    )(dq, dk, dv, lamv, sub_g2)


def _mla_kernel(q_ref, k_ref, v_ref, o_ref, acc_ref, *, tq, tk):
    i = pl.program_id(1)
    qp = _qpos(i, tq)
    nh = MLA_HEADS
    acc_ref[...] = jnp.zeros_like(acc_ref)

    def step(j, st, diag):
        mask = (_kpos(j, tk) <= qp) if diag else None
        return tuple(_chain(q_ref[:, h * LANES:(h + 1) * LANES], _ktile(k_ref, j, tk, h * LANES, (h + 1) * LANES),
                            _ktile(v_ref, j, tk, (h // 2) * LANES, (h // 2 + 1) * LANES), mask, st[h], acc_ref, h)
                     for h in range(nh))

    nfull = (i * tq) // tk
    st = lax.fori_loop(0, nfull, lambda j, s: step(j, s, False), _init_state(nh, tq))
    st = step(nfull, st, True)
    o = [acc_ref[h] / st[h][1] for h in range(nh)]
    o_ref[...] = jnp.concatenate([_pair(o[0], o[1]), _pair(o[2], o[3])], axis=1).astype(BF16)


def _mla(mq, mk, mv, batch, seq):
    tq, tk = TILE_MLA
    nb = seq // tq
    row = lambda w: pl.BlockSpec((tq, w), lambda b, i: (b * nb + i, 0))
    per_b = lambda w: pl.BlockSpec((seq, w), lambda b, i: (b, 0))
    return pl.pallas_call(
        functools.partial(_mla_kernel, tq=tq, tk=tk),
        grid=(batch, nb),
        in_specs=[row(512), per_b(512), per_b(256)],
        out_specs=row(256),
        out_shape=jax.ShapeDtypeStruct((batch * seq, 256), BF16),
        scratch_shapes=[pltpu.VMEM((MLA_HEADS, tq, LANES), F32)],
        compiler_params=_params("parallel", "arbitrary"),
        name="mla_attn",
    )(mq, mk, mv)


def _swa_kernel(q_ref, k_ref, v_ref, sink_ref, o_ref, *, tq, ts):
    i = pl.program_id(1)
    sk = sink_ref[...] * LOG2E
    for s in range(tq // ts):
        g = i * (tq // ts) + s
        k0 = pl.multiple_of(jnp.maximum(g - 1, 0) * ts, ts)
        k = k_ref[pl.ds(k0, 2 * ts), :]
        v = v_ref[pl.ds(k0, 2 * ts), :]
        dist = (g * ts + lax.broadcasted_iota(jnp.int32, (ts, 1), 0)) - (
            k0 + lax.broadcasted_iota(jnp.int32, (1, 2 * ts), 1))
        mask = jnp.where(dist >= 0, dist, SWA_WINDOW) < SWA_WINDOW
        o = []
        for c, h in enumerate((0, 2, 1, 3)):
            q = _half_heads(q_ref[s * ts:(s + 1) * ts, :])[c]
            sc = jnp.where(mask, _dot_nt(q, k), NEG)
            m = jnp.maximum(jnp.max(sc, axis=-1, keepdims=True), sk[:, h:h + 1])
            p = jnp.exp2(sc - m)
            l = jnp.sum(p, axis=-1, keepdims=True) + jnp.exp2(sk[:, h:h + 1] - m)
            o.append(_dot(p.astype(BF16), v) / l)
        o_ref[s * ts:(s + 1) * ts, :] = jnp.concatenate([_pair(o[0], o[1]), _pair(o[2], o[3])], axis=1).astype(BF16)


def _swa(sq, sk, sv, sinks, batch, seq):
    tq, ts = TILE_SWA
    assert ts == SWA_WINDOW and seq >= 2 * ts
    nb = seq // tq
    row = lambda w: pl.BlockSpec((tq, w), lambda b, i: (b * nb + i, 0))
    per_b = lambda w: pl.BlockSpec((seq, w), lambda b, i: (b, 0))
    return pl.pallas_call(
        functools.partial(_swa_kernel, tq=tq, ts=ts),
        grid=(batch, nb),
        in_specs=[row(256), per_b(128), per_b(128), pl.BlockSpec(sinks.shape, lambda b, i: (0, 0))],
        out_specs=row(256),
        out_shape=jax.ShapeDtypeStruct((batch * seq, 256), BF16),
        compiler_params=_params("parallel", "arbitrary"),
        name="swa_attn",
    )(sq, sk, sv, sinks)


def _outproj_kernel(x_ref, mod_ref, ng_ref, oa_ref, ob_ref, oc_ref, od_ref, w_ref, x1_ref, ht_ref):
    acc = _dot(oa_ref[...], w_ref[0:256, :])
    acc = acc + _dot(ob_ref[...], w_ref[256:512, :])
    acc = acc + _dot(oc_ref[...], w_ref[512:768, :])
    acc = acc + _dot(od_ref[...], w_ref[768:1024, :])
    x1 = x_ref[...] + mod_ref[0, 2:3, :] * acc
    x1_ref[...] = x1
    h = _rms(x1, ng_ref[...]) * (1.0 + mod_ref[0, 4:5, :]) + mod_ref[0, 3:4, :]
    ht_ref[...] = h.T.astype(BF16)


def _outproj(x2d, mod_l, norm_g, oa, ob, oc, od, w_out, seq):
    t, d = x2d.shape
    tm = TILE_OUTPROJ_ROWS
    tpb = seq // tm
    row = lambda w: pl.BlockSpec((tm, w), lambda i: (i, 0))
    full = lambda a: pl.BlockSpec(a.shape, lambda i: (0,) * a.ndim)
    return pl.pallas_call(
        _outproj_kernel,
        grid=(t // tm,),
        in_specs=[row(d), pl.BlockSpec((1, 6, d), lambda i: (i // tpb, 0, 0)), full(norm_g),
                  row(256), row(256), row(256), row(256), full(w_out)],
        out_specs=[row(d), pl.BlockSpec((d, tm), lambda i: (0, i))],
        out_shape=[jax.ShapeDtypeStruct((t, d), F32), jax.ShapeDtypeStruct((d, t), BF16)],
        compiler_params=_params("parallel"),
        name="outproj",
    )(x2d, mod_l, norm_g, oa, ob, oc, od, w_out)


_CAND_PIECES = [(0, 0, 8), (0, 8, 8), (1, 0, 8), (2, 0, 5), (3, 0, 4), (4, 0, 3), (5, 0, 2), (6, 0, 2), (7, 0, 2),
                (None, 0, 8)]


_CODE_UNIT = 2.0 ** 114
_TAKEN_BELOW = -(2.0 ** 119)
_INVALID = -(2.0 ** 100)


def _rank_code(r):
    return -(64.0 + r) * _CODE_UNIT


def _top16(s):
    tb = s.shape[1]
    row16 = lax.broadcasted_iota(jnp.int32, (PEER_TOPK, tb), 0)
    vals = jnp.zeros((PEER_TOPK, tb), F32)
    work = s
    for r in range(PEER_TOPK):
        m = jnp.max(work, axis=0, keepdims=True)
        work = jnp.where(work == m, _rank_code(r), work)
        vals = jnp.where(row16 == r, m, vals)
    return vals, work


def _router_head(h, ht, wq_ref, k1_ref, k2_ref, ea_ref, n1_ref, r2_ref, eb_ref):
    tb = ht.shape[1]
    nk = PEER_N_KEYS
    row8 = lax.broadcasted_iota(jnp.int32, (8, tb), 0)
    row16 = lax.broadcasted_iota(jnp.int32, (PEER_TOPK, tb), 0)
    o = pl.multiple_of(h * 2 * nk, 2 * nk)
    q1 = _dot(wq_ref[pl.ds(o, nk), :], ht).astype(BF16)
    q2 = _dot(wq_ref[pl.ds(o + nk, nk), :], ht).astype(BF16)
    s1 = _dot(k1_ref[...], q1)
    s2 = _dot(k2_ref[...], q2)
    v1, code1 = _top16(s1)
    v2, code2 = _top16(s2)
    top = v1[0:1] + v2[0:1]

    def cells(r1, c0):
        return v1[8:16] + v2[0:1] if r1 is None else v1[r1:r1 + 1] + v2[c0:c0 + 8]

    pieces = []
    for r1, c0, valid in _CAND_PIECES:
        p = cells(r1, c0)
        pieces.append(p if valid == 8 else jnp.where(row8 < valid, p, _INVALID))
    for _ in range(PEER_TOPK):
        m = pieces[0]
        for p in pieces[1:]:
            m = jnp.maximum(m, p)
        m = jnp.max(m, axis=0, keepdims=True)
        pieces = [jnp.where(p == m, _rank_code(0), p) for p in pieces]
    counts = jnp.zeros((PEER_TOPK, tb), F32)
    z = jnp.zeros((1, tb), F32)
    for p, (r1, c0, valid) in zip(pieces, _CAND_PIECES):
        taken = p < _TAKEN_BELOW
        if r1 is None:
            counts = counts + jnp.concatenate([jnp.zeros((8, tb), F32), jnp.where(taken, 1.0, 0.0)], axis=0)
        else:
            n = jnp.sum(jnp.where(taken, 1.0, 0.0), axis=0, keepdims=True)
            counts = counts + jnp.where(row16 == r1, n, 0.0)
        z = z + jnp.sum(jnp.where(taken, jnp.exp(cells(r1, c0) - top), 0.0), axis=0, keepdims=True)
    n1 = jnp.zeros((nk, tb), F32)
    for r in range(PEER_TOPK):
        n1 = jnp.where(code1 == _rank_code(r), counts[r:r + 1], n1)
    ea_ref[h] = jnp.exp(s1 - v1[0:1])
    n1_ref[h] = n1
    rank2 = jnp.where(code2 < _TAKEN_BELOW, code2 * (-1.0 / _CODE_UNIT) - 64.0, float(nk))
    r2_ref[h] = rank2.astype(BF16)
    eb_ref[h] = (jnp.exp(s2 - v2[0:1]) / z).astype(BF16)


def _router_kernel(ht_ref, wq_ref, k1_ref, k2_ref, ea_ref, n1_ref, r2_ref, eb_ref):
    ht = ht_ref[...]

    group = 8

    def heads(p, carry):
        for hh in range(group):
            _router_head(group * p + hh, ht, wq_ref, k1_ref, k2_ref, ea_ref, n1_ref, r2_ref, eb_ref)
        return carry

    lax.fori_loop(0, PEER_HEADS // group, heads, 0)


def _router(ht, wq_t, k1, k2):
    d, t = ht.shape
    tb = TILE_ROUTER_TOKENS
    full = lambda a: pl.BlockSpec(a.shape, lambda i: (0,) * a.ndim)
    out = pl.BlockSpec((PEER_HEADS, PEER_N_KEYS, tb), lambda i: (0, 0, i))
    return pl.pallas_call(
        _router_kernel,
        grid=(t // tb,),
        in_specs=[pl.BlockSpec((d, tb), lambda i: (0, i)), full(wq_t), full(k1), full(k2)],
        out_specs=[out, out, out, out],
        out_shape=[jax.ShapeDtypeStruct((PEER_HEADS, PEER_N_KEYS, t), dt) for dt in (F32, F32, BF16, BF16)],
        compiler_params=_params("parallel"),
        name="peer_router",
    )(ht, wq_t, k1, k2)


def _gelu_tanh(x):
    k = 2.0 * math.sqrt(2.0 / math.pi) * math.log2(math.e)
    return x / (1.0 + jnp.exp2(x * (-k - (k * 0.044715) * (x * x))))


def _peer_kernel(ht_ref, u_ref, vt_ref, ea_ref, n1_ref, r2_ref, eb_ref, x_ref, mod_ref, fg_ref, o_ref, acc_ref, wa_ref,
                 wb_ref, *, final, chunk, n_e):
    e = pl.program_id(1)
    nk = PEER_N_KEYS
    n_i1 = u_ref.shape[0] // nk
    sub = PACKED_ROWS
    assert n_i1 == 4

    def step(write_ref, read_ref, base):
        def tokens(c, carry):
            lanes = pl.ds(pl.multiple_of(c * chunk, chunk), chunk)
            if write_ref is not None:
                act = _gelu_tanh(_dot(u_ref[...], ht_ref[:, lanes])).astype(BF16)
            if read_ref is not None:
                acc_ref[:, lanes] += _dot(vt_ref[...], read_ref[:, lanes])
            if write_ref is None:
                return carry
            for j in range(n_i1):
                m = None
                for h in range(PEER_HEADS):
                    row = (h, slice(base + j, base + j + 1), lanes)
                    n_row = jnp.broadcast_to(n1_ref[row], (sub, chunk)).astype(BF16)
                    ea_row = jnp.broadcast_to(ea_ref[row], (sub, chunk)).astype(BF16)
                    r2 = r2_ref[h, :, lanes].reshape(nk // sub, sub, chunk)
                    eb = eb_ref[h, :, lanes].reshape(nk // sub, sub, chunk)
                    term = jnp.where(r2 < n_row[None], eb, jnp.zeros((), BF16)) * ea_row[None]
                    m = term if m is None else m + term
                write_ref[j * nk:(j + 1) * nk, lanes] = m.reshape(nk, chunk) * act[j * nk:(j + 1) * nk]
            return carry

        lax.fori_loop(0, ht_ref.shape[1] // chunk, tokens, 0)

    @pl.when(e == 0)
    def _():
        acc_ref[...] = jnp.zeros_like(acc_ref)
        step(wa_ref, None, 0)

    @pl.when((e > 0) & (e < n_e) & (lax.rem(e, 2) == 0))
    def _():
        step(wa_ref, wb_ref, 0)

    @pl.when((e < n_e) & (lax.rem(e, 2) == 1))
    def _():
        step(wb_ref, wa_ref, n_i1)

    @pl.when(e == n_e)
    def _():
        step(None, wa_ref if n_e % 2 else wb_ref, 0)
        y = x_ref[...] + mod_ref[0, 5:6, :] * acc_ref[...].T
        if final:
            y = _rms(y, fg_ref[...])
        o_ref[...] = y


def _peer(ht, u_bf, vt_bf, ea, n1, r2, eb, x1, mod_l, final_g, seq, final):
    d, t = ht.shape
    n_exp = u_bf.shape[0]
    tb = TILE_PEER_TOKENS if seq % TILE_PEER_TOKENS == 0 else TILE_PEER_CHUNK
    eb_blk = TILE_PEER_EXPERTS
    n_e = n_exp // eb_blk
    n_i1 = eb_blk // PEER_N_KEYS
    tpb = seq // tb
    cur = lambda e: jnp.minimum(e, n_e - 1)
    i1_spec = pl.BlockSpec((PEER_HEADS, 8, tb), lambda i, e: (0, cur(e) // (8 // n_i1), i))
    tok3 = pl.BlockSpec((PEER_HEADS, PEER_N_KEYS, tb), lambda i, e: (0, 0, i))
    return pl.pallas_call(
        functools.partial(_peer_kernel, final=final, chunk=TILE_PEER_CHUNK, n_e=n_e),
        grid=(t // tb, n_e + 1),
        in_specs=[pl.BlockSpec((d, tb), lambda i, e: (0, i)),
                  pl.BlockSpec((eb_blk, d), lambda i, e: (cur(e), 0)),
                  pl.BlockSpec((d, eb_blk), lambda i, e: (0, jnp.maximum(e - 1, 0))),
                  i1_spec, i1_spec, tok3, tok3,
                  pl.BlockSpec((tb, d), lambda i, e: (i, 0)),
                  pl.BlockSpec((1, 6, d), lambda i, e: (i // tpb, 0, 0)),
                  pl.BlockSpec(final_g.shape, lambda i, e: (0, 0))],
        out_specs=pl.BlockSpec((tb, d), lambda i, e: (i, 0)),
        out_shape=jax.ShapeDtypeStruct((t, d), F32),
        scratch_shapes=[pltpu.VMEM((d, tb), F32), pltpu.VMEM((eb_blk, tb), BF16), pltpu.VMEM((eb_blk, tb), BF16)],
        compiler_params=_params("parallel", "arbitrary"),
        name="peer_experts",
    )(ht, u_bf, vt_bf, ea, n1, r2, eb, x1, mod_l, final_g)


def kernel(x, c, ada_w, ada_b, norm_mix_g, norm_ffn_g, w_in, nsa_cmp_pos_k, nsa_cmp_pos_v, nsa_cmp_wk, nsa_cmp_wv, diff_lam_q1, diff_lam_k1, diff_lam_q2, diff_lam_k2, diff_sub_g, mla_q_norm_g, mla_w_uq, mla_kv_norm_g, mla_w_ukv, swa_sinks, w_out, peer_w_q, peer_sub_k1, peer_sub_k2, peer_u, peer_v, final_g):
    batch, seq, d = x.shape
    depth = w_in.shape[0]
    assert seq % 512 == 0
    x2d = x.reshape(batch * seq, d)
    mod = _adaln(c, ada_w, ada_b).reshape(depth, batch, 6, d)
    table = _rope_table(seq)
    mixw = 4 * HEAD_DIM
    swa_rows = 3 * mixw + np.concatenate([h * HEAD_DIM + np.arange(HEAD_DIM) for h in (0, 2, 1, 3)])
    out_rows = jnp.asarray(np.concatenate([np.arange(3 * mixw), swa_rows]), jnp.int32)
    pad128 = lambda v: jnp.pad(v, (0, LANES - v.shape[0])).reshape(1, LANES)
    fg = final_g.reshape(1, d)
    for l in range(depth):
        w_big = _take_cols(w_in[l], _IN_IDX, _IN_SGN)
        wuq = _take_cols(mla_w_uq[l], _UQ_IDX, _UQ_SGN)
        wukv = _take_cols(mla_w_ukv[l], _UKV_IDX, _UKV_SGN)
        (nq, nk, nv, ng, dq, dk, dv, mq, mk, mv, sq, sk, sv) = _inproj(
            x2d, mod[l], norm_mix_g[l].reshape(1, d), w_big, table, wuq, wukv,
            mla_q_norm_g[l].reshape(1, -1), mla_kv_norm_g[l].reshape(1, -1), seq)
        ocmp, sel = _nsa_cmp(nq, nk, nv, nsa_cmp_wk[l], nsa_cmp_wv[l], nsa_cmp_pos_k[l], nsa_cmp_pos_v[l], batch, seq)
        o_a = _nsa(nq, nk, nv, sel, ng, ocmp, batch, seq)
        lamv = jnp.concatenate([pad128(diff_lam_q1[l]), pad128(diff_lam_k1[l]),
                                pad128(diff_lam_q2[l]), pad128(diff_lam_k2[l])], axis=0)
        sub_g2 = jnp.concatenate([diff_sub_g[l], diff_sub_g[l]]).reshape(1, LANES)
        o_b = _diff(dq, dk, dv, lamv, sub_g2, l, batch, seq)
        o_c = _mla(mq, mk, mv, batch, seq)
        o_d = _swa(sq, sk, sv, pad128(swa_sinks[l]), batch, seq)
        w_o = jnp.take(w_out[l], out_rows, axis=0).astype(BF16)
        x1, ht = _outproj(x2d, mod[l], norm_ffn_g[l].reshape(1, d), o_a, o_b, o_c, o_d, w_o, seq)
        ea, n1, r2, eb = _router(ht, peer_w_q[l].T.astype(BF16), peer_sub_k1[l].astype(BF16),
                                 peer_sub_k2[l].astype(BF16))
        x2d = _peer(ht, peer_u[l].astype(BF16), peer_v[l].T.astype(BF16), ea, n1, r2, eb, x1, mod[l], fg,
                    seq, final=(l == depth - 1))
    return x2d.reshape(batch, seq, d)
```

```python
import functools
import math

import numpy as np
import jax
import jax.numpy as jnp
from jax import lax
from jax.experimental import pallas as pl
from jax.experimental.pallas import tpu as pltpu

F32 = jnp.float32
BF16 = jnp.bfloat16

HEAD_DIM = 64
ROPE_THETA = 10000.0
EPS = 1e-6
NEG = -1e30
FORCE = 1e4

NSA_HEADS = 4
NSA_CMP_LEN = 32
NSA_CMP_STRIDE = 16
NSA_SEL_LEN = 64
NSA_TOP_N = 16
NSA_WINDOW = 512

DIFF_HEADS = 4
DIFF_QK_DIM = 32
DIFF_V_DIM = 64

MLA_HEADS = 4
MLA_Q_RANK = 256
MLA_KV_RANK = 128
MLA_NOPE_DIM = 64
MLA_ROPE_DIM = 32
MLA_V_DIM = 64

SWA_HEADS = 4
SWA_KV_HEADS = 2
SWA_WINDOW = 128

PEER_HEADS = 8
PEER_N_KEYS = 128
PEER_TOPK = 16
PEER_QUERY_DIM = 256

LOG2E = math.log2(math.e)
LANES = 128
PACKED_ROWS = 16
VMEM_LIMIT = 56 * 1024 * 1024

TILE_PROJ_ROWS = 256
TILE_OUTPROJ_ROWS = 512
TILE_ADALN_COLS = 1536
TILE_CMP_ROWS = 256
TILE_NSA = (512, 512)
TILE_DIFF = (256, 512)
TILE_MLA = (512, 512)
TILE_SWA = (512, 128)
TILE_ROUTER_TOKENS = 128
TILE_PEER_TOKENS = 1024
TILE_PEER_EXPERTS = 512
TILE_PEER_CHUNK = 1024


def _dot(a, b):
    return jnp.dot(a, b, preferred_element_type=F32)


def _dot_nt(a, b):
    return lax.dot_general(a, b, (((1,), (1,)), ((), ())), preferred_element_type=F32)


def _params(*sem):
    return pltpu.CompilerParams(dimension_semantics=sem, vmem_limit_bytes=VMEM_LIMIT)


def _rms(x, g):
    return x * lax.rsqrt(jnp.mean(x * x, axis=-1, keepdims=True) + EPS) * g


def _rot_idx(base, dim):
    half = dim // 2
    idx = np.concatenate([base + half + np.arange(half), base + np.arange(half)])
    sgn = np.concatenate([-np.ones(half), np.ones(half)])
    return idx, sgn


def _in_plan():
    d = HEAD_DIM
    nsa0 = 0
    nsa_cols = NSA_HEADS * d + 6 * d + 3 * NSA_HEADS
    diff0 = nsa0 + nsa_cols
    diff_cols = 2 * DIFF_HEADS * 2 * DIFF_QK_DIM + DIFF_HEADS * DIFF_V_DIM
    mla0 = diff0 + diff_cols
    mla_cols = MLA_Q_RANK + MLA_KV_RANK + MLA_ROPE_DIM
    swa0 = mla0 + mla_cols
    idx, sgn, off = [], [], {}

    groups = {}

    def add(name, i, s=None):
        i = np.asarray(i, np.int64)
        s = np.ones(len(i)) if s is None else np.asarray(s, np.float64)
        pad = (-len(i)) % LANES
        groups[name] = (np.concatenate([i, np.zeros(pad, np.int64)]), np.concatenate([s, np.zeros(pad)]))

    def heads_rot(base, nheads, dim):
        ii, ss = zip(*[_rot_idx(base + h * dim, dim) for h in range(nheads)])
        return np.concatenate(ii), np.concatenate(ss)

    nq = nsa0 + np.arange(NSA_HEADS * d)
    add("nq", nq)
    add("nqr", *heads_rot(nsa0, NSA_HEADS, d))
    kb = nsa0 + NSA_HEADS * d
    kc, vc, ksl, vsl, kw, vw = [kb + j * d for j in range(6)]
    dup = lambda b: np.concatenate([b + np.arange(d), b + np.arange(d)])
    add("nk", np.concatenate([dup(kc), dup(ksl), dup(kw)]))
    kr = [_rot_idx(b, d) for b in (kc, kc, ksl, ksl, kw, kw)]
    add("nkr", np.concatenate([a for a, _ in kr]), np.concatenate([b for _, b in kr]))
    add("nv", np.concatenate([dup(vc), dup(vsl), dup(vw)]))
    add("ng", kb + 6 * d + np.arange(3 * NSA_HEADS))
    nqk = DIFF_HEADS * 2 * DIFF_QK_DIM
    add("dq", diff0 + np.arange(nqk))
    add("dqr", *heads_rot(diff0, 2 * DIFF_HEADS, DIFF_QK_DIM))
    add("dk", diff0 + nqk + np.arange(nqk))
    add("dkr", *heads_rot(diff0 + nqk, 2 * DIFF_HEADS, DIFF_QK_DIM))
    add("dv", diff0 + 2 * nqk + np.arange(DIFF_HEADS * DIFF_V_DIM))
    add("mcq", mla0 + np.arange(MLA_Q_RANK))
    add("mckv", mla0 + MLA_Q_RANK + np.arange(MLA_KV_RANK))
    kr0 = mla0 + MLA_Q_RANK + MLA_KV_RANK
    z64 = np.zeros(MLA_NOPE_DIM, np.int64)
    add("mkr", np.concatenate([z64, kr0 + np.arange(MLA_ROPE_DIM)]),
        np.concatenate([np.zeros(MLA_NOPE_DIM), np.ones(MLA_ROPE_DIM)]))
    ri, rs = _rot_idx(kr0, MLA_ROPE_DIM)
    add("mkrr", np.concatenate([z64, ri]), np.concatenate([np.zeros(MLA_NOPE_DIM), rs]))
    order = [0, 2, 1, 3]
    add("sq", np.concatenate([swa0 + h * d + np.arange(d) for h in order]))
    sr = [_rot_idx(swa0 + h * d, d) for h in order]
    add("sqr", np.concatenate([a for a, _ in sr]), np.concatenate([b for _, b in sr]))
    sk0 = swa0 + SWA_HEADS * d
    add("sk", sk0 + np.arange(SWA_KV_HEADS * d))
    add("skr", *heads_rot(sk0, SWA_KV_HEADS, d))
    add("sv", sk0 + SWA_KV_HEADS * d + np.arange(SWA_KV_HEADS * d))
    order = ["nq", "nqr", "nk", "ng", "nkr", "mckv", "nv", "sk", "dq", "dqr", "dk", "dkr", "dv", "mcq",
             "mkr", "mkrr", "skr", "sv", "sq", "sqr"]
    assert sorted(order) == sorted(groups)
    blk, run, pos = {}, [], 0
    for name in order:
        off[name] = pos
        idx.append(groups[name][0])
        sgn.append(groups[name][1])
        run.append(name)
        pos += len(groups[name][0])
        if pos % (2 * LANES) == 0:
            start = off[run[0]]
            blk.update({n: (start, pos - start) for n in run})
            run = []
    assert not run
    return np.concatenate(idx), np.concatenate(sgn), off, blk


_IN_IDX, _IN_SGN, _OFF, _BLK = _in_plan()


def _mla_plans():
    qd = MLA_NOPE_DIM + MLA_ROPE_DIM
    qi, qs, ri, rs = [], [], [], []
    for h in range(MLA_HEADS):
        b = h * qd
        qi += [b + np.arange(qd), np.zeros(LANES - qd, np.int64)]
        qs += [np.ones(qd), np.zeros(LANES - qd)]
        a, s = _rot_idx(b + MLA_NOPE_DIM, MLA_ROPE_DIM)
        ri += [np.zeros(MLA_NOPE_DIM, np.int64), a, np.zeros(LANES - qd, np.int64)]
        rs += [np.zeros(MLA_NOPE_DIM), s, np.zeros(LANES - qd)]
    kd = MLA_NOPE_DIM + MLA_V_DIM
    ki, ks, vi = [], [], []
    for h in range(MLA_HEADS):
        ki += [h * kd + np.arange(MLA_NOPE_DIM), np.zeros(LANES - MLA_NOPE_DIM, np.int64)]
        ks += [np.ones(MLA_NOPE_DIM), np.zeros(LANES - MLA_NOPE_DIM)]
        vi += [h * kd + MLA_NOPE_DIM + np.arange(MLA_V_DIM)]
    uq_idx = np.concatenate(qi + ri)
    uq_sgn = np.concatenate(qs + rs)
    ukv_idx = np.concatenate(ki + vi)
    ukv_sgn = np.concatenate(ks + [np.ones(MLA_HEADS * MLA_V_DIM)])
    return uq_idx, uq_sgn, ukv_idx, ukv_sgn


_UQ_IDX, _UQ_SGN, _UKV_IDX, _UKV_SGN = _mla_plans()


def _take_cols(w, idx, sgn):
    pieces, start = [], 0
    for i in range(1, len(idx) + 1):
        same_run = (i < len(idx) and sgn[i] == sgn[start]
                    and (sgn[i] == 0.0 or idx[i] == idx[i - 1] + 1))
        if not same_run:
            n, s = i - start, float(sgn[start])
            run = w[..., int(idx[start]):int(idx[start]) + n]
            pieces.append(jnp.zeros(w.shape[:-1] + (n,), w.dtype) if s == 0.0 else (run if s == 1.0 else -run))
            start = i
    return jnp.concatenate(pieces, axis=-1).astype(BF16)


def _rope_table(seq):
    def cs(dim):
        inv = 1.0 / (ROPE_THETA ** (jnp.arange(0, dim, 2, dtype=F32) / dim))
        ang = jnp.arange(seq, dtype=F32)[:, None] * inv[None, :]
        c, s = jnp.cos(ang), jnp.sin(ang)
        return jnp.concatenate([c, c], 1), jnp.concatenate([s, s], 1)
    ch, sh = cs(HEAD_DIM)
    cd, sd = cs(DIFF_QK_DIM)
    cm, sm = cs(MLA_ROPE_DIM)
    one = jnp.ones((seq, MLA_NOPE_DIM), F32)
    z64 = jnp.zeros((seq, MLA_NOPE_DIM), F32)
    z32 = jnp.zeros((seq, LANES - MLA_NOPE_DIM - MLA_ROPE_DIM), F32)
    parts = [jnp.tile(ch, (1, 2)), jnp.tile(sh, (1, 2)), jnp.tile(cd, (1, 4)), jnp.tile(sd, (1, 4)),
             jnp.concatenate([one, cm, z32], 1), jnp.concatenate([z64, sm, z32], 1),
             jnp.concatenate([z64, cm, z32], 1), jnp.concatenate([z64, sm, z32], 1)]
    return jnp.concatenate(parts, 1)


def _adaln_kernel(c_ref, w_ref, b_ref, o_ref):
    c = c_ref[...]
    sc = (c * jax.nn.sigmoid(c)).astype(BF16)
    o_ref[0] = _dot(sc, w_ref[0].astype(BF16)) + b_ref[0]


def _adaln(c, ada_w, ada_b):
    nl, d, n6 = ada_w.shape
    b = c.shape[0]
    tn = TILE_ADALN_COLS
    return pl.pallas_call(
        _adaln_kernel,
        grid=(nl, n6 // tn),
        in_specs=[pl.BlockSpec((b, d), lambda l, j: (0, 0)),
                  pl.BlockSpec((1, d, tn), lambda l, j: (l, 0, j)),
                  pl.BlockSpec((1, 1, tn), lambda l, j: (l, 0, j))],
        out_specs=pl.BlockSpec((1, b, tn), lambda l, j: (l, 0, j)),
        out_shape=jax.ShapeDtypeStruct((nl, b, n6), F32),
        compiler_params=_params("parallel", "parallel"),
        name="adaln",
    )(c, ada_w, ada_b.reshape(nl, 1, n6))


def _inproj_kernel(x_ref, mod_ref, ng_ref, w_ref, tab_ref, wuq_ref, wukv_ref, gq_ref, gkv_ref,
                   nq_ref, nk_ref, nv_ref, ngo_ref, dq_ref, dk_ref, dv_ref,
                   mq_ref, mk_ref, mv_ref, sq_ref, sk_ref, sv_ref):
    x = x_ref[...]
    h = _rms(x, ng_ref[...]) * (1.0 + mod_ref[0, 1:2, :]) + mod_ref[0, 0:1, :]
    hb = h.astype(BF16)

    runs = {}

    def mm(name, width):
        start, size = _BLK[name]
        if start not in runs:
            runs[start] = _dot(hb, w_ref[:, start:start + size])
        o = _OFF[name] - start
        return runs[start][:, o:o + width]

    def tab(j, reps):
        t = tab_ref[:, j * LANES:(j + 1) * LANES]
        return t if reps == 1 else jnp.concatenate([t] * reps, axis=1)

    def rope(name, rname, width, cj, scale=1.0):
        r = mm(name, width) * tab(cj, width // LANES) + mm(rname, width) * tab(cj + 1, width // LANES)
        return r if scale == 1.0 else r * scale

    d = HEAD_DIM
    nq_ref[...] = rope("nq", "nqr", 256, 0, LOG2E * d ** -0.5).astype(BF16)
    nk_ref[...] = rope("nk", "nkr", 384, 0).astype(BF16)
    nv_ref[...] = mm("nv", 384).astype(BF16)
    ngo_ref[...] = jax.nn.sigmoid(mm("ng", LANES))
    dq_ref[...] = rope("dq", "dqr", 256, 2, LOG2E * DIFF_QK_DIM ** -0.5).astype(BF16)
    dk_ref[...] = rope("dk", "dkr", 256, 2).astype(BF16)
    dv_ref[...] = mm("dv", 256).astype(BF16)
    cq = _rms(mm("mcq", MLA_Q_RANK), gq_ref[...]).astype(BF16)
    nh = MLA_HEADS * LANES
    qa = _dot(cq, wuq_ref[:, 0:nh])
    qb = _dot(cq, wuq_ref[:, nh:2 * nh])
    mq = (qa * tab(4, MLA_HEADS) + qb * tab(5, MLA_HEADS)) * (LOG2E * (MLA_NOPE_DIM + MLA_ROPE_DIM) ** -0.5)
    mq_ref[...] = mq.astype(BF16)
    ckv = _rms(mm("mckv", MLA_KV_RANK), gkv_ref[...]).astype(BF16)
    kk = _dot(ckv, wukv_ref[:, 0:nh])
    kr = mm("mkr", LANES) * tab(6, 1) + mm("mkrr", LANES) * tab(7, 1)
    mk_ref[...] = (kk + jnp.concatenate([kr] * MLA_HEADS, axis=1)).astype(BF16)
    mv_ref[...] = _dot(ckv, wukv_ref[:, nh:nh + MLA_HEADS * MLA_V_DIM]).astype(BF16)
    sq_ref[...] = rope("sq", "sqr", 256, 0, LOG2E * d ** -0.5).astype(BF16)
    sk_ref[...] = rope("sk", "skr", 128, 0).astype(BF16)
    sv_ref[...] = mm("sv", 128).astype(BF16)


def _inproj(x2d, mod_l, norm_g, w_big, table, wuq, wukv, gq, gkv, seq):
    t, d = x2d.shape
    tm = TILE_PROJ_ROWS
    tpb = seq // tm
    widths = [256, 384, 384, 128, 256, 256, 256, 512, 512, 256, 256, 128, 128]
    dts = [BF16, BF16, BF16, F32, BF16, BF16, BF16, BF16, BF16, BF16, BF16, BF16, BF16]
    full = lambda a: pl.BlockSpec(a.shape, lambda i: (0,) * a.ndim)
    return pl.pallas_call(
        _inproj_kernel,
        grid=(t // tm,),
        in_specs=[pl.BlockSpec((tm, d), lambda i: (i, 0)),
                  pl.BlockSpec((1, 6, d), lambda i: (i // tpb, 0, 0)),
                  full(norm_g), full(w_big),
                  pl.BlockSpec((tm, table.shape[1]), lambda i: (i % tpb, 0)),
                  full(wuq), full(wukv), full(gq), full(gkv)],
        out_specs=[pl.BlockSpec((tm, w), lambda i: (i, 0)) for w in widths],
        out_shape=[jax.ShapeDtypeStruct((t, w), dt) for w, dt in zip(widths, dts)],
        compiler_params=_params("parallel"),
        name="inproj",
    )(x2d, mod_l, norm_g, w_big, table, wuq, wukv, gq, gkv)


def _lane_mask(lo, hi):
    lane = lax.broadcasted_iota(jnp.int32, (1, LANES), 1)
    return (lane >= lo) & (lane < hi)


def _masked(q, lo, hi):
    return jnp.where(_lane_mask(lo, hi), q, jnp.zeros_like(q))


def _chain(q, k, v, mask, state, acc_ref, c):
    m, l = state
    s = _dot_nt(q, k)
    if mask is not None:
        s = jnp.where(mask, s, NEG)
    m2 = jnp.maximum(m, jnp.max(s, axis=-1, keepdims=True))
    a = jnp.exp2(m - m2)
    p = jnp.exp2(s - m2)
    acc_ref[c] = a * acc_ref[c] + _dot(p.astype(BF16), v)
    return m2, a * l + jnp.sum(p, axis=-1, keepdims=True)


def _init_state(n, rows):
    return tuple((jnp.full((rows, 1), NEG, F32), jnp.zeros((rows, 1), F32)) for _ in range(n))


def _ktile(ref, j, tk, c0, c1):
    return ref[pl.ds(pl.multiple_of(j * tk, tk), tk), c0:c1]


def _qpos(i, tq, reps=1):
    p = i * tq + lax.broadcasted_iota(jnp.int32, (tq, 1), 0)
    return p if reps == 1 else jnp.concatenate([p] * reps, axis=0)


def _kpos(j, tk):
    return j * tk + lax.broadcasted_iota(jnp.int32, (1, tk), 1)


def _half_heads(q):
    return [_masked(q[:, c * LANES:(c + 1) * LANES], 64 * hh, 64 * hh + 64) for c in range(2) for hh in range(2)]


def _pair(lo_val, hi_val):
    return jnp.where(_lane_mask(0, 64), lo_val, hi_val)


def _nsa_cmp_kernel(q_ref, kc_ref, vc_ref, wk_ref, wv_ref, pk_ref, pv_ref, ov_ref, oc_ref, sel_ref, *, top_n, n_sel):
    half = wk_ref.shape[1]

    def compress(x_ref, w_ref, p_ref):
        x = x_ref[...]
        a = _dot(x, w_ref[0])
        b = _dot(x, w_ref[1])
        p = jnp.broadcast_to(p_ref[...], (8, 2 * half)).astype(BF16)
        const = (_dot(p[:, 0:half], w_ref[0]) + _dot(p[:, half:2 * half], w_ref[1]))[0:1]
        return a + jnp.concatenate([b[1:], b[:1]], axis=0) + const

    kcmp = compress(kc_ref, wk_ref, pk_ref).astype(BF16)
    vcmp = compress(vc_ref, wv_ref, pv_ref).astype(BF16)
    ncp = kcmp.shape[0]
    ov = ov_ref[...]
    rb = TILE_CMP_ROWS
    cend = NSA_CMP_STRIDE * lax.broadcasted_iota(jnp.int32, (1, ncp), 1) + (NSA_CMP_LEN - 1)
    lane = lax.broadcasted_iota(jnp.int32, (1, LANES), 1)

    def block(r, carry):
        r0 = pl.multiple_of(r * rb, rb)
        q = q_ref[pl.ds(r0, rb), :]
        tpos = r0 + lax.broadcasted_iota(jnp.int32, (rb, 1), 0)
        vis = cend <= tpos
        psum = jnp.zeros((rb, ncp), F32)
        outs = []
        for c in range(2):
            halves = []
            for hh in range(2):
                qm = _masked(q[:, c * LANES:(c + 1) * LANES], 64 * hh, 64 * hh + 64)
                s = jnp.where(vis, _dot_nt(qm, kcmp), NEG)
                e = jnp.exp2(s - jnp.max(s, axis=-1, keepdims=True))
                p = jnp.where(vis, e / jnp.sum(e, axis=-1, keepdims=True), 0.0)
                psum = psum + p
                halves.append(_dot(p.astype(BF16), vcmp))
            outs.append(_pair(halves[0], halves[1]))
        oc_ref[pl.ds(r0, rb), :] = jnp.concatenate(outs, axis=1)
        hi = psum.astype(BF16)
        lo = (psum - hi.astype(F32)).astype(BF16)
        imp = _dot(hi, ov) + _dot(lo, ov)
        nsp = -(-n_sel // 8) * 8
        imp_t = imp.T[0:nsp]
        blk = lax.broadcasted_iota(jnp.int32, (nsp, 1), 0)
        qblk = (r0 + lax.broadcasted_iota(jnp.int32, (1, rb), 1)) // NSA_SEL_LEN
        allowed = blk <= qblk
        forced = (blk == 0) | (blk == qblk) | (blk == qblk - 1)
        impf = jnp.where(allowed, jnp.where(forced, FORCE, imp_t), NEG)
        rank = jnp.zeros((nsp, rb), F32)
        for j in range(n_sel):
            row = impf[j:j + 1, :]
            rank = rank + jnp.where(blk > j, jnp.where(row >= impf, 1.0, 0.0), jnp.where(row > impf, 1.0, 0.0))
        sel_t = jnp.where(allowed & (rank < top_n), 1.0, 0.0)
        if nsp < LANES:
            sel_t = jnp.concatenate([sel_t, jnp.zeros((LANES - nsp, rb), F32)], axis=0)
        sel_ref[pl.ds(r0, rb), :] = sel_t.T.astype(BF16)
        return carry

    lax.fori_loop(0, q_ref.shape[0] // rb, block, 0)


def _nsa_cmp(nq, nk, nv, wk, wv, pos_k, pos_v, batch, seq):
    d = HEAD_DIM
    nc = seq // NSA_CMP_STRIDE
    ncp = -(-nc // LANES) * LANES
    n_sel = seq // NSA_SEL_LEN
    assert n_sel <= LANES
    top_n = min(NSA_TOP_N, n_sel)

    def seg(a):
        a = a[:, :d].reshape(batch, nc, NSA_CMP_STRIDE * d)
        return jnp.pad(a, ((0, 0), (0, ncp - nc), (0, 0))).reshape(batch * ncp, NSA_CMP_STRIDE * d)

    half = NSA_CMP_STRIDE * d
    dupw = lambda w: jnp.concatenate([w, w], axis=1).reshape(2, half, 2 * d).astype(BF16)
    cpos = NSA_CMP_STRIDE * np.arange(ncp)[:, None] + np.arange(NSA_CMP_LEN)[None, :]
    ovl = np.zeros((ncp, LANES), np.float32)
    for j in range(n_sel):
        ovl[:, j] = (cpos // NSA_SEL_LEN == j).mean(axis=1)
    ovl[nc - 1:, :] = 0.0
    full = lambda a: pl.BlockSpec(a.shape, lambda b: (0,) * a.ndim)
    wk2, wv2 = dupw(wk), dupw(wv)
    pk, pv = pos_k.reshape(1, -1), pos_v.reshape(1, -1)
    ov = jnp.asarray(ovl, BF16)
    return pl.pallas_call(
        functools.partial(_nsa_cmp_kernel, top_n=top_n, n_sel=n_sel),
        grid=(batch,),
        in_specs=[pl.BlockSpec((seq, 256), lambda b: (b, 0)),
                  pl.BlockSpec((ncp, half), lambda b: (b, 0)),
                  pl.BlockSpec((ncp, half), lambda b: (b, 0)),
                  full(wk2), full(wv2), full(pk), full(pv), full(ov)],
        out_specs=[pl.BlockSpec((seq, 256), lambda b: (b, 0)),
                   pl.BlockSpec((seq, LANES), lambda b: (b, 0))],
        out_shape=[jax.ShapeDtypeStruct((batch * seq, 256), F32),
                   jax.ShapeDtypeStruct((batch * seq, LANES), BF16)],
        compiler_params=_params("parallel"),
        name="nsa_cmp",
    )(nq, seg(nk), seg(nv), wk2, wv2, pk, pv, ov)


def _nsa_kernel(q_ref, k_ref, v_ref, sel_ref, g_ref, oc_ref, e_ref, o_ref, acc_ref, *, tq, tk):
    i = pl.program_id(1)
    qh = _half_heads(q_ref[...])
    qp = _qpos(i, tq)
    sel = sel_ref[...]
    nh = NSA_HEADS
    acc_ref[...] = jnp.zeros_like(acc_ref)

    def sel_step(j, st, diag):
        mv = _dot(sel, e_ref[:, pl.ds(pl.multiple_of(j * tk, tk), tk)])
        if diag:
            mv = jnp.where(_kpos(j, tk) <= qp, mv, 0.0)
        mask = mv > 0.5
        k = _ktile(k_ref, j, tk, 128, 256)
        v = _ktile(v_ref, j, tk, 128, 256)
        return tuple(_chain(qh[h], k, v, mask, st[h], acc_ref, h) for h in range(nh))

    nfull = (i * tq) // tk
    st = lax.fori_loop(0, nfull, lambda j, s: sel_step(j, s, False), _init_state(nh, tq))
    st_sel = sel_step(nfull, st, True)

    def win_step(j, st):
        dist = qp - _kpos(j, tk)
        mask = jnp.where(dist >= 0, dist, NSA_WINDOW) < NSA_WINDOW
        k = _ktile(k_ref, j, tk, 256, 384)
        v = _ktile(v_ref, j, tk, 256, 384)
        return tuple(_chain(qh[h], k, v, mask, st[h], acc_ref, nh + h) for h in range(nh))

    wlo = jnp.maximum(i * tq - NSA_WINDOW, 0) // tk
    st_win = lax.fori_loop(wlo, nfull + 1, win_step, _init_state(nh, tq))
    g = g_ref[...]
    oc = oc_ref[...]
    outs = []
    for c in range(2):
        occ = oc[:, c * LANES:(c + 1) * LANES]

        def comb(h):
            o_sel = acc_ref[h] / st_sel[h][1]
            o_win = acc_ref[nh + h] / st_win[h][1]
            return g[:, 3 * h:3 * h + 1] * occ + g[:, 3 * h + 1:3 * h + 2] * o_sel + g[:, 3 * h + 2:3 * h + 3] * o_win

        outs.append(_pair(comb(2 * c), comb(2 * c + 1)))
    o_ref[...] = jnp.concatenate(outs, axis=1).astype(BF16)


def _nsa(nq, nk, nv, sel, gates, ocmp, batch, seq):
    tq, tk = TILE_NSA
    nb = seq // tq
    expand = np.zeros((LANES, seq), np.float32)
    for j in range(seq // NSA_SEL_LEN):
        expand[j, j * NSA_SEL_LEN:(j + 1) * NSA_SEL_LEN] = 1.0
    e = jnp.asarray(expand, BF16)
    row = lambda w: pl.BlockSpec((tq, w), lambda b, i: (b * nb + i, 0))
    per_b = lambda w: pl.BlockSpec((seq, w), lambda b, i: (b, 0))
    return pl.pallas_call(
        functools.partial(_nsa_kernel, tq=tq, tk=tk),
        grid=(batch, nb),
        in_specs=[row(256), per_b(384), per_b(384), row(LANES), row(LANES), row(256),
                  pl.BlockSpec(e.shape, lambda b, i: (0, 0))],
        out_specs=row(256),
        out_shape=jax.ShapeDtypeStruct((batch * seq, 256), BF16),
        scratch_shapes=[pltpu.VMEM((2 * NSA_HEADS, tq, LANES), F32)],
        compiler_params=_params("parallel", "arbitrary"),
        name="nsa_attn",
    )(nq, nk, nv, sel, gates, ocmp, e)


def _diff_kernel(q_ref, k_ref, v_ref, lam_ref, sg_ref, o_ref, acc_ref, *, lam_init, tq, tk):
    i = pl.program_id(1)
    q = q_ref[...]
    qp4 = _qpos(i, tq, 4)
    lv = lam_ref[...]
    lam = (jnp.exp(jnp.sum(lv[0:1] * lv[1:2], axis=-1, keepdims=True))
           - jnp.exp(jnp.sum(lv[2:3] * lv[3:4], axis=-1, keepdims=True)) + lam_init)
    qs = [jnp.concatenate([_masked(q[:, c * LANES:(c + 1) * LANES], 32 * t, 32 * t + 32) for t in range(4)], axis=0)
          for c in range(2)]
    acc_ref[...] = jnp.zeros_like(acc_ref)

    def step(j, st, diag):
        mask = (_kpos(j, tk) <= qp4) if diag else None
        return tuple(_chain(qs[c], _ktile(k_ref, j, tk, c * LANES, (c + 1) * LANES),
                            _ktile(v_ref, j, tk, c * LANES, (c + 1) * LANES), mask, st[c], acc_ref, c)
                     for c in range(2))

    nfull = (i * tq) // tk
    st = lax.fori_loop(0, nfull, lambda j, s: step(j, s, False), _init_state(2, 4 * tq))
    st = step(nfull, st, True)
    outs = []
    for c in range(2):
        o = acc_ref[c] / st[c][1]
        r = [o[t * tq:(t + 1) * tq] for t in range(4)]
        dd = _pair(r[0] - lam * r[1], r[2] - lam * r[3])
        sq = dd * dd
        lo = _lane_mask(0, 64)
        ms = _pair(jnp.sum(jnp.where(lo, sq, 0.0), axis=-1, keepdims=True),
                   jnp.sum(jnp.where(lo, 0.0, sq), axis=-1, keepdims=True)) * (1.0 / DIFF_V_DIM)
        outs.append(dd * lax.rsqrt(ms + EPS) * sg_ref[...] * (1.0 - lam_init))
    o_ref[...] = jnp.concatenate(outs, axis=1).astype(BF16)


def _diff(dq, dk, dv, lamv, sub_g2, layer, batch, seq):
    tq, tk = TILE_DIFF
    nb = seq // tq
    lam_init = 0.8 - 0.6 * math.exp(-0.3 * layer)
    row = lambda w: pl.BlockSpec((tq, w), lambda b, i: (b * nb + i, 0))
    per_b = lambda w: pl.BlockSpec((seq, w), lambda b, i: (b, 0))
    full = lambda a: pl.BlockSpec(a.shape, lambda b, i: (0,) * a.ndim)
    return pl.pallas_call(
        functools.partial(_diff_kernel, lam_init=lam_init, tq=tq, tk=tk),
        grid=(batch, nb),
        in_specs=[row(256), per_b(256), per_b(256), full(lamv), full(sub_g2)],
        out_specs=row(256),
        out_shape=jax.ShapeDtypeStruct((batch * seq, 256), BF16),
        scratch_shapes=[pltpu.VMEM((2, 4 * tq, LANES), F32)],
        compiler_params=_params("parallel", "arbitrary"),
        name="diff_attn",
    )(dq, dk, dv, lamv, sub_g2)


def _mla_kernel(q_ref, k_ref, v_ref, o_ref, acc_ref, *, tq, tk):
    i = pl.program_id(1)
    qp = _qpos(i, tq)
    nh = MLA_HEADS
    acc_ref[...] = jnp.zeros_like(acc_ref)

    def step(j, st, diag):
        mask = (_kpos(j, tk) <= qp) if diag else None
        return tuple(_chain(q_ref[:, h * LANES:(h + 1) * LANES], _ktile(k_ref, j, tk, h * LANES, (h + 1) * LANES),
                            _ktile(v_ref, j, tk, (h // 2) * LANES, (h // 2 + 1) * LANES), mask, st[h], acc_ref, h)
                     for h in range(nh))

    nfull = (i * tq) // tk
    st = lax.fori_loop(0, nfull, lambda j, s: step(j, s, False), _init_state(nh, tq))
    st = step(nfull, st, True)
    o = [acc_ref[h] / st[h][1] for h in range(nh)]
    o_ref[...] = jnp.concatenate([_pair(o[0], o[1]), _pair(o[2], o[3])], axis=1).astype(BF16)


def _mla(mq, mk, mv, batch, seq):
    tq, tk = TILE_MLA
    nb = seq // tq
    row = lambda w: pl.BlockSpec((tq, w), lambda b, i: (b * nb + i, 0))
    per_b = lambda w: pl.BlockSpec((seq, w), lambda b, i: (b, 0))
    return pl.pallas_call(
        functools.partial(_mla_kernel, tq=tq, tk=tk),
        grid=(batch, nb),
        in_specs=[row(512), per_b(512), per_b(256)],
        out_specs=row(256),
        out_shape=jax.ShapeDtypeStruct((batch * seq, 256), BF16),
        scratch_shapes=[pltpu.VMEM((MLA_HEADS, tq, LANES), F32)],
        compiler_params=_params("parallel", "arbitrary"),
        name="mla_attn",
    )(mq, mk, mv)


def _swa_kernel(q_ref, k_ref, v_ref, sink_ref, o_ref, *, tq, ts):
    i = pl.program_id(1)
    sk = sink_ref[...] * LOG2E
    for s in range(tq // ts):
        g = i * (tq // ts) + s
        k0 = pl.multiple_of(jnp.maximum(g - 1, 0) * ts, ts)
        k = k_ref[pl.ds(k0, 2 * ts), :]
        v = v_ref[pl.ds(k0, 2 * ts), :]
        dist = (g * ts + lax.broadcasted_iota(jnp.int32, (ts, 1), 0)) - (
            k0 + lax.broadcasted_iota(jnp.int32, (1, 2 * ts), 1))
        mask = jnp.where(dist >= 0, dist, SWA_WINDOW) < SWA_WINDOW
        o = []
        for c, h in enumerate((0, 2, 1, 3)):
            q = _half_heads(q_ref[s * ts:(s + 1) * ts, :])[c]
            sc = jnp.where(mask, _dot_nt(q, k), NEG)
            m = jnp.maximum(jnp.max(sc, axis=-1, keepdims=True), sk[:, h:h + 1])
            p = jnp.exp2(sc - m)
            l = jnp.sum(p, axis=-1, keepdims=True) + jnp.exp2(sk[:, h:h + 1] - m)
            o.append(_dot(p.astype(BF16), v) / l)
        o_ref[s * ts:(s + 1) * ts, :] = jnp.concatenate([_pair(o[0], o[1]), _pair(o[2], o[3])], axis=1).astype(BF16)


def _swa(sq, sk, sv, sinks, batch, seq):
    tq, ts = TILE_SWA
    assert ts == SWA_WINDOW and seq >= 2 * ts
    nb = seq // tq
    row = lambda w: pl.BlockSpec((tq, w), lambda b, i: (b * nb + i, 0))
    per_b = lambda w: pl.BlockSpec((seq, w), lambda b, i: (b, 0))
    return pl.pallas_call(
        functools.partial(_swa_kernel, tq=tq, ts=ts),
        grid=(batch, nb),
        in_specs=[row(256), per_b(128), per_b(128), pl.BlockSpec(sinks.shape, lambda b, i: (0, 0))],
        out_specs=row(256),
        out_shape=jax.ShapeDtypeStruct((batch * seq, 256), BF16),
        compiler_params=_params("parallel", "arbitrary"),
        name="swa_attn",
    )(sq, sk, sv, sinks)


def _outproj_kernel(x_ref, mod_ref, ng_ref, oa_ref, ob_ref, oc_ref, od_ref, w_ref, x1_ref, ht_ref):
    acc = _dot(oa_ref[...], w_ref[0:256, :])
    acc = acc + _dot(ob_ref[...], w_ref[256:512, :])
    acc = acc + _dot(oc_ref[...], w_ref[512:768, :])
    acc = acc + _dot(od_ref[...], w_ref[768:1024, :])
    x1 = x_ref[...] + mod_ref[0, 2:3, :] * acc
    x1_ref[...] = x1
    h = _rms(x1, ng_ref[...]) * (1.0 + mod_ref[0, 4:5, :]) + mod_ref[0, 3:4, :]
    ht_ref[...] = h.T.astype(BF16)


def _outproj(x2d, mod_l, norm_g, oa, ob, oc, od, w_out, seq):
    t, d = x2d.shape
    tm = TILE_OUTPROJ_ROWS
    tpb = seq // tm
    row = lambda w: pl.BlockSpec((tm, w), lambda i: (i, 0))
    full = lambda a: pl.BlockSpec(a.shape, lambda i: (0,) * a.ndim)
    return pl.pallas_call(
        _outproj_kernel,
        grid=(t // tm,),
        in_specs=[row(d), pl.BlockSpec((1, 6, d), lambda i: (i // tpb, 0, 0)), full(norm_g),
                  row(256), row(256), row(256), row(256), full(w_out)],
        out_specs=[row(d), pl.BlockSpec((d, tm), lambda i: (0, i))],
        out_shape=[jax.ShapeDtypeStruct((t, d), F32), jax.ShapeDtypeStruct((d, t), BF16)],
        compiler_params=_params("parallel"),
        name="outproj",
    )(x2d, mod_l, norm_g, oa, ob, oc, od, w_out)


_CAND_PIECES = [(0, 0, 8), (0, 8, 8), (1, 0, 8), (2, 0, 5), (3, 0, 4), (4, 0, 3), (5, 0, 2), (6, 0, 2), (7, 0, 2),
                (None, 0, 8)]


_CODE_UNIT = 2.0 ** 114
_TAKEN_BELOW = -(2.0 ** 119)
_INVALID = -(2.0 ** 100)


def _rank_code(r):
    return -(64.0 + r) * _CODE_UNIT


def _top16(s):
    tb = s.shape[1]
    row16 = lax.broadcasted_iota(jnp.int32, (PEER_TOPK, tb), 0)
    vals = jnp.zeros((PEER_TOPK, tb), F32)
    work = s
    for r in range(PEER_TOPK):
        m = jnp.max(work, axis=0, keepdims=True)
        work = jnp.where(work == m, _rank_code(r), work)
        vals = jnp.where(row16 == r, m, vals)
    return vals, work


def _router_head(h, ht, wq_ref, k1_ref, k2_ref, ea_ref, n1_ref, r2_ref, eb_ref):
    tb = ht.shape[1]
    nk = PEER_N_KEYS
    row8 = lax.broadcasted_iota(jnp.int32, (8, tb), 0)
    row16 = lax.broadcasted_iota(jnp.int32, (PEER_TOPK, tb), 0)
    o = pl.multiple_of(h * 2 * nk, 2 * nk)
    q1 = _dot(wq_ref[pl.ds(o, nk), :], ht).astype(BF16)
    q2 = _dot(wq_ref[pl.ds(o + nk, nk), :], ht).astype(BF16)
    s1 = _dot(k1_ref[...], q1)
    s2 = _dot(k2_ref[...], q2)
    v1, code1 = _top16(s1)
    v2, code2 = _top16(s2)
    top = v1[0:1] + v2[0:1]

    def cells(r1, c0):
        return v1[8:16] + v2[0:1] if r1 is None else v1[r1:r1 + 1] + v2[c0:c0 + 8]

    pieces = []
    for r1, c0, valid in _CAND_PIECES:
        p = cells(r1, c0)
        pieces.append(p if valid == 8 else jnp.where(row8 < valid, p, _INVALID))
    for _ in range(PEER_TOPK):
        m = pieces[0]
        for p in pieces[1:]:
            m = jnp.maximum(m, p)
        m = jnp.max(m, axis=0, keepdims=True)
        pieces = [jnp.where(p == m, _rank_code(0), p) for p in pieces]
    counts = jnp.zeros((PEER_TOPK, tb), F32)
    z = jnp.zeros((1, tb), F32)
    for p, (r1, c0, valid) in zip(pieces, _CAND_PIECES):
        taken = p < _TAKEN_BELOW
        if r1 is None:
            counts = counts + jnp.concatenate([jnp.zeros((8, tb), F32), jnp.where(taken, 1.0, 0.0)], axis=0)
        else:
            n = jnp.sum(jnp.where(taken, 1.0, 0.0), axis=0, keepdims=True)
            counts = counts + jnp.where(row16 == r1, n, 0.0)
        z = z + jnp.sum(jnp.where(taken, jnp.exp(cells(r1, c0) - top), 0.0), axis=0, keepdims=True)
    n1 = jnp.zeros((nk, tb), F32)
    for r in range(PEER_TOPK):
        n1 = jnp.where(code1 == _rank_code(r), counts[r:r + 1], n1)
    ea_ref[h] = jnp.exp(s1 - v1[0:1])
    n1_ref[h] = n1
    rank2 = jnp.where(code2 < _TAKEN_BELOW, code2 * (-1.0 / _CODE_UNIT) - 64.0, float(nk))
    r2_ref[h] = rank2.astype(BF16)
    eb_ref[h] = (jnp.exp(s2 - v2[0:1]) / z).astype(BF16)


def _router_kernel(ht_ref, wq_ref, k1_ref, k2_ref, ea_ref, n1_ref, r2_ref, eb_ref):
    ht = ht_ref[...]

    group = 8

    def heads(p, carry):
        for hh in range(group):
            _router_head(group * p + hh, ht, wq_ref, k1_ref, k2_ref, ea_ref, n1_ref, r2_ref, eb_ref)
        return carry

    lax.fori_loop(0, PEER_HEADS // group, heads, 0)


def _router(ht, wq_t, k1, k2):
    d, t = ht.shape
    tb = TILE_ROUTER_TOKENS
    full = lambda a: pl.BlockSpec(a.shape, lambda i: (0,) * a.ndim)
    out = pl.BlockSpec((PEER_HEADS, PEER_N_KEYS, tb), lambda i: (0, 0, i))
    return pl.pallas_call(
        _router_kernel,
        grid=(t // tb,),
        in_specs=[pl.BlockSpec((d, tb), lambda i: (0, i)), full(wq_t), full(k1), full(k2)],
        out_specs=[out, out, out, out],
        out_shape=[jax.ShapeDtypeStruct((PEER_HEADS, PEER_N_KEYS, t), dt) for dt in (F32, F32, BF16, BF16)],
        compiler_params=_params("parallel"),
        name="peer_router",
    )(ht, wq_t, k1, k2)


def _gelu_tanh(x):
    k = 2.0 * math.sqrt(2.0 / math.pi) * math.log2(math.e)
    return x / (1.0 + jnp.exp2(x * (-k - (k * 0.044715) * (x * x))))


def _peer_kernel(ht_ref, u_ref, vt_ref, ea_ref, n1_ref, r2_ref, eb_ref, x_ref, mod_ref, fg_ref, o_ref, acc_ref, wa_ref,
                 wb_ref, *, final, chunk, n_e):
    e = pl.program_id(1)
    nk = PEER_N_KEYS
    n_i1 = u_ref.shape[0] // nk
    sub = PACKED_ROWS
    assert n_i1 == 4

    def step(write_ref, read_ref, base):
        def tokens(c, carry):
            lanes = pl.ds(pl.multiple_of(c * chunk, chunk), chunk)
            if write_ref is not None:
                act = _gelu_tanh(_dot(u_ref[...], ht_ref[:, lanes])).astype(BF16)
            if read_ref is not None:
                acc_ref[:, lanes] += _dot(vt_ref[...], read_ref[:, lanes])
            if write_ref is None:
                return carry
            for j in range(n_i1):
                m = None
                for h in range(PEER_HEADS):
                    row = (h, slice(base + j, base + j + 1), lanes)
                    n_row = jnp.broadcast_to(n1_ref[row], (sub, chunk)).astype(BF16)
                    ea_row = jnp.broadcast_to(ea_ref[row], (sub, chunk)).astype(BF16)
                    r2 = r2_ref[h, :, lanes].reshape(nk // sub, sub, chunk)
                    eb = eb_ref[h, :, lanes].reshape(nk // sub, sub, chunk)
                    term = jnp.where(r2 < n_row[None], eb, jnp.zeros((), BF16)) * ea_row[None]
                    m = term if m is None else m + term
                write_ref[j * nk:(j + 1) * nk, lanes] = m.reshape(nk, chunk) * act[j * nk:(j + 1) * nk]
            return carry

        lax.fori_loop(0, ht_ref.shape[1] // chunk, tokens, 0)

    @pl.when(e == 0)
    def _():
        acc_ref[...] = jnp.zeros_like(acc_ref)
        step(wa_ref, None, 0)

    @pl.when((e > 0) & (e < n_e) & (lax.rem(e, 2) == 0))
    def _():
        step(wa_ref, wb_ref, 0)

    @pl.when((e < n_e) & (lax.rem(e, 2) == 1))
    def _():
        step(wb_ref, wa_ref, n_i1)

    @pl.when(e == n_e)
    def _():
        step(None, wa_ref if n_e % 2 else wb_ref, 0)
        y = x_ref[...] + mod_ref[0, 5:6, :] * acc_ref[...].T
        if final:
            y = _rms(y, fg_ref[...])
        o_ref[...] = y


def _peer(ht, u_bf, vt_bf, ea, n1, r2, eb, x1, mod_l, final_g, seq, final):
    d, t = ht.shape
    n_exp = u_bf.shape[0]
    tb = TILE_PEER_TOKENS if seq % TILE_PEER_TOKENS == 0 else TILE_PEER_TOKENS // 2
    eb_blk = TILE_PEER_EXPERTS
    n_e = n_exp // eb_blk
    n_i1 = eb_blk // PEER_N_KEYS
    tpb = seq // tb
    cur = lambda e: jnp.minimum(e, n_e - 1)
    i1_spec = pl.BlockSpec((PEER_HEADS, 8, tb), lambda i, e: (0, cur(e) // (8 // n_i1), i))
    tok3 = pl.BlockSpec((PEER_HEADS, PEER_N_KEYS, tb), lambda i, e: (0, 0, i))
    return pl.pallas_call(
        functools.partial(_peer_kernel, final=final, chunk=min(TILE_PEER_CHUNK, tb), n_e=n_e),
        grid=(t // tb, n_e + 1),
        in_specs=[pl.BlockSpec((d, tb), lambda i, e: (0, i)),
                  pl.BlockSpec((eb_blk, d), lambda i, e: (cur(e), 0)),
                  pl.BlockSpec((d, eb_blk), lambda i, e: (0, jnp.maximum(e - 1, 0))),
                  i1_spec, i1_spec, tok3, tok3,
                  pl.BlockSpec((tb, d), lambda i, e: (i, 0)),
                  pl.BlockSpec((1, 6, d), lambda i, e: (i // tpb, 0, 0)),
                  pl.BlockSpec(final_g.shape, lambda i, e: (0, 0))],
        out_specs=pl.BlockSpec((tb, d), lambda i, e: (i, 0)),
        out_shape=jax.ShapeDtypeStruct((t, d), F32),
        scratch_shapes=[pltpu.VMEM((d, tb), F32), pltpu.VMEM((eb_blk, tb), BF16), pltpu.VMEM((eb_blk, tb), BF16)],
        compiler_params=_params("parallel", "arbitrary"),
        name="peer_experts",
    )(ht, u_bf, vt_bf, ea, n1, r2, eb, x1, mod_l, final_g)


def kernel(x, c, ada_w, ada_b, norm_mix_g, norm_ffn_g, w_in, nsa_cmp_pos_k, nsa_cmp_pos_v, nsa_cmp_wk, nsa_cmp_wv, diff_lam_q1, diff_lam_k1, diff_lam_q2, diff_lam_k2, diff_sub_g, mla_q_norm_g, mla_w_uq, mla_kv_norm_g, mla_w_ukv, swa_sinks, w_out, peer_w_q, peer_sub_k1, peer_sub_k2, peer_u, peer_v, final_g):
    batch, seq, d = x.shape
    depth = w_in.shape[0]
    assert seq % 512 == 0
    x2d = x.reshape(batch * seq, d)
    mod = _adaln(c, ada_w, ada_b).reshape(depth, batch, 6, d)
    table = _rope_table(seq)
    mixw = 4 * HEAD_DIM
    swa_rows = 3 * mixw + np.concatenate([h * HEAD_DIM + np.arange(HEAD_DIM) for h in (0, 2, 1, 3)])
    out_rows = jnp.asarray(np.concatenate([np.arange(3 * mixw), swa_rows]), jnp.int32)
    pad128 = lambda v: jnp.pad(v, (0, LANES - v.shape[0])).reshape(1, LANES)
    fg = final_g.reshape(1, d)
    w_big = _take_cols(w_in, _IN_IDX, _IN_SGN)
    wuq = _take_cols(mla_w_uq, _UQ_IDX, _UQ_SGN)
    wukv = _take_cols(mla_w_ukv, _UKV_IDX, _UKV_SGN)
    for l in range(depth):
        (nq, nk, nv, ng, dq, dk, dv, mq, mk, mv, sq, sk, sv) = _inproj(
            x2d, mod[l], norm_mix_g[l].reshape(1, d), w_big[l], table, wuq[l], wukv[l],
            mla_q_norm_g[l].reshape(1, -1), mla_kv_norm_g[l].reshape(1, -1), seq)
        ocmp, sel = _nsa_cmp(nq, nk, nv, nsa_cmp_wk[l], nsa_cmp_wv[l], nsa_cmp_pos_k[l], nsa_cmp_pos_v[l], batch, seq)
        o_a = _nsa(nq, nk, nv, sel, ng, ocmp, batch, seq)
        lamv = jnp.concatenate([pad128(diff_lam_q1[l]), pad128(diff_lam_k1[l]),
                                pad128(diff_lam_q2[l]), pad128(diff_lam_k2[l])], axis=0)
        sub_g2 = jnp.concatenate([diff_sub_g[l], diff_sub_g[l]]).reshape(1, LANES)
        o_b = _diff(dq, dk, dv, lamv, sub_g2, l, batch, seq)
        o_c = _mla(mq, mk, mv, batch, seq)
        o_d = _swa(sq, sk, sv, pad128(swa_sinks[l]), batch, seq)
        w_o = jnp.take(w_out[l], out_rows, axis=0).astype(BF16)
        x1, ht = _outproj(x2d, mod[l], norm_ffn_g[l].reshape(1, d), o_a, o_b, o_c, o_d, w_o, seq)
        ea, n1, r2, eb = _router(ht, peer_w_q[l].T.astype(BF16), peer_sub_k1[l].astype(BF16),
                                 peer_sub_k2[l].astype(BF16))
        x2d = _peer(ht, peer_u[l].astype(BF16), peer_v[l].T.astype(BF16), ea, n1, r2, eb, x1, mod[l], fg,
                    seq, final=(l == depth - 1))
    return x2d.reshape(batch, seq, d)
```

```python
import functools
import math

import numpy as np
import jax
import jax.numpy as jnp
from jax import lax
from jax.experimental import pallas as pl
from jax.experimental.pallas import tpu as pltpu

F32 = jnp.float32
BF16 = jnp.bfloat16

HEAD_DIM = 64
ROPE_THETA = 10000.0
EPS = 1e-6
NEG = -1e30
FORCE = 1e4

NSA_HEADS = 4
NSA_CMP_LEN = 32
NSA_CMP_STRIDE = 16
NSA_SEL_LEN = 64
NSA_TOP_N = 16
NSA_WINDOW = 512

DIFF_HEADS = 4
DIFF_QK_DIM = 32
DIFF_V_DIM = 64

MLA_HEADS = 4
MLA_Q_RANK = 256
MLA_KV_RANK = 128
MLA_NOPE_DIM = 64
MLA_ROPE_DIM = 32
MLA_V_DIM = 64

SWA_HEADS = 4
SWA_KV_HEADS = 2
SWA_WINDOW = 128

PEER_HEADS = 8
PEER_N_KEYS = 128
PEER_TOPK = 16
PEER_QUERY_DIM = 256

LOG2E = math.log2(math.e)
LANES = 128
PACKED_ROWS = 16
VMEM_LIMIT = 56 * 1024 * 1024

TILE_PROJ_ROWS = 256
TILE_OUTPROJ_ROWS = 512
TILE_ADALN_COLS = 1536
TILE_CMP_ROWS = 1024
TILE_NSA = (512, 512)
TILE_DIFF = (256, 512)
TILE_MLA = (512, 512)
TILE_SWA = (512, 128)
TILE_ROUTER_TOKENS = 256
TILE_PEER_TOKENS = 1024
TILE_PEER_EXPERTS = 512
TILE_PEER_CHUNK = 1024


def _dot(a, b):
    return jnp.dot(a, b, preferred_element_type=F32)


def _dot_nt(a, b):
    return lax.dot_general(a, b, (((1,), (1,)), ((), ())), preferred_element_type=F32)


def _params(*sem):
    return pltpu.CompilerParams(dimension_semantics=sem, vmem_limit_bytes=VMEM_LIMIT)


def _rms(x, g):
    return x * lax.rsqrt(jnp.mean(x * x, axis=-1, keepdims=True) + EPS) * g


def _rot_idx(base, dim):
    half = dim // 2
    idx = np.concatenate([base + half + np.arange(half), base + np.arange(half)])
    sgn = np.concatenate([-np.ones(half), np.ones(half)])
    return idx, sgn


def _in_plan():
    d = HEAD_DIM
    nsa0 = 0
    nsa_cols = NSA_HEADS * d + 6 * d + 3 * NSA_HEADS
    diff0 = nsa0 + nsa_cols
    diff_cols = 2 * DIFF_HEADS * 2 * DIFF_QK_DIM + DIFF_HEADS * DIFF_V_DIM
    mla0 = diff0 + diff_cols
    mla_cols = MLA_Q_RANK + MLA_KV_RANK + MLA_ROPE_DIM
    swa0 = mla0 + mla_cols
    idx, sgn, off = [], [], {}

    groups = {}

    def add(name, i, s=None):
        i = np.asarray(i, np.int64)
        s = np.ones(len(i)) if s is None else np.asarray(s, np.float64)
        pad = (-len(i)) % LANES
        groups[name] = (np.concatenate([i, np.zeros(pad, np.int64)]), np.concatenate([s, np.zeros(pad)]))

    def heads_rot(base, nheads, dim):
        ii, ss = zip(*[_rot_idx(base + h * dim, dim) for h in range(nheads)])
        return np.concatenate(ii), np.concatenate(ss)

    nq = nsa0 + np.arange(NSA_HEADS * d)
    add("nq", nq)
    add("nqr", *heads_rot(nsa0, NSA_HEADS, d))
    kb = nsa0 + NSA_HEADS * d
    kc, vc, ksl, vsl, kw, vw = [kb + j * d for j in range(6)]
    dup = lambda b: np.concatenate([b + np.arange(d), b + np.arange(d)])
    add("nk", np.concatenate([dup(kc), dup(ksl), dup(kw)]))
    kr = [_rot_idx(b, d) for b in (kc, kc, ksl, ksl, kw, kw)]
    add("nkr", np.concatenate([a for a, _ in kr]), np.concatenate([b for _, b in kr]))
    add("nv", np.concatenate([dup(vc), dup(vsl), dup(vw)]))
    add("ng", kb + 6 * d + np.arange(3 * NSA_HEADS))
    nqk = DIFF_HEADS * 2 * DIFF_QK_DIM
    add("dq", diff0 + np.arange(nqk))
    add("dqr", *heads_rot(diff0, 2 * DIFF_HEADS, DIFF_QK_DIM))
    add("dk", diff0 + nqk + np.arange(nqk))
    add("dkr", *heads_rot(diff0 + nqk, 2 * DIFF_HEADS, DIFF_QK_DIM))
    add("dv", diff0 + 2 * nqk + np.arange(DIFF_HEADS * DIFF_V_DIM))
    add("mcq", mla0 + np.arange(MLA_Q_RANK))
    add("mckv", mla0 + MLA_Q_RANK + np.arange(MLA_KV_RANK))
    kr0 = mla0 + MLA_Q_RANK + MLA_KV_RANK
    z64 = np.zeros(MLA_NOPE_DIM, np.int64)
    add("mkr", np.concatenate([z64, kr0 + np.arange(MLA_ROPE_DIM)]),
        np.concatenate([np.zeros(MLA_NOPE_DIM), np.ones(MLA_ROPE_DIM)]))
    ri, rs = _rot_idx(kr0, MLA_ROPE_DIM)
    add("mkrr", np.concatenate([z64, ri]), np.concatenate([np.zeros(MLA_NOPE_DIM), rs]))
    order = [0, 2, 1, 3]
    add("sq", np.concatenate([swa0 + h * d + np.arange(d) for h in order]))
    sr = [_rot_idx(swa0 + h * d, d) for h in order]
    add("sqr", np.concatenate([a for a, _ in sr]), np.concatenate([b for _, b in sr]))
    sk0 = swa0 + SWA_HEADS * d
    add("sk", sk0 + np.arange(SWA_KV_HEADS * d))
    add("skr", *heads_rot(sk0, SWA_KV_HEADS, d))
    add("sv", sk0 + SWA_KV_HEADS * d + np.arange(SWA_KV_HEADS * d))
    order = ["nq", "nqr", "nk", "ng", "nkr", "mckv", "nv", "sk", "dq", "dqr", "dk", "dkr", "dv", "mcq",
             "mkr", "mkrr", "skr", "sv", "sq", "sqr"]
    assert sorted(order) == sorted(groups)
    blk, run, pos = {}, [], 0
    for name in order:
        off[name] = pos
        idx.append(groups[name][0])
        sgn.append(groups[name][1])
        run.append(name)
        pos += len(groups[name][0])
        if pos % (2 * LANES) == 0:
            start = off[run[0]]
            blk.update({n: (start, pos - start) for n in run})
            run = []
    assert not run
    return np.concatenate(idx), np.concatenate(sgn), off, blk


_IN_IDX, _IN_SGN, _OFF, _BLK = _in_plan()


def _mla_plans():
    qd = MLA_NOPE_DIM + MLA_ROPE_DIM
    qi, qs, ri, rs = [], [], [], []
    for h in range(MLA_HEADS):
        b = h * qd
        qi += [b + np.arange(qd), np.zeros(LANES - qd, np.int64)]
        qs += [np.ones(qd), np.zeros(LANES - qd)]
        a, s = _rot_idx(b + MLA_NOPE_DIM, MLA_ROPE_DIM)
        ri += [np.zeros(MLA_NOPE_DIM, np.int64), a, np.zeros(LANES - qd, np.int64)]
        rs += [np.zeros(MLA_NOPE_DIM), s, np.zeros(LANES - qd)]
    kd = MLA_NOPE_DIM + MLA_V_DIM
    ki, ks, vi = [], [], []
    for h in range(MLA_HEADS):
        ki += [h * kd + np.arange(MLA_NOPE_DIM), np.zeros(LANES - MLA_NOPE_DIM, np.int64)]
        ks += [np.ones(MLA_NOPE_DIM), np.zeros(LANES - MLA_NOPE_DIM)]
        vi += [h * kd + MLA_NOPE_DIM + np.arange(MLA_V_DIM)]
    uq_idx = np.concatenate(qi + ri)
    uq_sgn = np.concatenate(qs + rs)
    ukv_idx = np.concatenate(ki + vi)
    ukv_sgn = np.concatenate(ks + [np.ones(MLA_HEADS * MLA_V_DIM)])
    return uq_idx, uq_sgn, ukv_idx, ukv_sgn


_UQ_IDX, _UQ_SGN, _UKV_IDX, _UKV_SGN = _mla_plans()


def _take_cols(w, idx, sgn):
    pieces, start = [], 0
    for i in range(1, len(idx) + 1):
        same_run = (i < len(idx) and sgn[i] == sgn[start]
                    and (sgn[i] == 0.0 or idx[i] == idx[i - 1] + 1))
        if not same_run:
            n, s = i - start, float(sgn[start])
            run = w[..., int(idx[start]):int(idx[start]) + n]
            pieces.append(jnp.zeros(w.shape[:-1] + (n,), w.dtype) if s == 0.0 else (run if s == 1.0 else -run))
            start = i
    return jnp.concatenate(pieces, axis=-1).astype(BF16)


def _rope_table(seq):
    def cs(dim):
        inv = 1.0 / (ROPE_THETA ** (jnp.arange(0, dim, 2, dtype=F32) / dim))
        ang = jnp.arange(seq, dtype=F32)[:, None] * inv[None, :]
        c, s = jnp.cos(ang), jnp.sin(ang)
        return jnp.concatenate([c, c], 1), jnp.concatenate([s, s], 1)
    ch, sh = cs(HEAD_DIM)
    cd, sd = cs(DIFF_QK_DIM)
    cm, sm = cs(MLA_ROPE_DIM)
    one = jnp.ones((seq, MLA_NOPE_DIM), F32)
    z64 = jnp.zeros((seq, MLA_NOPE_DIM), F32)
    z32 = jnp.zeros((seq, LANES - MLA_NOPE_DIM - MLA_ROPE_DIM), F32)
    parts = [jnp.tile(ch, (1, 2)), jnp.tile(sh, (1, 2)), jnp.tile(cd, (1, 4)), jnp.tile(sd, (1, 4)),
             jnp.concatenate([one, cm, z32], 1), jnp.concatenate([z64, sm, z32], 1),
             jnp.concatenate([z64, cm, z32], 1), jnp.concatenate([z64, sm, z32], 1)]
    return jnp.concatenate(parts, 1)


def _adaln_kernel(c_ref, w_ref, b_ref, o_ref):
    c = c_ref[...]
    sc = (c * jax.nn.sigmoid(c)).astype(BF16)
    o_ref[0] = _dot(sc, w_ref[0].astype(BF16)) + b_ref[0]


def _adaln(c, ada_w, ada_b):
    nl, d, n6 = ada_w.shape
    b = c.shape[0]
    tn = TILE_ADALN_COLS
    return pl.pallas_call(
        _adaln_kernel,
        grid=(nl, n6 // tn),
        in_specs=[pl.BlockSpec((b, d), lambda l, j: (0, 0)),
                  pl.BlockSpec((1, d, tn), lambda l, j: (l, 0, j)),
                  pl.BlockSpec((1, 1, tn), lambda l, j: (l, 0, j))],
        out_specs=pl.BlockSpec((1, b, tn), lambda l, j: (l, 0, j)),
        out_shape=jax.ShapeDtypeStruct((nl, b, n6), F32),
        compiler_params=_params("parallel", "parallel"),
        name="adaln",
    )(c, ada_w, ada_b.reshape(nl, 1, n6))


def _inproj_kernel(x_ref, mod_ref, ng_ref, w_ref, tab_ref, wuq_ref, wukv_ref, gq_ref, gkv_ref,
                   nq_ref, nk_ref, nv_ref, ngo_ref, dq_ref, dk_ref, dv_ref,
                   mq_ref, mk_ref, mv_ref, sq_ref, sk_ref, sv_ref):
    x = x_ref[...]
    h = _rms(x, ng_ref[...]) * (1.0 + mod_ref[0, 1:2, :]) + mod_ref[0, 0:1, :]
    hb = h.astype(BF16)

    runs = {}

    def mm(name, width):
        start, size = _BLK[name]
        if start not in runs:
            runs[start] = _dot(hb, w_ref[:, start:start + size])
        o = _OFF[name] - start
        return runs[start][:, o:o + width]

    def tab(j, reps):
        t = tab_ref[:, j * LANES:(j + 1) * LANES]
        return t if reps == 1 else jnp.concatenate([t] * reps, axis=1)

    def rope(name, rname, width, cj, scale=1.0):
        r = mm(name, width) * tab(cj, width // LANES) + mm(rname, width) * tab(cj + 1, width // LANES)
        return r if scale == 1.0 else r * scale

    d = HEAD_DIM
    nq_ref[...] = rope("nq", "nqr", 256, 0, LOG2E * d ** -0.5).astype(BF16)
    nk_ref[...] = rope("nk", "nkr", 384, 0).astype(BF16)
    nv_ref[...] = mm("nv", 384).astype(BF16)
    ngo_ref[...] = jax.nn.sigmoid(mm("ng", LANES))
    dq_ref[...] = rope("dq", "dqr", 256, 2, LOG2E * DIFF_QK_DIM ** -0.5).astype(BF16)
    dk_ref[...] = rope("dk", "dkr", 256, 2).astype(BF16)
    dv_ref[...] = mm("dv", 256).astype(BF16)
    cq = _rms(mm("mcq", MLA_Q_RANK), gq_ref[...]).astype(BF16)
    nh = MLA_HEADS * LANES
    qa = _dot(cq, wuq_ref[:, 0:nh])
    qb = _dot(cq, wuq_ref[:, nh:2 * nh])
    mq = (qa * tab(4, MLA_HEADS) + qb * tab(5, MLA_HEADS)) * (LOG2E * (MLA_NOPE_DIM + MLA_ROPE_DIM) ** -0.5)
    mq_ref[...] = mq.astype(BF16)
    ckv = _rms(mm("mckv", MLA_KV_RANK), gkv_ref[...]).astype(BF16)
    kk = _dot(ckv, wukv_ref[:, 0:nh])
    kr = mm("mkr", LANES) * tab(6, 1) + mm("mkrr", LANES) * tab(7, 1)
    mk_ref[...] = (kk + jnp.concatenate([kr] * MLA_HEADS, axis=1)).astype(BF16)
    mv_ref[...] = _dot(ckv, wukv_ref[:, nh:nh + MLA_HEADS * MLA_V_DIM]).astype(BF16)
    sq_ref[...] = rope("sq", "sqr", 256, 0, LOG2E * d ** -0.5).astype(BF16)
    sk_ref[...] = rope("sk", "skr", 128, 0).astype(BF16)
    sv_ref[...] = mm("sv", 128).astype(BF16)


def _inproj(x2d, mod_l, norm_g, w_big, table, wuq, wukv, gq, gkv, seq):
    t, d = x2d.shape
    tm = TILE_PROJ_ROWS
    tpb = seq // tm
    widths = [256, 384, 384, 128, 256, 256, 256, 512, 512, 256, 256, 128, 128]
    dts = [BF16, BF16, BF16, F32, BF16, BF16, BF16, BF16, BF16, BF16, BF16, BF16, BF16]
    full = lambda a: pl.BlockSpec(a.shape, lambda i: (0,) * a.ndim)
    return pl.pallas_call(
        _inproj_kernel,
        grid=(t // tm,),
        in_specs=[pl.BlockSpec((tm, d), lambda i: (i, 0)),
                  pl.BlockSpec((1, 6, d), lambda i: (i // tpb, 0, 0)),
                  full(norm_g), full(w_big),
                  pl.BlockSpec((tm, table.shape[1]), lambda i: (i % tpb, 0)),
                  full(wuq), full(wukv), full(gq), full(gkv)],
        out_specs=[pl.BlockSpec((tm, w), lambda i: (i, 0)) for w in widths],
        out_shape=[jax.ShapeDtypeStruct((t, w), dt) for w, dt in zip(widths, dts)],
        compiler_params=_params("parallel"),
        name="inproj",
    )(x2d, mod_l, norm_g, w_big, table, wuq, wukv, gq, gkv)


def _lane_mask(lo, hi):
    lane = lax.broadcasted_iota(jnp.int32, (1, LANES), 1)
    return (lane >= lo) & (lane < hi)


def _masked(q, lo, hi):
    return jnp.where(_lane_mask(lo, hi), q, jnp.zeros_like(q))


def _chain(q, k, v, mask, state, acc_ref, c):
    m, l = state
    s = _dot_nt(q, k)
    if mask is not None:
        s = jnp.where(mask, s, NEG)
    m2 = jnp.maximum(m, jnp.max(s, axis=-1, keepdims=True))
    a = jnp.exp2(m - m2)
    p = jnp.exp2(s - m2)
    acc_ref[c] = a * acc_ref[c] + _dot(p.astype(BF16), v)
    return m2, a * l + jnp.sum(p, axis=-1, keepdims=True)


def _init_state(n, rows):
    return tuple((jnp.full((rows, 1), NEG, F32), jnp.zeros((rows, 1), F32)) for _ in range(n))


def _ktile(ref, j, tk, c0, c1):
    return ref[pl.ds(pl.multiple_of(j * tk, tk), tk), c0:c1]


def _qpos(i, tq, reps=1):
    p = i * tq + lax.broadcasted_iota(jnp.int32, (tq, 1), 0)
    return p if reps == 1 else jnp.concatenate([p] * reps, axis=0)


def _kpos(j, tk):
    return j * tk + lax.broadcasted_iota(jnp.int32, (1, tk), 1)


def _half_heads(q):
    return [_masked(q[:, c * LANES:(c + 1) * LANES], 64 * hh, 64 * hh + 64) for c in range(2) for hh in range(2)]


def _pair(lo_val, hi_val):
    return jnp.where(_lane_mask(0, 64), lo_val, hi_val)


def _nsa_cmp_kernel(q_ref, kc_ref, vc_ref, wk_ref, wv_ref, pk_ref, pv_ref, ov_ref, oc_ref, sel_ref, *, top_n, n_sel):
    half = wk_ref.shape[1]

    def compress(x_ref, w_ref, p_ref):
        x = x_ref[...]
        a = _dot(x, w_ref[0])
        b = _dot(x, w_ref[1])
        p = jnp.broadcast_to(p_ref[...], (8, 2 * half)).astype(BF16)
        const = (_dot(p[:, 0:half], w_ref[0]) + _dot(p[:, half:2 * half], w_ref[1]))[0:1]
        return a + jnp.concatenate([b[1:], b[:1]], axis=0) + const

    kcmp = compress(kc_ref, wk_ref, pk_ref).astype(BF16)
    vcmp = compress(vc_ref, wv_ref, pv_ref).astype(BF16)
    ncp = kcmp.shape[0]
    ov = ov_ref[...]
    rb = math.gcd(TILE_CMP_ROWS, q_ref.shape[0])
    cend = NSA_CMP_STRIDE * lax.broadcasted_iota(jnp.int32, (1, ncp), 1) + (NSA_CMP_LEN - 1)
    lane = lax.broadcasted_iota(jnp.int32, (1, LANES), 1)

    def block(r, carry):
        r0 = pl.multiple_of(r * rb, rb)
        q = q_ref[pl.ds(r0, rb), :]
        tpos = r0 + lax.broadcasted_iota(jnp.int32, (rb, 1), 0)
        vis = cend <= tpos
        psum = jnp.zeros((rb, ncp), F32)
        outs = []
        for c in range(2):
            halves = []
            for hh in range(2):
                qm = _masked(q[:, c * LANES:(c + 1) * LANES], 64 * hh, 64 * hh + 64)
                s = jnp.where(vis, _dot_nt(qm, kcmp), NEG)
                e = jnp.exp2(s - jnp.max(s, axis=-1, keepdims=True))
                p = jnp.where(vis, e / jnp.sum(e, axis=-1, keepdims=True), 0.0)
                psum = psum + p
                halves.append(_dot(p.astype(BF16), vcmp))
            outs.append(_pair(halves[0], halves[1]))
        oc_ref[pl.ds(r0, rb), :] = jnp.concatenate(outs, axis=1)
        hi = psum.astype(BF16)
        lo = (psum - hi.astype(F32)).astype(BF16)
        imp = _dot(hi, ov) + _dot(lo, ov)
        nsp = -(-n_sel // 8) * 8
        imp_t = imp.T[0:nsp]
        blk = lax.broadcasted_iota(jnp.int32, (nsp, 1), 0)
        qblk = (r0 + lax.broadcasted_iota(jnp.int32, (1, rb), 1)) // NSA_SEL_LEN
        allowed = blk <= qblk
        forced = (blk == 0) | (blk == qblk) | (blk == qblk - 1)
        impf = jnp.where(allowed, jnp.where(forced, FORCE, imp_t), NEG)
        rank = jnp.zeros((nsp, rb), F32)
        for j in range(n_sel):
            row = impf[j:j + 1, :]
            rank = rank + jnp.where(blk > j, jnp.where(row >= impf, 1.0, 0.0), jnp.where(row > impf, 1.0, 0.0))
        sel_t = jnp.where(allowed & (rank < top_n), 1.0, 0.0)
        if nsp < LANES:
            sel_t = jnp.concatenate([sel_t, jnp.zeros((LANES - nsp, rb), F32)], axis=0)
        sel_ref[pl.ds(r0, rb), :] = sel_t.T.astype(BF16)
        return carry

    lax.fori_loop(0, q_ref.shape[0] // rb, block, 0)


def _nsa_cmp(nq, nk, nv, wk, wv, pos_k, pos_v, batch, seq):
    d = HEAD_DIM
    nc = seq // NSA_CMP_STRIDE
    ncp = -(-nc // LANES) * LANES
    n_sel = seq // NSA_SEL_LEN
    assert n_sel <= LANES
    top_n = min(NSA_TOP_N, n_sel)

    def seg(a):
        a = a[:, :d].reshape(batch, nc, NSA_CMP_STRIDE * d)
        return jnp.pad(a, ((0, 0), (0, ncp - nc), (0, 0))).reshape(batch * ncp, NSA_CMP_STRIDE * d)

    half = NSA_CMP_STRIDE * d
    dupw = lambda w: jnp.concatenate([w, w], axis=1).reshape(2, half, 2 * d).astype(BF16)
    cpos = NSA_CMP_STRIDE * np.arange(ncp)[:, None] + np.arange(NSA_CMP_LEN)[None, :]
    ovl = np.zeros((ncp, LANES), np.float32)
    for j in range(n_sel):
        ovl[:, j] = (cpos // NSA_SEL_LEN == j).mean(axis=1)
    ovl[nc - 1:, :] = 0.0
    full = lambda a: pl.BlockSpec(a.shape, lambda b: (0,) * a.ndim)
    wk2, wv2 = dupw(wk), dupw(wv)
    pk, pv = pos_k.reshape(1, -1), pos_v.reshape(1, -1)
    ov = jnp.asarray(ovl, BF16)
    return pl.pallas_call(
        functools.partial(_nsa_cmp_kernel, top_n=top_n, n_sel=n_sel),
        grid=(batch,),
        in_specs=[pl.BlockSpec((seq, 256), lambda b: (b, 0)),
                  pl.BlockSpec((ncp, half), lambda b: (b, 0)),
                  pl.BlockSpec((ncp, half), lambda b: (b, 0)),
                  full(wk2), full(wv2), full(pk), full(pv), full(ov)],
        out_specs=[pl.BlockSpec((seq, 256), lambda b: (b, 0)),
                   pl.BlockSpec((seq, LANES), lambda b: (b, 0))],
        out_shape=[jax.ShapeDtypeStruct((batch * seq, 256), F32),
                   jax.ShapeDtypeStruct((batch * seq, LANES), BF16)],
        compiler_params=_params("parallel"),
        name="nsa_cmp",
    )(nq, seg(nk), seg(nv), wk2, wv2, pk, pv, ov)


def _nsa_kernel(q_ref, k_ref, v_ref, sel_ref, g_ref, oc_ref, e_ref, o_ref, acc_ref, *, tq, tk):
    i = pl.program_id(1)
    qh = _half_heads(q_ref[...])
    qp = _qpos(i, tq)
    sel = sel_ref[...]
    nh = NSA_HEADS
    acc_ref[...] = jnp.zeros_like(acc_ref)

    def sel_step(j, st, diag):
        mv = _dot(sel, e_ref[:, pl.ds(pl.multiple_of(j * tk, tk), tk)])
        if diag:
            mv = jnp.where(_kpos(j, tk) <= qp, mv, 0.0)
        mask = mv > 0.5
        k = _ktile(k_ref, j, tk, 128, 256)
        v = _ktile(v_ref, j, tk, 128, 256)
        return tuple(_chain(qh[h], k, v, mask, st[h], acc_ref, h) for h in range(nh))

    nfull = (i * tq) // tk
    st = lax.fori_loop(0, nfull, lambda j, s: sel_step(j, s, False), _init_state(nh, tq))
    st_sel = sel_step(nfull, st, True)

    def win_step(j, st):
        dist = qp - _kpos(j, tk)
        mask = jnp.where(dist >= 0, dist, NSA_WINDOW) < NSA_WINDOW
        k = _ktile(k_ref, j, tk, 256, 384)
        v = _ktile(v_ref, j, tk, 256, 384)
        return tuple(_chain(qh[h], k, v, mask, st[h], acc_ref, nh + h) for h in range(nh))

    wlo = jnp.maximum(i * tq - NSA_WINDOW, 0) // tk
    st_win = lax.fori_loop(wlo, nfull + 1, win_step, _init_state(nh, tq))
    g = g_ref[...]
    oc = oc_ref[...]
    outs = []
    for c in range(2):
        occ = oc[:, c * LANES:(c + 1) * LANES]

        def comb(h):
            o_sel = acc_ref[h] / st_sel[h][1]
            o_win = acc_ref[nh + h] / st_win[h][1]
            return g[:, 3 * h:3 * h + 1] * occ + g[:, 3 * h + 1:3 * h + 2] * o_sel + g[:, 3 * h + 2:3 * h + 3] * o_win

        outs.append(_pair(comb(2 * c), comb(2 * c + 1)))
    o_ref[...] = jnp.concatenate(outs, axis=1).astype(BF16)


def _nsa(nq, nk, nv, sel, gates, ocmp, batch, seq):
    tq, tk = TILE_NSA
    nb = seq // tq
    expand = np.zeros((LANES, seq), np.float32)
    for j in range(seq // NSA_SEL_LEN):
        expand[j, j * NSA_SEL_LEN:(j + 1) * NSA_SEL_LEN] = 1.0
    e = jnp.asarray(expand, BF16)
    row = lambda w: pl.BlockSpec((tq, w), lambda b, i: (b * nb + i, 0))
    per_b = lambda w: pl.BlockSpec((seq, w), lambda b, i: (b, 0))
    return pl.pallas_call(
        functools.partial(_nsa_kernel, tq=tq, tk=tk),
        grid=(batch, nb),
        in_specs=[row(256), per_b(384), per_b(384), row(LANES), row(LANES), row(256),
                  pl.BlockSpec(e.shape, lambda b, i: (0, 0))],
        out_specs=row(256),
        out_shape=jax.ShapeDtypeStruct((batch * seq, 256), BF16),
        scratch_shapes=[pltpu.VMEM((2 * NSA_HEADS, tq, LANES), F32)],
        compiler_params=_params("parallel", "arbitrary"),
        name="nsa_attn",
    )(nq, nk, nv, sel, gates, ocmp, e)


def _diff_kernel(q_ref, k_ref, v_ref, lam_ref, sg_ref, o_ref, acc_ref, *, lam_init, tq, tk):
    i = pl.program_id(1)
    q = q_ref[...]
    qp4 = _qpos(i, tq, 4)
    lv = lam_ref[...]
    lam = (jnp.exp(jnp.sum(lv[0:1] * lv[1:2], axis=-1, keepdims=True))
           - jnp.exp(jnp.sum(lv[2:3] * lv[3:4], axis=-1, keepdims=True)) + lam_init)
    qs = [jnp.concatenate([_masked(q[:, c * LANES:(c + 1) * LANES], 32 * t, 32 * t + 32) for t in range(4)], axis=0)
          for c in range(2)]
    acc_ref[...] = jnp.zeros_like(acc_ref)

    def step(j, st, diag):
        mask = (_kpos(j, tk) <= qp4) if diag else None
        return tuple(_chain(qs[c], _ktile(k_ref, j, tk, c * LANES, (c + 1) * LANES),
                            _ktile(v_ref, j, tk, c * LANES, (c + 1) * LANES), mask, st[c], acc_ref, c)
                     for c in range(2))

    nfull = (i * tq) // tk
    st = lax.fori_loop(0, nfull, lambda j, s: step(j, s, False), _init_state(2, 4 * tq))
    st = step(nfull, st, True)
    outs = []
    for c in range(2):
        o = acc_ref[c] / st[c][1]
        r = [o[t * tq:(t + 1) * tq] for t in range(4)]
        dd = _pair(r[0] - lam * r[1], r[2] - lam * r[3])
        sq = dd * dd
        lo = _lane_mask(0, 64)
        ms = _pair(jnp.sum(jnp.where(lo, sq, 0.0), axis=-1, keepdims=True),
                   jnp.sum(jnp.where(lo, 0.0, sq), axis=-1, keepdims=True)) * (1.0 / DIFF_V_DIM)
        outs.append(dd * lax.rsqrt(ms + EPS) * sg_ref[...] * (1.0 - lam_init))
    o_ref[...] = jnp.concatenate(outs, axis=1).astype(BF16)


def _diff(dq, dk, dv, lamv, sub_g2, layer, batch, seq):
    tq, tk = TILE_DIFF
    nb = seq // tq
    lam_init = 0.8 - 0.6 * math.exp(-0.3 * layer)
    row = lambda w: pl.BlockSpec((tq, w), lambda b, i: (b * nb + i, 0))
    per_b = lambda w: pl.BlockSpec((seq, w), lambda b, i: (b, 0))
    full = lambda a: pl.BlockSpec(a.shape, lambda b, i: (0,) * a.ndim)
    return pl.pallas_call(
        functools.partial(_diff_kernel, lam_init=lam_init, tq=tq, tk=tk),
        grid=(batch, nb),
        in_specs=[row(256), per_b(256), per_b(256), full(lamv), full(sub_g2)],
        out_specs=row(256),
        out_shape=jax.ShapeDtypeStruct((batch * seq, 256), BF16),
        scratch_shapes=[pltpu.VMEM((2, 4 * tq, LANES), F32)],
        compiler_params=_params("parallel", "arbitrary"),
        name="diff_attn",
    )(dq, dk, dv, lamv, sub_g2)


def _mla_kernel(q_ref, k_ref, v_ref, o_ref, acc_ref, *, tq, tk):
    i = pl.program_id(1)
    qp = _qpos(i, tq)
    nh = MLA_HEADS
    acc_ref[...] = jnp.zeros_like(acc_ref)

    def step(j, st, diag):
        mask = (_kpos(j, tk) <= qp) if diag else None
        return tuple(_chain(q_ref[:, h * LANES:(h + 1) * LANES], _ktile(k_ref, j, tk, h * LANES, (h + 1) * LANES),
                            _ktile(v_ref, j, tk, (h // 2) * LANES, (h // 2 + 1) * LANES), mask, st[h], acc_ref, h)
                     for h in range(nh))

    nfull = (i * tq) // tk
    st = lax.fori_loop(0, nfull, lambda j, s: step(j, s, False), _init_state(nh, tq))
    st = step(nfull, st, True)
    o = [acc_ref[h] / st[h][1] for h in range(nh)]
    o_ref[...] = jnp.concatenate([_pair(o[0], o[1]), _pair(o[2], o[3])], axis=1).astype(BF16)


def _mla(mq, mk, mv, batch, seq):
    tq, tk = TILE_MLA
    nb = seq // tq
    row = lambda w: pl.BlockSpec((tq, w), lambda b, i: (b * nb + i, 0))
    per_b = lambda w: pl.BlockSpec((seq, w), lambda b, i: (b, 0))
    return pl.pallas_call(
        functools.partial(_mla_kernel, tq=tq, tk=tk),
        grid=(batch, nb),
        in_specs=[row(512), per_b(512), per_b(256)],
        out_specs=row(256),
        out_shape=jax.ShapeDtypeStruct((batch * seq, 256), BF16),
        scratch_shapes=[pltpu.VMEM((MLA_HEADS, tq, LANES), F32)],
        compiler_params=_params("parallel", "arbitrary"),
        name="mla_attn",
    )(mq, mk, mv)


def _swa_kernel(q_ref, k_ref, v_ref, sink_ref, o_ref, *, tq, ts):
    i = pl.program_id(1)
    sk = sink_ref[...] * LOG2E
    for s in range(tq // ts):
        g = i * (tq // ts) + s
        k0 = pl.multiple_of(jnp.maximum(g - 1, 0) * ts, ts)
        k = k_ref[pl.ds(k0, 2 * ts), :]
        v = v_ref[pl.ds(k0, 2 * ts), :]
        dist = (g * ts + lax.broadcasted_iota(jnp.int32, (ts, 1), 0)) - (
            k0 + lax.broadcasted_iota(jnp.int32, (1, 2 * ts), 1))
        mask = jnp.where(dist >= 0, dist, SWA_WINDOW) < SWA_WINDOW
        o = []
        for c, h in enumerate((0, 2, 1, 3)):
            q = _half_heads(q_ref[s * ts:(s + 1) * ts, :])[c]
            sc = jnp.where(mask, _dot_nt(q, k), NEG)
            m = jnp.maximum(jnp.max(sc, axis=-1, keepdims=True), sk[:, h:h + 1])
            p = jnp.exp2(sc - m)
            l = jnp.sum(p, axis=-1, keepdims=True) + jnp.exp2(sk[:, h:h + 1] - m)
            o.append(_dot(p.astype(BF16), v) / l)
        o_ref[s * ts:(s + 1) * ts, :] = jnp.concatenate([_pair(o[0], o[1]), _pair(o[2], o[3])], axis=1).astype(BF16)


def _swa(sq, sk, sv, sinks, batch, seq):
    tq, ts = TILE_SWA
    assert ts == SWA_WINDOW and seq >= 2 * ts
    nb = seq // tq
    row = lambda w: pl.BlockSpec((tq, w), lambda b, i: (b * nb + i, 0))
    per_b = lambda w: pl.BlockSpec((seq, w), lambda b, i: (b, 0))
    return pl.pallas_call(
        functools.partial(_swa_kernel, tq=tq, ts=ts),
        grid=(batch, nb),
        in_specs=[row(256), per_b(128), per_b(128), pl.BlockSpec(sinks.shape, lambda b, i: (0, 0))],
        out_specs=row(256),
        out_shape=jax.ShapeDtypeStruct((batch * seq, 256), BF16),
        compiler_params=_params("parallel", "arbitrary"),
        name="swa_attn",
    )(sq, sk, sv, sinks)


def _outproj_kernel(x_ref, mod_ref, ng_ref, oa_ref, ob_ref, oc_ref, od_ref, w_ref, x1_ref, ht_ref):
    acc = _dot(oa_ref[...], w_ref[0:256, :])
    acc = acc + _dot(ob_ref[...], w_ref[256:512, :])
    acc = acc + _dot(oc_ref[...], w_ref[512:768, :])
    acc = acc + _dot(od_ref[...], w_ref[768:1024, :])
    x1 = x_ref[...] + mod_ref[0, 2:3, :] * acc
    x1_ref[...] = x1
    h = _rms(x1, ng_ref[...]) * (1.0 + mod_ref[0, 4:5, :]) + mod_ref[0, 3:4, :]
    ht_ref[...] = h.T.astype(BF16)


def _outproj(x2d, mod_l, norm_g, oa, ob, oc, od, w_out, seq):
    t, d = x2d.shape
    tm = TILE_OUTPROJ_ROWS
    tpb = seq // tm
    row = lambda w: pl.BlockSpec((tm, w), lambda i: (i, 0))
    full = lambda a: pl.BlockSpec(a.shape, lambda i: (0,) * a.ndim)
    return pl.pallas_call(
        _outproj_kernel,
        grid=(t // tm,),
        in_specs=[row(d), pl.BlockSpec((1, 6, d), lambda i: (i // tpb, 0, 0)), full(norm_g),
                  row(256), row(256), row(256), row(256), full(w_out)],
        out_specs=[row(d), pl.BlockSpec((d, tm), lambda i: (0, i))],
        out_shape=[jax.ShapeDtypeStruct((t, d), F32), jax.ShapeDtypeStruct((d, t), BF16)],
        compiler_params=_params("parallel"),
        name="outproj",
    )(x2d, mod_l, norm_g, oa, ob, oc, od, w_out)


_CAND_PIECES = [(0, 0, 8), (0, 8, 8), (1, 0, 8), (2, 0, 5), (3, 0, 4), (4, 0, 3), (5, 0, 2), (6, 0, 2), (7, 0, 2),
                (None, 0, 8)]


_CODE_UNIT = 2.0 ** 114
_TAKEN_BELOW = -(2.0 ** 119)
_INVALID = -(2.0 ** 100)


def _rank_code(r):
    return -(64.0 + r) * _CODE_UNIT


def _top16(s):
    tb = s.shape[1]
    row16 = lax.broadcasted_iota(jnp.int32, (PEER_TOPK, tb), 0)
    vals = jnp.zeros((PEER_TOPK, tb), F32)
    work = s
    for r in range(PEER_TOPK):
        m = jnp.max(work, axis=0, keepdims=True)
        work = jnp.where(work == m, _rank_code(r), work)
        vals = jnp.where(row16 == r, m, vals)
    return vals, work


def _router_head(h, ht, wq_ref, k1_ref, k2_ref, ea_ref, n1_ref, r2_ref, eb_ref):
    tb = ht.shape[1]
    nk = PEER_N_KEYS
    row8 = lax.broadcasted_iota(jnp.int32, (8, tb), 0)
    row16 = lax.broadcasted_iota(jnp.int32, (PEER_TOPK, tb), 0)
    o = pl.multiple_of(h * 2 * nk, 2 * nk)
    q1 = _dot(wq_ref[pl.ds(o, nk), :], ht).astype(BF16)
    q2 = _dot(wq_ref[pl.ds(o + nk, nk), :], ht).astype(BF16)
    s1 = _dot(k1_ref[...], q1)
    s2 = _dot(k2_ref[...], q2)
    v1, code1 = _top16(s1)
    v2, code2 = _top16(s2)
    top = v1[0:1] + v2[0:1]

    def cells(r1, c0):
        return v1[8:16] + v2[0:1] if r1 is None else v1[r1:r1 + 1] + v2[c0:c0 + 8]

    pieces = []
    for r1, c0, valid in _CAND_PIECES:
        p = cells(r1, c0)
        pieces.append(p if valid == 8 else jnp.where(row8 < valid, p, _INVALID))
    for _ in range(PEER_TOPK):
        m = pieces[0]
        for p in pieces[1:]:
            m = jnp.maximum(m, p)
        m = jnp.max(m, axis=0, keepdims=True)
        pieces = [jnp.where(p == m, _rank_code(0), p) for p in pieces]
    counts = jnp.zeros((PEER_TOPK, tb), F32)
    z = jnp.zeros((1, tb), F32)
    for p, (r1, c0, valid) in zip(pieces, _CAND_PIECES):
        taken = p < _TAKEN_BELOW
        if r1 is None:
            counts = counts + jnp.concatenate([jnp.zeros((8, tb), F32), jnp.where(taken, 1.0, 0.0)], axis=0)
        else:
            n = jnp.sum(jnp.where(taken, 1.0, 0.0), axis=0, keepdims=True)
            counts = counts + jnp.where(row16 == r1, n, 0.0)
        z = z + jnp.sum(jnp.where(taken, jnp.exp(cells(r1, c0) - top), 0.0), axis=0, keepdims=True)
    n1 = jnp.zeros((nk, tb), F32)
    for r in range(PEER_TOPK):
        n1 = jnp.where(code1 == _rank_code(r), counts[r:r + 1], n1)
    ea_ref[h] = jnp.exp(s1 - v1[0:1])
    n1_ref[h] = n1
    rank2 = jnp.where(code2 < _TAKEN_BELOW, code2 * (-1.0 / _CODE_UNIT) - 64.0, float(nk))
    r2_ref[h] = rank2.astype(BF16)
    eb_ref[h] = (jnp.exp(s2 - v2[0:1]) / z).astype(BF16)


def _router_kernel(ht_ref, wq_ref, k1_ref, k2_ref, ea_ref, n1_ref, r2_ref, eb_ref):
    ht = ht_ref[...]

    group = 8

    def heads(p, carry):
        for hh in range(group):
            _router_head(group * p + hh, ht, wq_ref, k1_ref, k2_ref, ea_ref, n1_ref, r2_ref, eb_ref)
        return carry

    lax.fori_loop(0, PEER_HEADS // group, heads, 0)


def _router(ht, wq_t, k1, k2):
    d, t = ht.shape
    tb = TILE_ROUTER_TOKENS
    full = lambda a: pl.BlockSpec(a.shape, lambda i: (0,) * a.ndim)
    out = pl.BlockSpec((PEER_HEADS, PEER_N_KEYS, tb), lambda i: (0, 0, i))
    return pl.pallas_call(
        _router_kernel,
        grid=(t // tb,),
        in_specs=[pl.BlockSpec((d, tb), lambda i: (0, i)), full(wq_t), full(k1), full(k2)],
        out_specs=[out, out, out, out],
        out_shape=[jax.ShapeDtypeStruct((PEER_HEADS, PEER_N_KEYS, t), dt) for dt in (F32, F32, BF16, BF16)],
        compiler_params=_params("parallel"),
        name="peer_router",
    )(ht, wq_t, k1, k2)


def _gelu_tanh(x):
    k = 2.0 * math.sqrt(2.0 / math.pi) * math.log2(math.e)
    return x / (1.0 + jnp.exp2(x * (-k - (k * 0.044715) * (x * x))))


def _peer_kernel(ht_ref, u_ref, vt_ref, ea_ref, n1_ref, r2_ref, eb_ref, x_ref, mod_ref, fg_ref, o_ref, acc_ref, wa_ref,
                 wb_ref, *, final, chunk, n_e):
    e = pl.program_id(1)
    nk = PEER_N_KEYS
    n_i1 = u_ref.shape[0] // nk
    sub = PACKED_ROWS
    assert n_i1 == 4

    def step(write_ref, read_ref, base):
        def tokens(c, carry):
            lanes = pl.ds(pl.multiple_of(c * chunk, chunk), chunk)
            if write_ref is not None:
                act = _gelu_tanh(_dot(u_ref[...], ht_ref[:, lanes])).astype(BF16)
            if read_ref is not None:
                acc_ref[:, lanes] += _dot(vt_ref[...], read_ref[:, lanes])
            if write_ref is None:
                return carry
            for j in range(n_i1):
                m = None
                for h in range(PEER_HEADS):
                    row = (h, slice(base + j, base + j + 1), lanes)
                    n_row = jnp.broadcast_to(n1_ref[row], (sub, chunk)).astype(BF16)
                    ea_row = jnp.broadcast_to(ea_ref[row], (sub, chunk)).astype(BF16)
                    r2 = r2_ref[h, :, lanes].reshape(nk // sub, sub, chunk)
                    eb = eb_ref[h, :, lanes].reshape(nk // sub, sub, chunk)
                    term = jnp.where(r2 < n_row[None], eb, jnp.zeros((), BF16)) * ea_row[None]
                    m = term if m is None else m + term
                write_ref[j * nk:(j + 1) * nk, lanes] = m.reshape(nk, chunk) * act[j * nk:(j + 1) * nk]
            return carry

        lax.fori_loop(0, ht_ref.shape[1] // chunk, tokens, 0)

    @pl.when(e == 0)
    def _():
        acc_ref[...] = jnp.zeros_like(acc_ref)
        step(wa_ref, None, 0)

    @pl.when((e > 0) & (e < n_e) & (lax.rem(e, 2) == 0))
    def _():
        step(wa_ref, wb_ref, 0)

    @pl.when((e < n_e) & (lax.rem(e, 2) == 1))
    def _():
        step(wb_ref, wa_ref, n_i1)

    @pl.when(e == n_e)
    def _():
        step(None, wa_ref if n_e % 2 else wb_ref, 0)
        y = x_ref[...] + mod_ref[0, 5:6, :] * acc_ref[...].T
        if final:
            y = _rms(y, fg_ref[...])
        o_ref[...] = y


def _peer(ht, u_bf, vt_bf, ea, n1, r2, eb, x1, mod_l, final_g, seq, final):
    d, t = ht.shape
    n_exp = u_bf.shape[0]
    tb = TILE_PEER_TOKENS if seq % TILE_PEER_TOKENS == 0 else TILE_PEER_TOKENS // 2
    eb_blk = TILE_PEER_EXPERTS
    n_e = n_exp // eb_blk
    n_i1 = eb_blk // PEER_N_KEYS
    tpb = seq // tb
    cur = lambda e: jnp.minimum(e, n_e - 1)
    i1_spec = pl.BlockSpec((PEER_HEADS, 8, tb), lambda i, e: (0, cur(e) // (8 // n_i1), i))
    tok3 = pl.BlockSpec((PEER_HEADS, PEER_N_KEYS, tb), lambda i, e: (0, 0, i))
    return pl.pallas_call(
        functools.partial(_peer_kernel, final=final, chunk=min(TILE_PEER_CHUNK, tb), n_e=n_e),
        grid=(t // tb, n_e + 1),
        in_specs=[pl.BlockSpec((d, tb), lambda i, e: (0, i)),
                  pl.BlockSpec((eb_blk, d), lambda i, e: (cur(e), 0)),
                  pl.BlockSpec((d, eb_blk), lambda i, e: (0, jnp.maximum(e - 1, 0))),
                  i1_spec, i1_spec, tok3, tok3,
                  pl.BlockSpec((tb, d), lambda i, e: (i, 0)),
                  pl.BlockSpec((1, 6, d), lambda i, e: (i // tpb, 0, 0)),
                  pl.BlockSpec(final_g.shape, lambda i, e: (0, 0))],
        out_specs=pl.BlockSpec((tb, d), lambda i, e: (i, 0)),
        out_shape=jax.ShapeDtypeStruct((t, d), F32),
        scratch_shapes=[pltpu.VMEM((d, tb), F32), pltpu.VMEM((eb_blk, tb), BF16), pltpu.VMEM((eb_blk, tb), BF16)],
        compiler_params=_params("parallel", "arbitrary"),
        name="peer_experts",
    )(ht, u_bf, vt_bf, ea, n1, r2, eb, x1, mod_l, final_g)


def kernel(x, c, ada_w, ada_b, norm_mix_g, norm_ffn_g, w_in, nsa_cmp_pos_k, nsa_cmp_pos_v, nsa_cmp_wk, nsa_cmp_wv, diff_lam_q1, diff_lam_k1, diff_lam_q2, diff_lam_k2, diff_sub_g, mla_q_norm_g, mla_w_uq, mla_kv_norm_g, mla_w_ukv, swa_sinks, w_out, peer_w_q, peer_sub_k1, peer_sub_k2, peer_u, peer_v, final_g):
    batch, seq, d = x.shape
    depth = w_in.shape[0]
    assert seq % 512 == 0
    x2d = x.reshape(batch * seq, d)
    mod = _adaln(c, ada_w, ada_b).reshape(depth, batch, 6, d)
    table = _rope_table(seq)
    mixw = 4 * HEAD_DIM
    swa_rows = 3 * mixw + np.concatenate([h * HEAD_DIM + np.arange(HEAD_DIM) for h in (0, 2, 1, 3)])
    out_rows = jnp.asarray(np.concatenate([np.arange(3 * mixw), swa_rows]), jnp.int32)
    pad128 = lambda v: jnp.pad(v, (0, LANES - v.shape[0])).reshape(1, LANES)
    fg = final_g.reshape(1, d)
    w_big = _take_cols(w_in, _IN_IDX, _IN_SGN)
    wuq = _take_cols(mla_w_uq, _UQ_IDX, _UQ_SGN)
    wukv = _take_cols(mla_w_ukv, _UKV_IDX, _UKV_SGN)
    for l in range(depth):
        (nq, nk, nv, ng, dq, dk, dv, mq, mk, mv, sq, sk, sv) = _inproj(
            x2d, mod[l], norm_mix_g[l].reshape(1, d), w_big[l], table, wuq[l], wukv[l],
            mla_q_norm_g[l].reshape(1, -1), mla_kv_norm_g[l].reshape(1, -1), seq)
        ocmp, sel = _nsa_cmp(nq, nk, nv, nsa_cmp_wk[l], nsa_cmp_wv[l], nsa_cmp_pos_k[l], nsa_cmp_pos_v[l], batch, seq)
        o_a = _nsa(nq, nk, nv, sel, ng, ocmp, batch, seq)
        lamv = jnp.concatenate([pad128(diff_lam_q1[l]), pad128(diff_lam_k1[l]),
                                pad128(diff_lam_q2[l]), pad128(diff_lam_k2[l])], axis=0)
        sub_g2 = jnp.concatenate([diff_sub_g[l], diff_sub_g[l]]).reshape(1, LANES)
        o_b = _diff(dq, dk, dv, lamv, sub_g2, l, batch, seq)
        o_c = _mla(mq, mk, mv, batch, seq)
        o_d = _swa(sq, sk, sv, pad128(swa_sinks[l]), batch, seq)
        w_o = jnp.take(w_out[l], out_rows, axis=0).astype(BF16)
        x1, ht = _outproj(x2d, mod[l], norm_ffn_g[l].reshape(1, d), o_a, o_b, o_c, o_d, w_o, seq)
        ea, n1, r2, eb = _router(ht, peer_w_q[l].T.astype(BF16), peer_sub_k1[l].astype(BF16),
                                 peer_sub_k2[l].astype(BF16))
        x2d = _peer(ht, peer_u[l].astype(BF16), peer_v[l].T.astype(BF16), ea, n1, r2, eb, x1, mod[l], fg,
                    seq, final=(l == depth - 1))
    return x2d.reshape(batch, seq, d)
```

```python
import functools
import math

import numpy as np
import jax
import jax.numpy as jnp
from jax import lax
from jax.experimental import pallas as pl
from jax.experimental.pallas import tpu as pltpu

F32 = jnp.float32
BF16 = jnp.bfloat16

HEAD_DIM = 64
ROPE_THETA = 10000.0
EPS = 1e-6
NEG = -1e30
FORCE = 1e4

NSA_HEADS = 4
NSA_CMP_LEN = 32
NSA_CMP_STRIDE = 16
NSA_SEL_LEN = 64
NSA_TOP_N = 16
NSA_WINDOW = 512

DIFF_HEADS = 4
DIFF_QK_DIM = 32
DIFF_V_DIM = 64

MLA_HEADS = 4
MLA_Q_RANK = 256
MLA_KV_RANK = 128
MLA_NOPE_DIM = 64
MLA_ROPE_DIM = 32
MLA_V_DIM = 64

SWA_HEADS = 4
SWA_KV_HEADS = 2
SWA_WINDOW = 128

PEER_HEADS = 8
PEER_N_KEYS = 128
PEER_TOPK = 16
PEER_QUERY_DIM = 256

LOG2E = math.log2(math.e)
LANES = 128
PACKED_ROWS = 16
VMEM_LIMIT = 56 * 1024 * 1024

TILE_PROJ_ROWS = 256
TILE_OUTPROJ_ROWS = 512
TILE_ADALN_COLS = 1536
TILE_CMP_ROWS = 2048
TILE_NSA = (512, 512)
TILE_DIFF = (256, 512)
TILE_MLA = (512, 512)
TILE_SWA = (1024, 128)
TILE_ROUTER_TOKENS = 256
TILE_PEER_TOKENS = 1024
TILE_PEER_EXPERTS = 512
TILE_PEER_CHUNK = 1024


def _dot(a, b):
    return jnp.dot(a, b, preferred_element_type=F32)


def _dot_nt(a, b):
    return lax.dot_general(a, b, (((1,), (1,)), ((), ())), preferred_element_type=F32)


def _params(*sem):
    return pltpu.CompilerParams(dimension_semantics=sem, vmem_limit_bytes=VMEM_LIMIT)


def _rms(x, g):
    return x * lax.rsqrt(jnp.mean(x * x, axis=-1, keepdims=True) + EPS) * g


def _rot_idx(base, dim):
    half = dim // 2
    idx = np.concatenate([base + half + np.arange(half), base + np.arange(half)])
    sgn = np.concatenate([-np.ones(half), np.ones(half)])
    return idx, sgn


def _in_plan():
    d = HEAD_DIM
    nsa0 = 0
    nsa_cols = NSA_HEADS * d + 6 * d + 3 * NSA_HEADS
    diff0 = nsa0 + nsa_cols
    diff_cols = 2 * DIFF_HEADS * 2 * DIFF_QK_DIM + DIFF_HEADS * DIFF_V_DIM
    mla0 = diff0 + diff_cols
    mla_cols = MLA_Q_RANK + MLA_KV_RANK + MLA_ROPE_DIM
    swa0 = mla0 + mla_cols
    idx, sgn, off = [], [], {}

    groups = {}

    def add(name, i, s=None):
        i = np.asarray(i, np.int64)
        s = np.ones(len(i)) if s is None else np.asarray(s, np.float64)
        pad = (-len(i)) % LANES
        groups[name] = (np.concatenate([i, np.zeros(pad, np.int64)]), np.concatenate([s, np.zeros(pad)]))

    def heads_rot(base, nheads, dim):
        ii, ss = zip(*[_rot_idx(base + h * dim, dim) for h in range(nheads)])
        return np.concatenate(ii), np.concatenate(ss)

    nq = nsa0 + np.arange(NSA_HEADS * d)
    add("nq", nq)
    add("nqr", *heads_rot(nsa0, NSA_HEADS, d))
    kb = nsa0 + NSA_HEADS * d
    kc, vc, ksl, vsl, kw, vw = [kb + j * d for j in range(6)]
    dup = lambda b: np.concatenate([b + np.arange(d), b + np.arange(d)])
    add("nk", np.concatenate([dup(kc), dup(ksl), dup(kw)]))
    kr = [_rot_idx(b, d) for b in (kc, kc, ksl, ksl, kw, kw)]
    add("nkr", np.concatenate([a for a, _ in kr]), np.concatenate([b for _, b in kr]))
    add("nv", np.concatenate([dup(vc), dup(vsl), dup(vw)]))
    add("ng", kb + 6 * d + np.arange(3 * NSA_HEADS))
    nqk = DIFF_HEADS * 2 * DIFF_QK_DIM
    add("dq", diff0 + np.arange(nqk))
    add("dqr", *heads_rot(diff0, 2 * DIFF_HEADS, DIFF_QK_DIM))
    add("dk", diff0 + nqk + np.arange(nqk))
    add("dkr", *heads_rot(diff0 + nqk, 2 * DIFF_HEADS, DIFF_QK_DIM))
    add("dv", diff0 + 2 * nqk + np.arange(DIFF_HEADS * DIFF_V_DIM))
    add("mcq", mla0 + np.arange(MLA_Q_RANK))
    add("mckv", mla0 + MLA_Q_RANK + np.arange(MLA_KV_RANK))
    kr0 = mla0 + MLA_Q_RANK + MLA_KV_RANK
    z64 = np.zeros(MLA_NOPE_DIM, np.int64)
    add("mkr", np.concatenate([z64, kr0 + np.arange(MLA_ROPE_DIM)]),
        np.concatenate([np.zeros(MLA_NOPE_DIM), np.ones(MLA_ROPE_DIM)]))
    ri, rs = _rot_idx(kr0, MLA_ROPE_DIM)
    add("mkrr", np.concatenate([z64, ri]), np.concatenate([np.zeros(MLA_NOPE_DIM), rs]))
    order = [0, 2, 1, 3]
    add("sq", np.concatenate([swa0 + h * d + np.arange(d) for h in order]))
    sr = [_rot_idx(swa0 + h * d, d) for h in order]
    add("sqr", np.concatenate([a for a, _ in sr]), np.concatenate([b for _, b in sr]))
    sk0 = swa0 + SWA_HEADS * d
    add("sk", sk0 + np.arange(SWA_KV_HEADS * d))
    add("skr", *heads_rot(sk0, SWA_KV_HEADS, d))
    add("sv", sk0 + SWA_KV_HEADS * d + np.arange(SWA_KV_HEADS * d))
    order = ["nq", "nqr", "nk", "ng", "nkr", "mckv", "nv", "sk", "dq", "dqr", "dk", "dkr", "dv", "mcq",
             "mkr", "mkrr", "skr", "sv", "sq", "sqr"]
    assert sorted(order) == sorted(groups)
    blk, run, pos = {}, [], 0
    for name in order:
        off[name] = pos
        idx.append(groups[name][0])
        sgn.append(groups[name][1])
        run.append(name)
        pos += len(groups[name][0])
        if pos % (2 * LANES) == 0:
            start = off[run[0]]
            blk.update({n: (start, pos - start) for n in run})
            run = []
    assert not run
    return np.concatenate(idx), np.concatenate(sgn), off, blk


_IN_IDX, _IN_SGN, _OFF, _BLK = _in_plan()


def _mla_plans():
    qd = MLA_NOPE_DIM + MLA_ROPE_DIM
    qi, qs, ri, rs = [], [], [], []
    for h in range(MLA_HEADS):
        b = h * qd
        qi += [b + np.arange(qd), np.zeros(LANES - qd, np.int64)]
        qs += [np.ones(qd), np.zeros(LANES - qd)]
        a, s = _rot_idx(b + MLA_NOPE_DIM, MLA_ROPE_DIM)
        ri += [np.zeros(MLA_NOPE_DIM, np.int64), a, np.zeros(LANES - qd, np.int64)]
        rs += [np.zeros(MLA_NOPE_DIM), s, np.zeros(LANES - qd)]
    kd = MLA_NOPE_DIM + MLA_V_DIM
    ki, ks, vi = [], [], []
    for h in range(MLA_HEADS):
        ki += [h * kd + np.arange(MLA_NOPE_DIM), np.zeros(LANES - MLA_NOPE_DIM, np.int64)]
        ks += [np.ones(MLA_NOPE_DIM), np.zeros(LANES - MLA_NOPE_DIM)]
        vi += [h * kd + MLA_NOPE_DIM + np.arange(MLA_V_DIM)]
    uq_idx = np.concatenate(qi + ri)
    uq_sgn = np.concatenate(qs + rs)
    ukv_idx = np.concatenate(ki + vi)
    ukv_sgn = np.concatenate(ks + [np.ones(MLA_HEADS * MLA_V_DIM)])
    return uq_idx, uq_sgn, ukv_idx, ukv_sgn


_UQ_IDX, _UQ_SGN, _UKV_IDX, _UKV_SGN = _mla_plans()


def _take_cols(w, idx, sgn):
    pieces, start = [], 0
    for i in range(1, len(idx) + 1):
        same_run = (i < len(idx) and sgn[i] == sgn[start]
                    and (sgn[i] == 0.0 or idx[i] == idx[i - 1] + 1))
        if not same_run:
            n, s = i - start, float(sgn[start])
            run = w[..., int(idx[start]):int(idx[start]) + n]
            pieces.append(jnp.zeros(w.shape[:-1] + (n,), w.dtype) if s == 0.0 else (run if s == 1.0 else -run))
            start = i
    return jnp.concatenate(pieces, axis=-1).astype(BF16)


def _rope_table(seq):
    def cs(dim):
        inv = 1.0 / (ROPE_THETA ** (jnp.arange(0, dim, 2, dtype=F32) / dim))
        ang = jnp.arange(seq, dtype=F32)[:, None] * inv[None, :]
        c, s = jnp.cos(ang), jnp.sin(ang)
        return jnp.concatenate([c, c], 1), jnp.concatenate([s, s], 1)
    ch, sh = cs(HEAD_DIM)
    cd, sd = cs(DIFF_QK_DIM)
    cm, sm = cs(MLA_ROPE_DIM)
    one = jnp.ones((seq, MLA_NOPE_DIM), F32)
    z64 = jnp.zeros((seq, MLA_NOPE_DIM), F32)
    z32 = jnp.zeros((seq, LANES - MLA_NOPE_DIM - MLA_ROPE_DIM), F32)
    parts = [jnp.tile(ch, (1, 2)), jnp.tile(sh, (1, 2)), jnp.tile(cd, (1, 4)), jnp.tile(sd, (1, 4)),
             jnp.concatenate([one, cm, z32], 1), jnp.concatenate([z64, sm, z32], 1),
             jnp.concatenate([z64, cm, z32], 1), jnp.concatenate([z64, sm, z32], 1)]
    return jnp.concatenate(parts, 1)


def _adaln_kernel(c_ref, w_ref, b_ref, o_ref):
    c = c_ref[...]
    sc = (c * jax.nn.sigmoid(c)).astype(BF16)
    o_ref[0] = _dot(sc, w_ref[0].astype(BF16)) + b_ref[0]


def _adaln(c, ada_w, ada_b):
    nl, d, n6 = ada_w.shape
    b = c.shape[0]
    tn = TILE_ADALN_COLS
    return pl.pallas_call(
        _adaln_kernel,
        grid=(nl, n6 // tn),
        in_specs=[pl.BlockSpec((b, d), lambda l, j: (0, 0)),
                  pl.BlockSpec((1, d, tn), lambda l, j: (l, 0, j)),
                  pl.BlockSpec((1, 1, tn), lambda l, j: (l, 0, j))],
        out_specs=pl.BlockSpec((1, b, tn), lambda l, j: (l, 0, j)),
        out_shape=jax.ShapeDtypeStruct((nl, b, n6), F32),
        compiler_params=_params("parallel", "parallel"),
        name="adaln",
    )(c, ada_w, ada_b.reshape(nl, 1, n6))


def _inproj_kernel(x_ref, mod_ref, ng_ref, w_ref, tab_ref, wuq_ref, wukv_ref, gq_ref, gkv_ref,
                   nq_ref, nk_ref, nv_ref, ngo_ref, dq_ref, dk_ref, dv_ref,
                   mq_ref, mk_ref, mv_ref, sq_ref, sk_ref, sv_ref):
    x = x_ref[...]
    h = _rms(x, ng_ref[...]) * (1.0 + mod_ref[0, 1:2, :]) + mod_ref[0, 0:1, :]
    hb = h.astype(BF16)

    runs = {}

    def mm(name, width):
        start, size = _BLK[name]
        if start not in runs:
            runs[start] = _dot(hb, w_ref[:, start:start + size])
        o = _OFF[name] - start
        return runs[start][:, o:o + width]

    def tab(j, reps):
        t = tab_ref[:, j * LANES:(j + 1) * LANES]
        return t if reps == 1 else jnp.concatenate([t] * reps, axis=1)

    def rope(name, rname, width, cj, scale=1.0):
        r = mm(name, width) * tab(cj, width // LANES) + mm(rname, width) * tab(cj + 1, width // LANES)
        return r if scale == 1.0 else r * scale

    d = HEAD_DIM
    nq_ref[...] = rope("nq", "nqr", 256, 0, LOG2E * d ** -0.5).astype(BF16)
    nk_ref[...] = rope("nk", "nkr", 384, 0).astype(BF16)
    nv_ref[...] = mm("nv", 384).astype(BF16)
    ngo_ref[...] = jax.nn.sigmoid(mm("ng", LANES))
    dq_ref[...] = rope("dq", "dqr", 256, 2, LOG2E * DIFF_QK_DIM ** -0.5).astype(BF16)
    dk_ref[...] = rope("dk", "dkr", 256, 2).astype(BF16)
    dv_ref[...] = mm("dv", 256).astype(BF16)
    cq = _rms(mm("mcq", MLA_Q_RANK), gq_ref[...]).astype(BF16)
    nh = MLA_HEADS * LANES
    qa = _dot(cq, wuq_ref[:, 0:nh])
    qb = _dot(cq, wuq_ref[:, nh:2 * nh])
    mq = (qa * tab(4, MLA_HEADS) + qb * tab(5, MLA_HEADS)) * (LOG2E * (MLA_NOPE_DIM + MLA_ROPE_DIM) ** -0.5)
    mq_ref[...] = mq.astype(BF16)
    ckv = _rms(mm("mckv", MLA_KV_RANK), gkv_ref[...]).astype(BF16)
    kk = _dot(ckv, wukv_ref[:, 0:nh])
    kr = mm("mkr", LANES) * tab(6, 1) + mm("mkrr", LANES) * tab(7, 1)
    mk_ref[...] = (kk + jnp.concatenate([kr] * MLA_HEADS, axis=1)).astype(BF16)
    mv_ref[...] = _dot(ckv, wukv_ref[:, nh:nh + MLA_HEADS * MLA_V_DIM]).astype(BF16)
    sq_ref[...] = rope("sq", "sqr", 256, 0, LOG2E * d ** -0.5).astype(BF16)
    sk_ref[...] = rope("sk", "skr", 128, 0).astype(BF16)
    sv_ref[...] = mm("sv", 128).astype(BF16)


def _inproj(x2d, mod_l, norm_g, w_big, table, wuq, wukv, gq, gkv, seq):
    t, d = x2d.shape
    tm = TILE_PROJ_ROWS
    tpb = seq // tm
    widths = [256, 384, 384, 128, 256, 256, 256, 512, 512, 256, 256, 128, 128]
    dts = [BF16, BF16, BF16, F32, BF16, BF16, BF16, BF16, BF16, BF16, BF16, BF16, BF16]
    full = lambda a: pl.BlockSpec(a.shape, lambda i: (0,) * a.ndim)
    return pl.pallas_call(
        _inproj_kernel,
        grid=(t // tm,),
        in_specs=[pl.BlockSpec((tm, d), lambda i: (i, 0)),
                  pl.BlockSpec((1, 6, d), lambda i: (i // tpb, 0, 0)),
                  full(norm_g), full(w_big),
                  pl.BlockSpec((tm, table.shape[1]), lambda i: (i % tpb, 0)),
                  full(wuq), full(wukv), full(gq), full(gkv)],
        out_specs=[pl.BlockSpec((tm, w), lambda i: (i, 0)) for w in widths],
        out_shape=[jax.ShapeDtypeStruct((t, w), dt) for w, dt in zip(widths, dts)],
        compiler_params=_params("parallel"),
        name="inproj",
    )(x2d, mod_l, norm_g, w_big, table, wuq, wukv, gq, gkv)


def _lane_mask(lo, hi):
    lane = lax.broadcasted_iota(jnp.int32, (1, LANES), 1)
    return (lane >= lo) & (lane < hi)


def _masked(q, lo, hi):
    return jnp.where(_lane_mask(lo, hi), q, jnp.zeros_like(q))


def _chain(q, k, v, mask, state, acc_ref, c):
    m, l = state
    s = _dot_nt(q, k)
    if mask is not None:
        s = jnp.where(mask, s, NEG)
    m2 = jnp.maximum(m, jnp.max(s, axis=-1, keepdims=True))
    a = jnp.exp2(m - m2)
    p = jnp.exp2(s - m2)
    acc_ref[c] = a * acc_ref[c] + _dot(p.astype(BF16), v)
    return m2, a * l + jnp.sum(p, axis=-1, keepdims=True)


def _init_state(n, rows):
    return tuple((jnp.full((rows, 1), NEG, F32), jnp.zeros((rows, 1), F32)) for _ in range(n))


def _ktile(ref, j, tk, c0, c1):
    return ref[pl.ds(pl.multiple_of(j * tk, tk), tk), c0:c1]


def _qpos(i, tq, reps=1):
    p = i * tq + lax.broadcasted_iota(jnp.int32, (tq, 1), 0)
    return p if reps == 1 else jnp.concatenate([p] * reps, axis=0)


def _kpos(j, tk):
    return j * tk + lax.broadcasted_iota(jnp.int32, (1, tk), 1)


def _half_heads(q):
    return [_masked(q[:, c * LANES:(c + 1) * LANES], 64 * hh, 64 * hh + 64) for c in range(2) for hh in range(2)]


def _pair(lo_val, hi_val):
    return jnp.where(_lane_mask(0, 64), lo_val, hi_val)


def _nsa_cmp_kernel(q_ref, kc_ref, vc_ref, wk_ref, wv_ref, pk_ref, pv_ref, ov_ref, oc_ref, sel_ref, *, top_n, n_sel):
    half = wk_ref.shape[1]

    def compress(x_ref, w_ref, p_ref):
        x = x_ref[...]
        a = _dot(x, w_ref[0])
        b = _dot(x, w_ref[1])
        p = jnp.broadcast_to(p_ref[...], (8, 2 * half)).astype(BF16)
        const = (_dot(p[:, 0:half], w_ref[0]) + _dot(p[:, half:2 * half], w_ref[1]))[0:1]
        return a + jnp.concatenate([b[1:], b[:1]], axis=0) + const

    kcmp = compress(kc_ref, wk_ref, pk_ref).astype(BF16)
    vcmp = compress(vc_ref, wv_ref, pv_ref).astype(BF16)
    ncp = kcmp.shape[0]
    ov = ov_ref[...]
    rb = math.gcd(TILE_CMP_ROWS, q_ref.shape[0])
    cend = NSA_CMP_STRIDE * lax.broadcasted_iota(jnp.int32, (1, ncp), 1) + (NSA_CMP_LEN - 1)
    lane = lax.broadcasted_iota(jnp.int32, (1, LANES), 1)

    def block(r, carry):
        r0 = pl.multiple_of(r * rb, rb)
        q = q_ref[pl.ds(r0, rb), :]
        tpos = r0 + lax.broadcasted_iota(jnp.int32, (rb, 1), 0)
        vis = cend <= tpos
        psum = jnp.zeros((rb, ncp), F32)
        outs = []
        for c in range(2):
            halves = []
            for hh in range(2):
                qm = _masked(q[:, c * LANES:(c + 1) * LANES], 64 * hh, 64 * hh + 64)
                s = jnp.where(vis, _dot_nt(qm, kcmp), NEG)
                e = jnp.exp2(s - jnp.max(s, axis=-1, keepdims=True))
                p = jnp.where(vis, e / jnp.sum(e, axis=-1, keepdims=True), 0.0)
                psum = psum + p
                halves.append(_dot(p.astype(BF16), vcmp))
            outs.append(_pair(halves[0], halves[1]))
        oc_ref[pl.ds(r0, rb), :] = jnp.concatenate(outs, axis=1)
        hi = psum.astype(BF16)
        lo = (psum - hi.astype(F32)).astype(BF16)
        imp = _dot(hi, ov) + _dot(lo, ov)
        nsp = -(-n_sel // 8) * 8
        imp_t = imp.T[0:nsp]
        blk = lax.broadcasted_iota(jnp.int32, (nsp, 1), 0)
        qblk = (r0 + lax.broadcasted_iota(jnp.int32, (1, rb), 1)) // NSA_SEL_LEN
        allowed = blk <= qblk
        forced = (blk == 0) | (blk == qblk) | (blk == qblk - 1)
        impf = jnp.where(allowed, jnp.where(forced, FORCE, imp_t), NEG)
        rank = jnp.zeros((nsp, rb), F32)
        for j in range(n_sel):
            row = impf[j:j + 1, :]
            rank = rank + jnp.where(blk > j, jnp.where(row >= impf, 1.0, 0.0), jnp.where(row > impf, 1.0, 0.0))
        sel_t = jnp.where(allowed & (rank < top_n), 1.0, 0.0)
        if nsp < LANES:
            sel_t = jnp.concatenate([sel_t, jnp.zeros((LANES - nsp, rb), F32)], axis=0)
        sel_ref[pl.ds(r0, rb), :] = sel_t.T.astype(BF16)
        return carry

    lax.fori_loop(0, q_ref.shape[0] // rb, block, 0)


def _nsa_cmp(nq, nk, nv, wk, wv, pos_k, pos_v, batch, seq):
    d = HEAD_DIM
    nc = seq // NSA_CMP_STRIDE
    ncp = -(-nc // LANES) * LANES
    n_sel = seq // NSA_SEL_LEN
    assert n_sel <= LANES
    top_n = min(NSA_TOP_N, n_sel)

    def seg(a):
        a = a[:, :d].reshape(batch, nc, NSA_CMP_STRIDE * d)
        return jnp.pad(a, ((0, 0), (0, ncp - nc), (0, 0))).reshape(batch * ncp, NSA_CMP_STRIDE * d)

    half = NSA_CMP_STRIDE * d
    dupw = lambda w: jnp.concatenate([w, w], axis=1).reshape(2, half, 2 * d).astype(BF16)
    cpos = NSA_CMP_STRIDE * np.arange(ncp)[:, None] + np.arange(NSA_CMP_LEN)[None, :]
    ovl = np.zeros((ncp, LANES), np.float32)
    for j in range(n_sel):
        ovl[:, j] = (cpos // NSA_SEL_LEN == j).mean(axis=1)
    ovl[nc - 1:, :] = 0.0
    full = lambda a: pl.BlockSpec(a.shape, lambda b: (0,) * a.ndim)
    wk2, wv2 = dupw(wk), dupw(wv)
    pk, pv = pos_k.reshape(1, -1), pos_v.reshape(1, -1)
    ov = jnp.asarray(ovl, BF16)
    return pl.pallas_call(
        functools.partial(_nsa_cmp_kernel, top_n=top_n, n_sel=n_sel),
        grid=(batch,),
        in_specs=[pl.BlockSpec((seq, 256), lambda b: (b, 0)),
                  pl.BlockSpec((ncp, half), lambda b: (b, 0)),
                  pl.BlockSpec((ncp, half), lambda b: (b, 0)),
                  full(wk2), full(wv2), full(pk), full(pv), full(ov)],
        out_specs=[pl.BlockSpec((seq, 256), lambda b: (b, 0)),
                   pl.BlockSpec((seq, LANES), lambda b: (b, 0))],
        out_shape=[jax.ShapeDtypeStruct((batch * seq, 256), F32),
                   jax.ShapeDtypeStruct((batch * seq, LANES), BF16)],
        compiler_params=_params("parallel"),
        name="nsa_cmp",
    )(nq, seg(nk), seg(nv), wk2, wv2, pk, pv, ov)


def _nsa_kernel(q_ref, k_ref, v_ref, sel_ref, g_ref, oc_ref, e_ref, o_ref, acc_ref, *, tq, tk):
    i = pl.program_id(1)
    qh = _half_heads(q_ref[...])
    qp = _qpos(i, tq)
    sel = sel_ref[...]
    nh = NSA_HEADS
    acc_ref[...] = jnp.zeros_like(acc_ref)

    def sel_step(j, st, diag):
        mv = _dot(sel, e_ref[:, pl.ds(pl.multiple_of(j * tk, tk), tk)])
        if diag:
            mv = jnp.where(_kpos(j, tk) <= qp, mv, 0.0)
        mask = mv > 0.5
        k = _ktile(k_ref, j, tk, 128, 256)
        v = _ktile(v_ref, j, tk, 128, 256)
        return tuple(_chain(qh[h], k, v, mask, st[h], acc_ref, h) for h in range(nh))

    nfull = (i * tq) // tk
    st = lax.fori_loop(0, nfull, lambda j, s: sel_step(j, s, False), _init_state(nh, tq))
    st_sel = sel_step(nfull, st, True)

    def win_step(j, st):
        dist = qp - _kpos(j, tk)
        mask = jnp.where(dist >= 0, dist, NSA_WINDOW) < NSA_WINDOW
        k = _ktile(k_ref, j, tk, 256, 384)
        v = _ktile(v_ref, j, tk, 256, 384)
        return tuple(_chain(qh[h], k, v, mask, st[h], acc_ref, nh + h) for h in range(nh))

    wlo = jnp.maximum(i * tq - NSA_WINDOW, 0) // tk
    st_win = lax.fori_loop(wlo, nfull + 1, win_step, _init_state(nh, tq))
    g = g_ref[...]
    oc = oc_ref[...]
    outs = []
    for c in range(2):
        occ = oc[:, c * LANES:(c + 1) * LANES]

        def comb(h):
            o_sel = acc_ref[h] / st_sel[h][1]
            o_win = acc_ref[nh + h] / st_win[h][1]
            return g[:, 3 * h:3 * h + 1] * occ + g[:, 3 * h + 1:3 * h + 2] * o_sel + g[:, 3 * h + 2:3 * h + 3] * o_win

        outs.append(_pair(comb(2 * c), comb(2 * c + 1)))
    o_ref[...] = jnp.concatenate(outs, axis=1).astype(BF16)


def _nsa(nq, nk, nv, sel, gates, ocmp, batch, seq):
    tq, tk = TILE_NSA
    nb = seq // tq
    expand = np.zeros((LANES, seq), np.float32)
    for j in range(seq // NSA_SEL_LEN):
        expand[j, j * NSA_SEL_LEN:(j + 1) * NSA_SEL_LEN] = 1.0
    e = jnp.asarray(expand, BF16)
    row = lambda w: pl.BlockSpec((tq, w), lambda b, i: (b * nb + i, 0))
    per_b = lambda w: pl.BlockSpec((seq, w), lambda b, i: (b, 0))
    return pl.pallas_call(
        functools.partial(_nsa_kernel, tq=tq, tk=tk),
        grid=(batch, nb),
        in_specs=[row(256), per_b(384), per_b(384), row(LANES), row(LANES), row(256),
                  pl.BlockSpec(e.shape, lambda b, i: (0, 0))],
        out_specs=row(256),
        out_shape=jax.ShapeDtypeStruct((batch * seq, 256), BF16),
        scratch_shapes=[pltpu.VMEM((2 * NSA_HEADS, tq, LANES), F32)],
        compiler_params=_params("parallel", "arbitrary"),
        name="nsa_attn",
    )(nq, nk, nv, sel, gates, ocmp, e)


def _diff_kernel(q_ref, k_ref, v_ref, lam_ref, sg_ref, o_ref, acc_ref, *, lam_init, tq, tk):
    i = pl.program_id(1)
    q = q_ref[...]
    qp4 = _qpos(i, tq, 4)
    lv = lam_ref[...]
    lam = (jnp.exp(jnp.sum(lv[0:1] * lv[1:2], axis=-1, keepdims=True))
           - jnp.exp(jnp.sum(lv[2:3] * lv[3:4], axis=-1, keepdims=True)) + lam_init)
    qs = [jnp.concatenate([_masked(q[:, c * LANES:(c + 1) * LANES], 32 * t, 32 * t + 32) for t in range(4)], axis=0)
          for c in range(2)]
    acc_ref[...] = jnp.zeros_like(acc_ref)

    def step(j, st, diag):
        mask = (_kpos(j, tk) <= qp4) if diag else None
        return tuple(_chain(qs[c], _ktile(k_ref, j, tk, c * LANES, (c + 1) * LANES),
                            _ktile(v_ref, j, tk, c * LANES, (c + 1) * LANES), mask, st[c], acc_ref, c)
                     for c in range(2))

    nfull = (i * tq) // tk
    st = lax.fori_loop(0, nfull, lambda j, s: step(j, s, False), _init_state(2, 4 * tq))
    st = step(nfull, st, True)
    outs = []
    for c in range(2):
        o = acc_ref[c] / st[c][1]
        r = [o[t * tq:(t + 1) * tq] for t in range(4)]
        dd = _pair(r[0] - lam * r[1], r[2] - lam * r[3])
        sq = dd * dd
        lo = _lane_mask(0, 64)
        ms = _pair(jnp.sum(jnp.where(lo, sq, 0.0), axis=-1, keepdims=True),
                   jnp.sum(jnp.where(lo, 0.0, sq), axis=-1, keepdims=True)) * (1.0 / DIFF_V_DIM)
        outs.append(dd * lax.rsqrt(ms + EPS) * sg_ref[...] * (1.0 - lam_init))
    o_ref[...] = jnp.concatenate(outs, axis=1).astype(BF16)


def _diff(dq, dk, dv, lamv, sub_g2, layer, batch, seq):
    tq, tk = TILE_DIFF
    nb = seq // tq
    lam_init = 0.8 - 0.6 * math.exp(-0.3 * layer)
    row = lambda w: pl.BlockSpec((tq, w), lambda b, i: (b * nb + i, 0))
    per_b = lambda w: pl.BlockSpec((seq, w), lambda b, i: (b, 0))
    full = lambda a: pl.BlockSpec(a.shape, lambda b, i: (0,) * a.ndim)
    return pl.pallas_call(
        functools.partial(_diff_kernel, lam_init=lam_init, tq=tq, tk=tk),
        grid=(batch, nb),
        in_specs=[row(256), per_b(256), per_b(256), full(lamv), full(sub_g2)],
        out_specs=row(256),
        out_shape=jax.ShapeDtypeStruct((batch * seq, 256), BF16),
        scratch_shapes=[pltpu.VMEM((2, 4 * tq, LANES), F32)],
        compiler_params=_params("parallel", "arbitrary"),
        name="diff_attn",
    )(dq, dk, dv, lamv, sub_g2)


def _mla_kernel(q_ref, k_ref, v_ref, o_ref, acc_ref, *, tq, tk):
    i = pl.program_id(1)
    qp = _qpos(i, tq)
    nh = MLA_HEADS
    acc_ref[...] = jnp.zeros_like(acc_ref)

    def step(j, st, diag):
        mask = (_kpos(j, tk) <= qp) if diag else None
        return tuple(_chain(q_ref[:, h * LANES:(h + 1) * LANES], _ktile(k_ref, j, tk, h * LANES, (h + 1) * LANES),
                            _ktile(v_ref, j, tk, (h // 2) * LANES, (h // 2 + 1) * LANES), mask, st[h], acc_ref, h)
                     for h in range(nh))

    nfull = (i * tq) // tk
    st = lax.fori_loop(0, nfull, lambda j, s: step(j, s, False), _init_state(nh, tq))
    st = step(nfull, st, True)
    o = [acc_ref[h] / st[h][1] for h in range(nh)]
    o_ref[...] = jnp.concatenate([_pair(o[0], o[1]), _pair(o[2], o[3])], axis=1).astype(BF16)


def _mla(mq, mk, mv, batch, seq):
    tq, tk = TILE_MLA
    nb = seq // tq
    row = lambda w: pl.BlockSpec((tq, w), lambda b, i: (b * nb + i, 0))
    per_b = lambda w: pl.BlockSpec((seq, w), lambda b, i: (b, 0))
    return pl.pallas_call(
        functools.partial(_mla_kernel, tq=tq, tk=tk),
        grid=(batch, nb),
        in_specs=[row(512), per_b(512), per_b(256)],
        out_specs=row(256),
        out_shape=jax.ShapeDtypeStruct((batch * seq, 256), BF16),
        scratch_shapes=[pltpu.VMEM((MLA_HEADS, tq, LANES), F32)],
        compiler_params=_params("parallel", "arbitrary"),
        name="mla_attn",
    )(mq, mk, mv)


def _swa_kernel(q_ref, k_ref, v_ref, sink_ref, o_ref, *, tq, ts):
    i = pl.program_id(1)
    sk = sink_ref[...] * LOG2E
    for s in range(tq // ts):
        g = i * (tq // ts) + s
        k0 = pl.multiple_of(jnp.maximum(g - 1, 0) * ts, ts)
        k = k_ref[pl.ds(k0, 2 * ts), :]
        v = v_ref[pl.ds(k0, 2 * ts), :]
        dist = (g * ts + lax.broadcasted_iota(jnp.int32, (ts, 1), 0)) - (
            k0 + lax.broadcasted_iota(jnp.int32, (1, 2 * ts), 1))
        mask = jnp.where(dist >= 0, dist, SWA_WINDOW) < SWA_WINDOW
        o = []
        for c, h in enumerate((0, 2, 1, 3)):
            q = _half_heads(q_ref[s * ts:(s + 1) * ts, :])[c]
            sc = jnp.where(mask, _dot_nt(q, k), NEG)
            m = jnp.maximum(jnp.max(sc, axis=-1, keepdims=True), sk[:, h:h + 1])
            p = jnp.exp2(sc - m)
            l = jnp.sum(p, axis=-1, keepdims=True) + jnp.exp2(sk[:, h:h + 1] - m)
            o.append(_dot(p.astype(BF16), v) / l)
        o_ref[s * ts:(s + 1) * ts, :] = jnp.concatenate([_pair(o[0], o[1]), _pair(o[2], o[3])], axis=1).astype(BF16)


def _swa(sq, sk, sv, sinks, batch, seq):
    tq, ts = TILE_SWA
    tq = math.gcd(tq, seq)
    assert ts == SWA_WINDOW and seq >= 2 * ts
    nb = seq // tq
    row = lambda w: pl.BlockSpec((tq, w), lambda b, i: (b * nb + i, 0))
    per_b = lambda w: pl.BlockSpec((seq, w), lambda b, i: (b, 0))
    return pl.pallas_call(
        functools.partial(_swa_kernel, tq=tq, ts=ts),
        grid=(batch, nb),
        in_specs=[row(256), per_b(128), per_b(128), pl.BlockSpec(sinks.shape, lambda b, i: (0, 0))],
        out_specs=row(256),
        out_shape=jax.ShapeDtypeStruct((batch * seq, 256), BF16),
        compiler_params=_params("parallel", "arbitrary"),
        name="swa_attn",
    )(sq, sk, sv, sinks)


def _outproj_kernel(x_ref, mod_ref, ng_ref, oa_ref, ob_ref, oc_ref, od_ref, w_ref, x1_ref, ht_ref):
    acc = _dot(oa_ref[...], w_ref[0:256, :])
    acc = acc + _dot(ob_ref[...], w_ref[256:512, :])
    acc = acc + _dot(oc_ref[...], w_ref[512:768, :])
    acc = acc + _dot(od_ref[...], w_ref[768:1024, :])
    x1 = x_ref[...] + mod_ref[0, 2:3, :] * acc
    x1_ref[...] = x1
    h = _rms(x1, ng_ref[...]) * (1.0 + mod_ref[0, 4:5, :]) + mod_ref[0, 3:4, :]
    ht_ref[...] = h.T.astype(BF16)


def _outproj(x2d, mod_l, norm_g, oa, ob, oc, od, w_out, seq):
    t, d = x2d.shape
    tm = TILE_OUTPROJ_ROWS
    tpb = seq // tm
    row = lambda w: pl.BlockSpec((tm, w), lambda i: (i, 0))
    full = lambda a: pl.BlockSpec(a.shape, lambda i: (0,) * a.ndim)
    return pl.pallas_call(
        _outproj_kernel,
        grid=(t // tm,),
        in_specs=[row(d), pl.BlockSpec((1, 6, d), lambda i: (i // tpb, 0, 0)), full(norm_g),
                  row(256), row(256), row(256), row(256), full(w_out)],
        out_specs=[row(d), pl.BlockSpec((d, tm), lambda i: (0, i))],
        out_shape=[jax.ShapeDtypeStruct((t, d), F32), jax.ShapeDtypeStruct((d, t), BF16)],
        compiler_params=_params("parallel"),
        name="outproj",
    )(x2d, mod_l, norm_g, oa, ob, oc, od, w_out)


_CAND_PIECES = [(0, 0, 8), (0, 8, 8), (1, 0, 8), (2, 0, 5), (3, 0, 4), (4, 0, 3), (5, 0, 2), (6, 0, 2), (7, 0, 2),
                (None, 0, 8)]


_CODE_UNIT = 2.0 ** 114
_TAKEN_BELOW = -(2.0 ** 119)
_INVALID = -(2.0 ** 100)


def _rank_code(r):
    return -(64.0 + r) * _CODE_UNIT


def _top16(s):
    tb = s.shape[1]
    row16 = lax.broadcasted_iota(jnp.int32, (PEER_TOPK, tb), 0)
    vals = jnp.zeros((PEER_TOPK, tb), F32)
    work = s
    for r in range(PEER_TOPK):
        m = jnp.max(work, axis=0, keepdims=True)
        work = jnp.where(work == m, _rank_code(r), work)
        vals = jnp.where(row16 == r, m, vals)
    return vals, work


def _router_head(h, ht, wq_ref, k1_ref, k2_ref, ea_ref, n1_ref, r2_ref, eb_ref):
    tb = ht.shape[1]
    nk = PEER_N_KEYS
    row8 = lax.broadcasted_iota(jnp.int32, (8, tb), 0)
    row16 = lax.broadcasted_iota(jnp.int32, (PEER_TOPK, tb), 0)
    o = pl.multiple_of(h * 2 * nk, 2 * nk)
    q1 = _dot(wq_ref[pl.ds(o, nk), :], ht).astype(BF16)
    q2 = _dot(wq_ref[pl.ds(o + nk, nk), :], ht).astype(BF16)
    s1 = _dot(k1_ref[...], q1)
    s2 = _dot(k2_ref[...], q2)
    v1, code1 = _top16(s1)
    v2, code2 = _top16(s2)
    top = v1[0:1] + v2[0:1]

    def cells(r1, c0):
        return v1[8:16] + v2[0:1] if r1 is None else v1[r1:r1 + 1] + v2[c0:c0 + 8]

    pieces = []
    for r1, c0, valid in _CAND_PIECES:
        p = cells(r1, c0)
        pieces.append(p if valid == 8 else jnp.where(row8 < valid, p, _INVALID))
    for _ in range(PEER_TOPK):
        m = pieces[0]
        for p in pieces[1:]:
            m = jnp.maximum(m, p)
        m = jnp.max(m, axis=0, keepdims=True)
        pieces = [jnp.where(p == m, _rank_code(0), p) for p in pieces]
    counts = jnp.zeros((PEER_TOPK, tb), F32)
    z = jnp.zeros((1, tb), F32)
    for p, (r1, c0, valid) in zip(pieces, _CAND_PIECES):
        taken = p < _TAKEN_BELOW
        if r1 is None:
            counts = counts + jnp.concatenate([jnp.zeros((8, tb), F32), jnp.where(taken, 1.0, 0.0)], axis=0)
        else:
            n = jnp.sum(jnp.where(taken, 1.0, 0.0), axis=0, keepdims=True)
            counts = counts + jnp.where(row16 == r1, n, 0.0)
        z = z + jnp.sum(jnp.where(taken, jnp.exp(cells(r1, c0) - top), 0.0), axis=0, keepdims=True)
    n1 = jnp.zeros((nk, tb), F32)
    for r in range(PEER_TOPK):
        n1 = jnp.where(code1 == _rank_code(r), counts[r:r + 1], n1)
    ea_ref[h] = jnp.exp(s1 - v1[0:1])
    n1_ref[h] = n1
    rank2 = jnp.where(code2 < _TAKEN_BELOW, code2 * (-1.0 / _CODE_UNIT) - 64.0, float(nk))
    r2_ref[h] = rank2.astype(BF16)
    eb_ref[h] = (jnp.exp(s2 - v2[0:1]) / z).astype(BF16)


def _router_kernel(ht_ref, wq_ref, k1_ref, k2_ref, ea_ref, n1_ref, r2_ref, eb_ref):
    ht = ht_ref[...]

    group = 8

    def heads(p, carry):
        for hh in range(group):
            _router_head(group * p + hh, ht, wq_ref, k1_ref, k2_ref, ea_ref, n1_ref, r2_ref, eb_ref)
        return carry

    lax.fori_loop(0, PEER_HEADS // group, heads, 0)


def _router(ht, wq_t, k1, k2):
    d, t = ht.shape
    tb = TILE_ROUTER_TOKENS
    full = lambda a: pl.BlockSpec(a.shape, lambda i: (0,) * a.ndim)
    out = pl.BlockSpec((PEER_HEADS, PEER_N_KEYS, tb), lambda i: (0, 0, i))
    return pl.pallas_call(
        _router_kernel,
        grid=(t // tb,),
        in_specs=[pl.BlockSpec((d, tb), lambda i: (0, i)), full(wq_t), full(k1), full(k2)],
        out_specs=[out, out, out, out],
        out_shape=[jax.ShapeDtypeStruct((PEER_HEADS, PEER_N_KEYS, t), dt) for dt in (F32, F32, BF16, BF16)],
        compiler_params=_params("parallel"),
        name="peer_router",
    )(ht, wq_t, k1, k2)


def _gelu_tanh(x):
    k = 2.0 * math.sqrt(2.0 / math.pi) * math.log2(math.e)
    return x / (1.0 + jnp.exp2(x * (-k - (k * 0.044715) * (x * x))))


def _peer_kernel(ht_ref, u_ref, vt_ref, ea_ref, n1_ref, r2_ref, eb_ref, x_ref, mod_ref, fg_ref, o_ref, acc_ref, wa_ref,
                 wb_ref, *, final, chunk, n_e):
    e = pl.program_id(1)
    nk = PEER_N_KEYS
    n_i1 = u_ref.shape[0] // nk
    sub = PACKED_ROWS
    assert n_i1 == 4

    def step(write_ref, read_ref, base):
        def tokens(c, carry):
            lanes = pl.ds(pl.multiple_of(c * chunk, chunk), chunk)
            if write_ref is not None:
                act = _gelu_tanh(_dot(u_ref[...], ht_ref[:, lanes])).astype(BF16)
            if read_ref is not None:
                acc_ref[:, lanes] += _dot(vt_ref[...], read_ref[:, lanes])
            if write_ref is None:
                return carry
            for j in range(n_i1):
                m = None
                for h in range(PEER_HEADS):
                    row = (h, slice(base + j, base + j + 1), lanes)
                    n_row = jnp.broadcast_to(n1_ref[row], (sub, chunk)).astype(BF16)
                    ea_row = jnp.broadcast_to(ea_ref[row], (sub, chunk)).astype(BF16)
                    r2 = r2_ref[h, :, lanes].reshape(nk // sub, sub, chunk)
                    eb = eb_ref[h, :, lanes].reshape(nk // sub, sub, chunk)
                    term = jnp.where(r2 < n_row[None], eb, jnp.zeros((), BF16)) * ea_row[None]
                    m = term if m is None else m + term
                write_ref[j * nk:(j + 1) * nk, lanes] = m.reshape(nk, chunk) * act[j * nk:(j + 1) * nk]
            return carry

        lax.fori_loop(0, ht_ref.shape[1] // chunk, tokens, 0)

    @pl.when(e == 0)
    def _():
        acc_ref[...] = jnp.zeros_like(acc_ref)
        step(wa_ref, None, 0)

    @pl.when((e > 0) & (e < n_e) & (lax.rem(e, 2) == 0))
    def _():
        step(wa_ref, wb_ref, 0)

    @pl.when((e < n_e) & (lax.rem(e, 2) == 1))
    def _():
        step(wb_ref, wa_ref, n_i1)

    @pl.when(e == n_e)
    def _():
        step(None, wa_ref if n_e % 2 else wb_ref, 0)
        y = x_ref[...] + mod_ref[0, 5:6, :] * acc_ref[...].T
        if final:
            y = _rms(y, fg_ref[...])
        o_ref[...] = y


def _peer(ht, u_bf, vt_bf, ea, n1, r2, eb, x1, mod_l, final_g, seq, final):
    d, t = ht.shape
    n_exp = u_bf.shape[0]
    tb = TILE_PEER_TOKENS if seq % TILE_PEER_TOKENS == 0 else TILE_PEER_TOKENS // 2
    eb_blk = TILE_PEER_EXPERTS
    n_e = n_exp // eb_blk
    n_i1 = eb_blk // PEER_N_KEYS
    tpb = seq // tb
    cur = lambda e: jnp.minimum(e, n_e - 1)
    i1_spec = pl.BlockSpec((PEER_HEADS, 8, tb), lambda i, e: (0, cur(e) // (8 // n_i1), i))
    tok3 = pl.BlockSpec((PEER_HEADS, PEER_N_KEYS, tb), lambda i, e: (0, 0, i))
    return pl.pallas_call(
        functools.partial(_peer_kernel, final=final, chunk=min(TILE_PEER_CHUNK, tb), n_e=n_e),
        grid=(t // tb, n_e + 1),
        in_specs=[pl.BlockSpec((d, tb), lambda i, e: (0, i)),
                  pl.BlockSpec((eb_blk, d), lambda i, e: (cur(e), 0)),
                  pl.BlockSpec((d, eb_blk), lambda i, e: (0, jnp.maximum(e - 1, 0))),
                  i1_spec, i1_spec, tok3, tok3,
                  pl.BlockSpec((tb, d), lambda i, e: (i, 0)),
                  pl.BlockSpec((1, 6, d), lambda i, e: (i // tpb, 0, 0)),
                  pl.BlockSpec(final_g.shape, lambda i, e: (0, 0))],
        out_specs=pl.BlockSpec((tb, d), lambda i, e: (i, 0)),
        out_shape=jax.ShapeDtypeStruct((t, d), F32),
        scratch_shapes=[pltpu.VMEM((d, tb), F32), pltpu.VMEM((eb_blk, tb), BF16), pltpu.VMEM((eb_blk, tb), BF16)],
        compiler_params=_params("parallel", "arbitrary"),
        name="peer_experts",
    )(ht, u_bf, vt_bf, ea, n1, r2, eb, x1, mod_l, final_g)


def kernel(x, c, ada_w, ada_b, norm_mix_g, norm_ffn_g, w_in, nsa_cmp_pos_k, nsa_cmp_pos_v, nsa_cmp_wk, nsa_cmp_wv, diff_lam_q1, diff_lam_k1, diff_lam_q2, diff_lam_k2, diff_sub_g, mla_q_norm_g, mla_w_uq, mla_kv_norm_g, mla_w_ukv, swa_sinks, w_out, peer_w_q, peer_sub_k1, peer_sub_k2, peer_u, peer_v, final_g):
    batch, seq, d = x.shape
    depth = w_in.shape[0]
    assert seq % 512 == 0
    x2d = x.reshape(batch * seq, d)
    mod = _adaln(c, ada_w, ada_b).reshape(depth, batch, 6, d)
    table = _rope_table(seq)
    mixw = 4 * HEAD_DIM
    swa_rows = 3 * mixw + np.concatenate([h * HEAD_DIM + np.arange(HEAD_DIM) for h in (0, 2, 1, 3)])
    out_rows = jnp.asarray(np.concatenate([np.arange(3 * mixw), swa_rows]), jnp.int32)
    pad128 = lambda v: jnp.pad(v, (0, LANES - v.shape[0])).reshape(1, LANES)
    fg = final_g.reshape(1, d)
    w_big = _take_cols(w_in, _IN_IDX, _IN_SGN)
    wuq = _take_cols(mla_w_uq, _UQ_IDX, _UQ_SGN)
    wukv = _take_cols(mla_w_ukv, _UKV_IDX, _UKV_SGN)
    for l in range(depth):
        (nq, nk, nv, ng, dq, dk, dv, mq, mk, mv, sq, sk, sv) = _inproj(
            x2d, mod[l], norm_mix_g[l].reshape(1, d), w_big[l], table, wuq[l], wukv[l],
            mla_q_norm_g[l].reshape(1, -1), mla_kv_norm_g[l].reshape(1, -1), seq)
        ocmp, sel = _nsa_cmp(nq, nk, nv, nsa_cmp_wk[l], nsa_cmp_wv[l], nsa_cmp_pos_k[l], nsa_cmp_pos_v[l], batch, seq)
        o_a = _nsa(nq, nk, nv, sel, ng, ocmp, batch, seq)
        lamv = jnp.concatenate([pad128(diff_lam_q1[l]), pad128(diff_lam_k1[l]),
                                pad128(diff_lam_q2[l]), pad128(diff_lam_k2[l])], axis=0)
        sub_g2 = jnp.concatenate([diff_sub_g[l], diff_sub_g[l]]).reshape(1, LANES)
        o_b = _diff(dq, dk, dv, lamv, sub_g2, l, batch, seq)
        o_c = _mla(mq, mk, mv, batch, seq)
        o_d = _swa(sq, sk, sv, pad128(swa_sinks[l]), batch, seq)
        w_o = jnp.take(w_out[l], out_rows, axis=0).astype(BF16)
        x1, ht = _outproj(x2d, mod[l], norm_ffn_g[l].reshape(1, d), o_a, o_b, o_c, o_d, w_o, seq)
        ea, n1, r2, eb = _router(ht, peer_w_q[l].T.astype(BF16), peer_sub_k1[l].astype(BF16),
                                 peer_sub_k2[l].astype(BF16))
        x2d = _peer(ht, peer_u[l].astype(BF16), peer_v[l].T.astype(BF16), ea, n1, r2, eb, x1, mod[l], fg,
                    seq, final=(l == depth - 1))
    return x2d.reshape(batch, seq, d)
```
